```python
import math
import jax, jax.numpy as jnp
from jax import lax
import numpy as np

D_MODEL = 2048
BATCH = 1
SEQ = 16384
DEPTH = 2

GRID_W = 64
CTX_LEN = 256
BLOCK = 128
A_GROUPS = 4
A_CH = 128
A_WIDTH = A_GROUPS * A_CH
B_WIDTH = 512
C_HEADS = 4
C_HD = 64
C_VD = 2 * C_HD
C_QK_W = C_HEADS * 2 * C_HD
C_WIDTH = C_HEADS * C_VD
D_HEADS = 8
D_KV_HEADS = 2
D_HD = 64
D_WIDTH = D_HEADS * D_HD
D_KV_W = D_KV_HEADS * D_HD
WINDOW = 128
N_BRANCH = 4
BRANCH_W = 512
PROJ_SIZES = (A_WIDTH, A_WIDTH, B_WIDTH, B_WIDTH, B_WIDTH, C_QK_W, C_QK_W, C_WIDTH, D_WIDTH, D_KV_W, D_KV_W, N_BRANCH * D_MODEL)
D_PROJ = sum(PROJ_SIZES)
N_EXPERTS = 32
TOP_K = 4
D_FF = 1024
SWIGLU_LIMIT = 7.0
SWIGLU_ALPHA = 1.702
ROPE_DIM = 64
ROPE_BASE = 10000.0
LN_EPS = 1e-5
DN_ALPHA = (2 * DEPTH) ** 0.25
DN_BETA = (8 * DEPTH) ** -0.25

kernel_name = "hybrid_gated_diffusion_trunk"


def layer_norm(x, g, b):
    xf = x.astype(jnp.float32)
    mu = jnp.mean(xf, -1, keepdims=True)
    var = jnp.mean(jnp.square(xf - mu), -1, keepdims=True)
    return ((xf - mu) * lax.rsqrt(var + LN_EPS)).astype(x.dtype) * g + b


def rms_norm(x, g):
    xf = x.astype(jnp.float32)
    return (xf * lax.rsqrt(jnp.mean(xf * xf, -1, keepdims=True) + LN_EPS)).astype(x.dtype) * g


def axial_angles(n_tok, head_dim):
    n_freq = head_dim // 4
    inv = ROPE_BASE ** (-jnp.arange(n_freq, dtype=jnp.float32) / n_freq)
    t = jnp.arange(n_tok)
    row = (t // GRID_W).astype(jnp.float32)
    col = (t % GRID_W).astype(jnp.float32)
    return row[:, None] * inv, col[:, None] * inv


def rotate_half(x, ang):
    x1, x2 = jnp.split(x, 2, axis=-1)
    cs = jnp.cos(ang).astype(x.dtype)
    sn = jnp.sin(ang).astype(x.dtype)
    return jnp.concatenate([x1 * cs - x2 * sn, x2 * cs + x1 * sn], axis=-1)


def axial_rope(x, ang_row, ang_col):
    shape = (1, ang_row.shape[0]) + (1,) * (x.ndim - 3) + (ang_row.shape[1],)
    xr, xc = jnp.split(x, 2, axis=-1)
    return jnp.concatenate([rotate_half(xr, ang_row.reshape(shape)), rotate_half(xc, ang_col.reshape(shape))], axis=-1)


def chunk_mlp(u, v, ln_g, ln_b, w_s, b_s):
    bsz, n, _ = u.shape
    u = jax.nn.gelu(u)
    v = layer_norm(jax.nn.gelu(v), ln_g, ln_b)
    vc = v.reshape(bsz, n // BLOCK, BLOCK, A_GROUPS, A_CH)
    mixed = jnp.einsum('gpq,bnqgc->bnpgc', w_s, vc) + b_s.T[None, None, :, :, None]
    return u * mixed.reshape(bsz, n, A_WIDTH)


def short_conv(xin, gate_b, gate_c, conv_w, conv_b):
    z = gate_c * xin
    n = z.shape[1]
    zp = jnp.pad(z, ((0, 0), (1, 1), (0, 0)))
    y = conv_w[0] * zp[:, :n] + conv_w[1] * zp[:, 1:n + 1] + conv_w[2] * zp[:, 2:] + conv_b
    return gate_b * y


def diff_attend(q, k, v, lam):
    s = jnp.einsum('bqhcd,bkhcd->bhcqk', q, k).astype(jnp.float32) * (C_HD ** -0.5)
    p = jax.nn.softmax(s, axis=-1)
    a = p[:, :, 0] - lam * p[:, :, 1]
    return jnp.einsum('bhqk,bkhe->bqhe', a.astype(v.dtype), v)


def diff_post(o, subln_g, lam_init):
    bsz, n = o.shape[:2]
    return (rms_norm(o, subln_g) * (1.0 - lam_init)).reshape(bsz, n, C_WIDTH)


def window_attend_latent(q, k, v, k_ctx, v_ctx, sink):
    bsz, n = q.shape[:2]
    nblk = n // BLOCK
    grp = D_HEADS // D_KV_HEADS
    qb = q.reshape(bsz, nblk, BLOCK, D_KV_HEADS, grp, D_HD)

    def band(t):
        tp = jnp.pad(t, ((0, 0), (BLOCK, BLOCK), (0, 0), (0, 0))).reshape(bsz, nblk + 2, BLOCK, D_KV_HEADS, D_HD)
        return jnp.concatenate([tp[:, :-2], tp[:, 1:-1], tp[:, 2:]], axis=2)

    kb, vb = band(k), band(v)
    scale = D_HD ** -0.5
    s_loc = jnp.einsum('bnqkgd,bnjkd->bnkgqj', qb, kb).astype(jnp.float32) * scale
    s_ctx = jnp.einsum('bnqkgd,bjkd->bnkgqj', qb, k_ctx).astype(jnp.float32) * scale
    blk = jnp.arange(nblk)[:, None, None]
    qpos = blk * BLOCK + jnp.arange(BLOCK)[None, :, None]
    kpos = (blk - 1) * BLOCK + jnp.arange(3 * BLOCK)[None, None, :]
    valid = (jnp.abs(kpos - qpos) <= WINDOW) & (kpos >= 0) & (kpos < n)
    s_loc = jnp.where(valid[None, :, None, None], s_loc, -jnp.inf)
    s_sink = jnp.broadcast_to(sink.astype(jnp.float32).reshape(1, 1, D_KV_HEADS, grp, 1, 1), s_loc.shape[:-1] + (1,))
    p = jax.nn.softmax(jnp.concatenate([s_loc, s_ctx, s_sink], axis=-1), axis=-1).astype(v.dtype)
    n_loc = 3 * BLOCK
    n_ctx = k_ctx.shape[1]
    o = (jnp.einsum('bnkgqj,bnjkd->bnqkgd', p[..., :n_loc], vb)
         + jnp.einsum('bnkgqj,bjkd->bnqkgd', p[..., n_loc:n_loc + n_ctx], v_ctx))
    return o.reshape(bsz, n, D_WIDTH)


def window_attend_ctx(q, k, v, sink):
    bsz, m = q.shape[:2]
    grp = D_HEADS // D_KV_HEADS
    qg = q.reshape(bsz, m, D_KV_HEADS, grp, D_HD)
    s = jnp.einsum('bqkgd,bjkd->bkgqj', qg, k).astype(jnp.float32) * (D_HD ** -0.5)
    s_sink = jnp.broadcast_to(sink.astype(jnp.float32).reshape(1, D_KV_HEADS, grp, 1, 1), s.shape[:-1] + (1,))
    p = jax.nn.softmax(jnp.concatenate([s, s_sink], axis=-1), axis=-1)[..., :m].astype(v.dtype)
    o = jnp.einsum('bkgqj,bjkd->bqkgd', p, v)
    return o.reshape(bsz, m, D_WIDTH)


def gated_merge(branches, gate_logits, b_gate, w_br, w_out):
    bsz, n = gate_logits.shape[:2]
    gates = jax.nn.sigmoid(gate_logits.reshape(bsz, n, N_BRANCH, D_MODEL) + b_gate)
    merged = gates[:, :, 0] * (branches[0] @ w_br[0])
    for g in range(1, N_BRANCH):
        merged = merged + gates[:, :, g] * (branches[g] @ w_br[g])
    return merged @ w_out


def hybrid_mixer(h, hc, lp, lam_init, ang_row, ang_col, need_ctx):
    bsz, n, _ = h.shape
    m = hc.shape[1]
    splits = np.cumsum(PROJ_SIZES)[:-1].tolist()
    (a_u, a_v, b_x, b_b, b_c, c_q, c_k, c_v, d_q, d_k, d_v, gl) = jnp.split(h @ lp['w_in'], splits, axis=-1)
    (ca_u, ca_v, cb_x, cb_b, cb_c, cc_q, cc_k, cc_v, cd_q, cd_k, cd_v, cgl) = jnp.split(hc @ lp['w_in'], splits, axis=-1)

    br_a = chunk_mlp(a_u, a_v, lp['a_ln_g'], lp['a_ln_b'], lp['a_ws'], lp['a_bs'])
    br_b = short_conv(b_x, b_b, b_c, lp['b_conv_w'], lp['b_conv_b'])

    lam = (jnp.exp(jnp.sum(lp['c_lq1'] * lp['c_lk1'])) - jnp.exp(jnp.sum(lp['c_lq2'] * lp['c_lk2'])) + lam_init).astype(jnp.float32)
    ql = axial_rope(c_q.reshape(bsz, n, C_HEADS, 2, C_HD), ang_row, ang_col)
    kl = axial_rope(c_k.reshape(bsz, n, C_HEADS, 2, C_HD), ang_row, ang_col)
    vl = c_v.reshape(bsz, n, C_HEADS, C_VD)
    kc = cc_k.reshape(bsz, m, C_HEADS, 2, C_HD)
    vc = cc_v.reshape(bsz, m, C_HEADS, C_VD)
    k_all = jnp.concatenate([kl, kc], axis=1)
    v_all = jnp.concatenate([vl, vc], axis=1)
    qb = ql.reshape(bsz, n // BLOCK, BLOCK, C_HEADS, 2, C_HD).swapaxes(0, 1)
    o_c = lax.map(lambda q_blk: diff_attend(q_blk, k_all, v_all, lam), qb)
    br_c = diff_post(o_c.swapaxes(0, 1).reshape(bsz, n, C_HEADS, C_VD), lp['c_subln_g'], lam_init)

    dql = axial_rope(d_q.reshape(bsz, n, D_HEADS, D_HD), ang_row, ang_col)
    dkl = axial_rope(d_k.reshape(bsz, n, D_KV_HEADS, D_HD), ang_row, ang_col)
    dvl = d_v.reshape(bsz, n, D_KV_HEADS, D_HD)
    dkc = cd_k.reshape(bsz, m, D_KV_HEADS, D_HD)
    dvc = cd_v.reshape(bsz, m, D_KV_HEADS, D_HD)
    br_d = window_attend_latent(dql, dkl, dvl, dkc, dvc, lp['d_sink'])

    y = gated_merge([br_a, br_b, br_c, br_d], gl, lp['b_gate'], lp['w_br'], lp['w_out'])
    if not need_ctx:
        return y, None

    cbr_a = chunk_mlp(ca_u, ca_v, lp['a_ln_g'], lp['a_ln_b'], lp['a_ws'], lp['a_bs'])
    cbr_b = short_conv(cb_x, cb_b, cb_c, lp['b_conv_w'], lp['b_conv_b'])
    qc = cc_q.reshape(bsz, m, C_HEADS, 2, C_HD)
    cbr_c = diff_post(diff_attend(qc, kc, vc, lam), lp['c_subln_g'], lam_init)
    cbr_d = window_attend_ctx(cd_q.reshape(bsz, m, D_HEADS, D_HD), dkc, dvc, lp['d_sink'])
    yc = gated_merge([cbr_a, cbr_b, cbr_c, cbr_d], cgl, lp['b_gate'], lp['w_br'], lp['w_out'])
    return y, yc


def moe(h, w_router, b_router, w_gu, b_gu, w_down, b_down):
    shp = h.shape
    t = h.reshape(-1, D_MODEL)
    n_tok = t.shape[0]
    logits = (t @ w_router + b_router).astype(jnp.float32)
    top_val, top_idx = lax.top_k(logits, TOP_K)
    top_w = jax.nn.softmax(top_val, axis=-1).astype(h.dtype)
    n_pairs = n_tok * TOP_K
    e_flat = top_idx.reshape(-1)
    tok_flat = jnp.arange(n_pairs, dtype=jnp.int32) // TOP_K
    order = jnp.argsort(e_flat)
    e_sorted = e_flat[order]
    tok_sorted = tok_flat[order]
    w_sorted = top_w.reshape(-1)[order]
    counts = jnp.bincount(e_flat, length=N_EXPERTS)
    padded = (counts + BLOCK - 1) // BLOCK * BLOCK
    starts = jnp.cumsum(counts) - counts
    pends = jnp.cumsum(padded)
    pstarts = pends - padded
    dest = pstarts[e_sorted] + (jnp.arange(n_pairs) - starts[e_sorted])
    n_rows = (n_pairs + BLOCK - 1) // BLOCK * BLOCK + N_EXPERTS * BLOCK
    n_blk = n_rows // BLOCK
    row_tok = jnp.zeros((n_rows,), jnp.int32).at[dest].set(tok_sorted)
    row_w = jnp.zeros((n_rows,), h.dtype).at[dest].set(w_sorted)
    blk_expert = jnp.minimum(jnp.searchsorted(pends, jnp.arange(n_blk) * BLOCK, side='right'), N_EXPERTS - 1)

    def run_block(args):
        e, toks, wts = args
        gu = t[toks] @ w_gu[e] + b_gu[e]
        g, u = jnp.split(gu, 2, axis=-1)
        g = jnp.minimum(g, SWIGLU_LIMIT)
        u = jnp.clip(u, -SWIGLU_LIMIT, SWIGLU_LIMIT)
        act = (u + 1.0) * (g * jax.nn.sigmoid(SWIGLU_ALPHA * g))
        return (act @ w_down[e] + b_down[e]) * wts[:, None]

    rows_out = lax.map(run_block, (blk_expert, row_tok.reshape(n_blk, BLOCK), row_w.reshape(n_blk, BLOCK)))
    y = jax.ops.segment_sum(rows_out.reshape(n_rows, D_MODEL), row_tok, num_segments=n_tok)
    return y.reshape(shp)


def setup_inputs(seed: int = 0) -> dict:
    key = jax.random.key(seed)
    ks = iter(jax.random.split(key, 40))
    L = DEPTH

    def nrm(shape, scale):
        return jax.random.normal(next(ks), shape, jnp.float32) * scale

    return {
        'x': nrm((BATCH, SEQ, D_MODEL), 1.0),
        'c': nrm((BATCH, D_MODEL), 1.0),
        'ctx': nrm((BATCH, CTX_LEN, D_MODEL), 1.0),
        'c_ctx': nrm((D_MODEL,), 1.0),
        'w_ada': nrm((L, D_MODEL, 6 * D_MODEL), 0.5 * D_MODEL ** -0.5),
        'b_ada': nrm((L, 6 * D_MODEL), 0.02),
        'w_in': nrm((L, D_MODEL, D_PROJ), D_MODEL ** -0.5),
        'b_gate': nrm((L, N_BRANCH, D_MODEL), 0.1),
        'a_ln_g': 1.0 + nrm((L, A_WIDTH), 0.02),
        'a_ln_b': nrm((L, A_WIDTH), 0.02),
        'a_ws': nrm((L, A_GROUPS, BLOCK, BLOCK), BLOCK ** -0.5),
        'a_bs': 1.0 + nrm((L, A_GROUPS, BLOCK), 0.02),
        'b_conv_w': nrm((L, 3, B_WIDTH), 3 ** -0.5),
        'b_conv_b': nrm((L, B_WIDTH), 0.02),
        'c_lq1': nrm((L, C_HD), 0.1),
        'c_lk1': nrm((L, C_HD), 0.1),
        'c_lq2': nrm((L, C_HD), 0.1),
        'c_lk2': nrm((L, C_HD), 0.1),
        'c_subln_g': 1.0 + nrm((L, C_VD), 0.02),
        'd_sink': nrm((L, D_HEADS), 0.5),
        'w_br': nrm((L, N_BRANCH, BRANCH_W, D_MODEL), BRANCH_W ** -0.5),
        'w_out': nrm((L, D_MODEL, D_MODEL), DN_BETA * D_MODEL ** -0.5),
        'ln1_g': 1.0 + nrm((L, D_MODEL), 0.02),
        'ln1_b': nrm((L, D_MODEL), 0.02),
        'w_router': nrm((L, D_MODEL, N_EXPERTS), D_MODEL ** -0.5),
        'b_router': nrm((L, N_EXPERTS), 0.01),
        'e_w_gu': nrm((L, N_EXPERTS, D_MODEL, 2 * D_FF), D_MODEL ** -0.5),
        'e_b_gu': nrm((L, N_EXPERTS, 2 * D_FF), 0.02),
        'e_w_down': nrm((L, N_EXPERTS, D_FF, D_MODEL), DN_BETA * D_FF ** -0.5),
        'e_b_down': nrm((L, N_EXPERTS, D_MODEL), 0.02),
        'ln2_g': 1.0 + nrm((L, D_MODEL), 0.02),
        'ln2_b': nrm((L, D_MODEL), 0.02),
    }


def reference(x, c, ctx, c_ctx, w_ada, b_ada, w_in, b_gate, a_ln_g, a_ln_b, a_ws, a_bs, b_conv_w, b_conv_b,
              c_lq1, c_lk1, c_lq2, c_lk2, c_subln_g, d_sink, w_br, w_out, ln1_g, ln1_b, w_router, b_router,
              e_w_gu, e_b_gu, e_w_down, e_b_down, ln2_g, ln2_b):
    n = x.shape[1]
    ang_row, ang_col = axial_angles(n, ROPE_DIM)
    xc = ctx
    for l in range(DEPTH):
        need_ctx = l < DEPTH - 1
        lam_init = 0.8 - 0.6 * math.exp(-0.3 * l)
        lp = {'w_in': w_in[l], 'b_gate': b_gate[l], 'a_ln_g': a_ln_g[l], 'a_ln_b': a_ln_b[l], 'a_ws': a_ws[l],
              'a_bs': a_bs[l], 'b_conv_w': b_conv_w[l], 'b_conv_b': b_conv_b[l], 'c_lq1': c_lq1[l], 'c_lk1': c_lk1[l],
              'c_lq2': c_lq2[l], 'c_lk2': c_lk2[l], 'c_subln_g': c_subln_g[l], 'd_sink': d_sink[l],
              'w_br': w_br[l], 'w_out': w_out[l]}
        mod = jax.nn.silu(c) @ w_ada[l] + b_ada[l]
        mod_c = jax.nn.silu(c_ctx) @ w_ada[l] + b_ada[l]
        sh1, sc1, g1, sh2, sc2, g2 = [mm[:, None] for mm in jnp.split(mod, 6, axis=-1)]
        csh1, csc1, cg1, csh2, csc2, cg2 = jnp.split(mod_c, 6, axis=-1)

        h = x * (1.0 + sc1) + sh1
        hc = xc * (1.0 + csc1) + csh1
        y, yc = hybrid_mixer(h, hc, lp, lam_init, ang_row, ang_col, need_ctx)
        x = layer_norm(DN_ALPHA * x + g1 * y, ln1_g[l], ln1_b[l])
        h = x * (1.0 + sc2) + sh2
        f = moe(h, w_router[l], b_router[l], e_w_gu[l], e_b_gu[l], e_w_down[l], e_b_down[l])
        x = layer_norm(DN_ALPHA * x + g2 * f, ln2_g[l], ln2_b[l])

        if need_ctx:
            xc = layer_norm(DN_ALPHA * xc + cg1 * yc, ln1_g[l], ln1_b[l])
            hc = xc * (1.0 + csc2) + csh2
            fc = moe(hc, w_router[l], b_router[l], e_w_gu[l], e_b_gu[l], e_w_down[l], e_b_down[l])
            xc = layer_norm(DN_ALPHA * xc + cg2 * fc, ln2_g[l], ln2_b[l])
    return x
```

```python
import functools
import math

import jax
import jax.numpy as jnp
from jax import lax
from jax.experimental import pallas as pl
from jax.experimental.pallas import tpu as pltpu

BF = jnp.bfloat16
F32 = jnp.float32

D_MODEL = 2048
DEPTH = 2
GRID_W = 64
BLOCK = 128
A_GROUPS = 4
C_HEADS = 4
C_HD = 64
C_VD = 128
D_HEADS = 8
D_KV_HEADS = 2
D_HD = 64
WINDOW = 128
N_BRANCH = 4
BRANCH_W = 512
PROJ_SIZES = (512, 512, 512, 512, 512, 512, 512, 512, 512, 128, 128, N_BRANCH * D_MODEL)
N_EXPERTS = 32
TOP_K = 4
D_FF = 1024
SWIGLU_LIMIT = 7.0
SWIGLU_ALPHA = 1.702
ROPE_DIM = 64
ROPE_BASE = 10000.0
LN_EPS = 1e-5
DN_ALPHA = (2 * DEPTH) ** 0.25

LANE = 128
NEG_BIG = -1e30

GL_W = N_BRANCH * D_MODEL
PROJ_TN = 512
P_W = GL_W + 9 * 512 + 512
COL_AU, COL_AV, COL_BX, COL_BB, COL_BC, COL_CQ, COL_CK, COL_CV, COL_DQ = [GL_W + 512 * s for s in range(9)]
COL_DK = GL_W + 9 * 512
COL_DV = COL_DK + 128
ROPE_FULL_TILES = (COL_CQ // PROJ_TN, COL_CK // PROJ_TN, COL_DQ // PROJ_TN)
ROPE_PART_TILE = COL_DK // PROJ_TN

MOE_BLK = 256


def _cparams(dims, vmem_mib):
    return pltpu.CompilerParams(dimension_semantics=dims, vmem_limit_bytes=vmem_mib * 1024 * 1024)


def _const_spec(shape):
    nd = len(shape)
    return pl.BlockSpec(shape, lambda *_: (0,) * nd, pipeline_mode=pl.Buffered(1))


def _layer_norm_rows(r, g, b):
    mu = jnp.mean(r, axis=-1, keepdims=True)
    d = r - mu
    var = jnp.mean(d * d, axis=-1, keepdims=True)
    return d * lax.rsqrt(var + LN_EPS) * g + b


def _ada_kernel(c_ref, w_ref, b_ref, o_ref):
    cs = c_ref[...]
    s = cs * jax.nn.sigmoid(cs)
    o_ref[0] = jnp.dot(s.astype(BF), w_ref[0].astype(BF), preferred_element_type=F32) + b_ref[0]


def _ada(c_rows, w_ada, b_ada):
    n_l, _, n_out = w_ada.shape
    tn = 1536
    return pl.pallas_call(
        _ada_kernel,
        grid=(n_l, n_out // tn),
        in_specs=[pl.BlockSpec((8, D_MODEL), lambda l, j: (0, 0)),
                  pl.BlockSpec((1, D_MODEL, tn), lambda l, j: (l, 0, j)),
                  pl.BlockSpec((1, 1, tn), lambda l, j: (l, 0, j))],
        out_specs=pl.BlockSpec((1, 8, tn), lambda l, j: (l, 0, j)),
        out_shape=jax.ShapeDtypeStruct((n_l, 8, n_out), F32),
        compiler_params=_cparams(("arbitrary", "arbitrary"), 40),
        name="ada",
    )(c_rows, w_ada, b_ada.reshape(n_l, 1, n_out))


def _rope_rotate(a, cos_ref, sin_ref):
    w = a.shape[1]
    lane = lax.broadcasted_iota(jnp.int32, a.shape, 1)
    first = jnp.bitwise_and(lane, 16) == 0
    swapped = jnp.where(first, pltpu.roll(a, w - 16, 1), pltpu.roll(a, 16, 1))
    reps = w // LANE
    cos = cos_ref[...]
    sin = sin_ref[...]
    if reps > 1:
        cos = jnp.concatenate([cos] * reps, axis=1)
        sin = jnp.concatenate([sin] * reps, axis=1)
    return a * cos + swapped * sin


def _proj_kernel(x_ref, sc_ref, sh_ref, w_ref, cos_ref, sin_ref, o_ref, h_scr, *, rope):
    j = pl.program_id(1)

    @pl.when(j == 0)
    def _():
        h_scr[...] = (x_ref[...] * (1.0 + sc_ref[...]) + sh_ref[...]).astype(BF)

    acc = jnp.dot(h_scr[...], w_ref[...], preferred_element_type=F32)
    if not rope:
        o_ref[...] = acc.astype(BF)
        return

    full = functools.reduce(jnp.logical_or, [j == t for t in ROPE_FULL_TILES])
    part = j == ROPE_PART_TILE

    @pl.when(full)
    def _():
        o_ref[...] = _rope_rotate(acc, cos_ref, sin_ref).astype(BF)

    @pl.when(part)
    def _():
        o_ref[:, :LANE] = _rope_rotate(acc[:, :LANE], cos_ref, sin_ref).astype(BF)
        o_ref[:, LANE:] = acc[:, LANE:].astype(BF)

    @pl.when(jnp.logical_not(jnp.logical_or(full, part)))
    def _():
        o_ref[...] = acc.astype(BF)


def _proj(x2d, sc, sh, w_r, cos_t, sin_t, *, rope, tm):
    n = x2d.shape[0]
    kern = functools.partial(_proj_kernel, rope=rope)
    return pl.pallas_call(
        kern,
        grid=(n // tm, P_W // PROJ_TN),
        in_specs=[pl.BlockSpec((tm, D_MODEL), lambda i, j: (i, 0)),
                  pl.BlockSpec((1, D_MODEL), lambda i, j: (0, 0)),
                  pl.BlockSpec((1, D_MODEL), lambda i, j: (0, 0)),
                  pl.BlockSpec((D_MODEL, PROJ_TN), lambda i, j: (0, j)),
                  pl.BlockSpec((tm, LANE), lambda i, j: (i, 0)),
                  pl.BlockSpec((tm, LANE), lambda i, j: (i, 0))],
        out_specs=pl.BlockSpec((tm, PROJ_TN), lambda i, j: (i, j)),
        out_shape=jax.ShapeDtypeStruct((n, P_W), BF),
        scratch_shapes=[pltpu.VMEM((tm, D_MODEL), BF)],
        compiler_params=_cparams(("arbitrary", "arbitrary"), 48),
        name="proj_rope" if rope else "proj_ctx",
    )(x2d, sc, sh, w_r, cos_t, sin_t)


def _diff_attn_kernel(lam_ref, qt_ref, k_ref, vt_ref, g_ref, o_ref, m_scr, l_scr, acc_scr, *,
                      tq, tk, n_chunks, post_scale):
    qt = qt_ref[...].astype(F32) * (C_HD ** -0.5)
    row = lax.broadcasted_iota(jnp.int32, qt.shape, 0)
    qbd = jnp.concatenate([jnp.where(row < C_HD, qt, 0.0), jnp.where(row >= C_HD, qt, 0.0)], axis=1).astype(BF)

    m_scr[...] = jnp.full(m_scr.shape, -jnp.inf, F32)
    l_scr[...] = jnp.zeros(l_scr.shape, F32)
    acc_scr[...] = jnp.zeros(acc_scr.shape, F32)

    def body(c, carry):
        k = k_ref[pl.ds(pl.multiple_of(c * tk, tk), tk), :]
        s = jnp.dot(k, qbd, preferred_element_type=F32)
        m_old = m_scr[...]
        m_new = jnp.maximum(m_old, jnp.max(s, axis=0, keepdims=True))
        alpha = jnp.exp(m_old - m_new)
        p = jnp.exp(s - m_new)
        l_scr[...] = alpha * l_scr[...] + jnp.sum(p, axis=0, keepdims=True)
        acc_scr[...] = acc_scr[...] * alpha + jnp.dot(vt_ref[0, c], p.astype(BF), preferred_element_type=F32)
        m_scr[...] = m_new
        return carry

    lax.fori_loop(0, n_chunks, body, 0)

    o = acc_scr[...] / l_scr[...]
    od = o[:, :tq] - lam_ref[0] * o[:, tq:]
    ms = jnp.mean(od * od, axis=0, keepdims=True)
    on = od * lax.rsqrt(ms + LN_EPS) * g_ref[...] * post_scale
    o_ref[...] = on.T.astype(BF)


def _pick_tk(n_k):
    for tk in (1280, 1024, 768, 512, 256, 128):
        if n_k % tk == 0:
            return tk
    raise ValueError(f"unsupported key count {n_k}")


def _diff_attn(lam, q, k_all, v_all, subln_g, lam_init):
    n = q.shape[0]
    n_k = k_all.shape[0]
    tq = 256
    tk = _pick_tk(n_k)
    n_chunks = n_k // tk
    qt = q.T
    vt = v_all.reshape(n_chunks, tk, C_HEADS, C_VD).transpose(2, 0, 3, 1)
    kern = functools.partial(_diff_attn_kernel, tq=tq, tk=tk, n_chunks=n_chunks, post_scale=1.0 - lam_init)
    return pl.pallas_call(
        kern,
        grid=(C_HEADS, n // tq),
        in_specs=[pl.BlockSpec(memory_space=pltpu.SMEM),
                  pl.BlockSpec((C_VD, tq), lambda h, i: (h, i)),
                  pl.BlockSpec((n_k, C_VD), lambda h, i: (0, h)),
                  pl.BlockSpec((1, n_chunks, C_VD, tk), lambda h, i: (h, 0, 0, 0)),
                  pl.BlockSpec((C_VD, 1), lambda h, i: (0, 0))],
        out_specs=pl.BlockSpec((tq, C_VD), lambda h, i: (i, h)),
        out_shape=jax.ShapeDtypeStruct((n, C_HEADS * C_VD), BF),
        scratch_shapes=[pltpu.VMEM((1, 2 * tq), F32), pltpu.VMEM((1, 2 * tq), F32),
                        pltpu.VMEM((C_VD, 2 * tq), F32)],
        compiler_params=_cparams(("arbitrary", "arbitrary"), 48),
        name="diff_attn",
    )(lam, qt, k_all, vt, subln_g.reshape(C_VD, 1))


def _win_heads(q, kb, vb, valid, sink_ref):
    outs = []
    grp = D_HEADS // D_KV_HEADS
    for h in range(D_HEADS):
        kh = h // grp
        qh = q[:, h * D_HD:(h + 1) * D_HD]
        k_h = kb[:, kh * D_HD:(kh + 1) * D_HD]
        v_h = vb[:, kh * D_HD:(kh + 1) * D_HD]
        s = lax.dot_general(qh, k_h, (((1,), (1,)), ((), ())), preferred_element_type=F32) * (D_HD ** -0.5)
        if valid is not None:
            s = jnp.where(valid, s, NEG_BIG)
        sk = sink_ref[h]
        m = jnp.maximum(jnp.max(s, axis=-1, keepdims=True), sk)
        e = jnp.exp(s - m)
        l = jnp.sum(e, axis=-1, keepdims=True) + jnp.exp(sk - m)
        p = (e / l).astype(BF)
        outs.append(jnp.dot(p, v_h, preferred_element_type=F32))
    return jnp.concatenate(outs, axis=1).astype(BF)


def _win_attn_kernel(sink_ref, q_ref, kp_ref, kc_ref, kn_ref, vp_ref, vc_ref, vn_ref, kx_ref, vx_ref, o_ref, *,
                     n_tok, n_ctx):
    i = pl.program_id(0)
    kb = jnp.concatenate([kp_ref[...], kc_ref[...], kn_ref[...], kx_ref[...]], axis=0)
    vb = jnp.concatenate([vp_ref[...], vc_ref[...], vn_ref[...], vx_ref[...]], axis=0)
    n_keys = 3 * BLOCK + n_ctx
    r = lax.broadcasted_iota(jnp.int32, (BLOCK, n_keys), 0)
    j = lax.broadcasted_iota(jnp.int32, (BLOCK, n_keys), 1)
    kpos = (i - 1) * BLOCK + j
    dist = j - BLOCK - r
    in_band = (jnp.abs(dist) <= WINDOW) & (kpos >= 0) & (kpos < n_tok)
    valid = (j >= 3 * BLOCK) | in_band
    o_ref[...] = _win_heads(q_ref[...], kb, vb, valid, sink_ref)


def _win_attn_ctx_kernel(sink_ref, q_ref, kx_ref, vx_ref, o_ref):
    o_ref[...] = _win_heads(q_ref[...], kx_ref[...], vx_ref[...], None, sink_ref)


def _win_attn(sink, p_lat, p_ctx):
    n = p_lat.shape[0]
    m = p_ctx.shape[0]
    nb = n // BLOCK
    ck, cv, cq = COL_DK // LANE, COL_DV // LANE, COL_DQ // 512
    kern = functools.partial(_win_attn_kernel, n_tok=n, n_ctx=m)
    prev = lambda i: jnp.maximum(i - 1, 0)
    nxt = lambda i: jnp.minimum(i + 1, nb - 1)
    return pl.pallas_call(
        kern,
        grid=(nb,),
        in_specs=[pl.BlockSpec(memory_space=pltpu.SMEM),
                  pl.BlockSpec((BLOCK, 512), lambda i: (i, cq)),
                  pl.BlockSpec((BLOCK, LANE), lambda i: (prev(i), ck)),
                  pl.BlockSpec((BLOCK, LANE), lambda i: (i, ck)),
                  pl.BlockSpec((BLOCK, LANE), lambda i: (nxt(i), ck)),
                  pl.BlockSpec((BLOCK, LANE), lambda i: (prev(i), cv)),
                  pl.BlockSpec((BLOCK, LANE), lambda i: (i, cv)),
                  pl.BlockSpec((BLOCK, LANE), lambda i: (nxt(i), cv)),
                  pl.BlockSpec((m, LANE), lambda i: (0, ck)),
                  pl.BlockSpec((m, LANE), lambda i: (0, cv))],
        out_specs=pl.BlockSpec((BLOCK, 512), lambda i: (i, 0)),
        out_shape=jax.ShapeDtypeStruct((n, 512), BF),
        compiler_params=_cparams(("arbitrary",), 32),
        name="win_attn",
    )(sink, p_lat, p_lat, p_lat, p_lat, p_lat, p_lat, p_lat, p_ctx, p_ctx)


def _win_attn_ctx(sink, p_ctx):
    m = p_ctx.shape[0]
    ck, cv, cq = COL_DK // LANE, COL_DV // LANE, COL_DQ // 512
    return pl.pallas_call(
        _win_attn_ctx_kernel,
        grid=(m // BLOCK,),
        in_specs=[pl.BlockSpec(memory_space=pltpu.SMEM),
                  pl.BlockSpec((BLOCK, 512), lambda i: (i, cq)),
                  pl.BlockSpec((m, LANE), lambda i: (0, ck)),
                  pl.BlockSpec((m, LANE), lambda i: (0, cv))],
        out_specs=pl.BlockSpec((BLOCK, 512), lambda i: (i, 0)),
        out_shape=jax.ShapeDtypeStruct((m, 512), BF),
        compiler_params=_cparams(("arbitrary",), 32),
        name="win_attn_ctx",
    )(sink, p_ctx, p_ctx, p_ctx)


def _merge_kernel(gl_ref, au_ref, av_ref, bx_ref, bb_ref, bc_ref, bxp_ref, bcp_ref, bxn_ref, bcn_ref,
                  brc_ref, brd_ref, x_ref, mod_ref, bgate_ref, alng_ref, alnb_ref, ws_ref, bs_ref, cw_ref, cb_ref,
                  wbr_ref, wout_ref, ln1g_ref, ln1b_ref, wr_ref, brt_ref,
                  x1_ref, h2_ref, tidx_ref, tw_ref, *, tm):
    i = pl.program_id(0)
    last = pl.num_programs(0) - 1

    u = jax.nn.gelu(au_ref[...].astype(F32), approximate=True)
    v = jax.nn.gelu(av_ref[...].astype(F32), approximate=True)
    vn = _layer_norm_rows(v, alng_ref[...], alnb_ref[...]).astype(BF)
    blocks = []
    for b in range(tm // BLOCK):
        cols = []
        for g in range(A_GROUPS):
            vbg = vn[b * BLOCK:(b + 1) * BLOCK, g * LANE:(g + 1) * LANE]
            cols.append(jnp.dot(ws_ref[g], vbg, preferred_element_type=F32))
        blocks.append(jnp.concatenate(cols, axis=1) + bs_ref[...])
    mixed = jnp.concatenate(blocks, axis=0) if len(blocks) > 1 else blocks[0]
    br_a = u * mixed

    z = bc_ref[...].astype(F32) * bx_ref[...].astype(F32)
    z_prev = bcp_ref[7:8, :].astype(F32) * bxp_ref[7:8, :].astype(F32) * (i > 0).astype(F32)
    z_next = bcn_ref[0:1, :].astype(F32) * bxn_ref[0:1, :].astype(F32) * (i < last).astype(F32)
    row = lax.broadcasted_iota(jnp.int32, z.shape, 0)
    z_up = jnp.where(row == 0, z_prev, pltpu.roll(z, 1, 0))
    z_dn = jnp.where(row == tm - 1, z_next, pltpu.roll(z, tm - 1, 0))
    y_conv = cw_ref[0:1, :] * z_up + cw_ref[1:2, :] * z + cw_ref[2:3, :] * z_dn + cb_ref[...]
    br_b = bb_ref[...].astype(F32) * y_conv

    branches = (br_a.astype(BF), br_b.astype(BF), brc_ref[...], brd_ref[...])
    merged = None
    for g in range(N_BRANCH):
        pr = jnp.dot(branches[g], wbr_ref[g], preferred_element_type=F32)
        gate = jax.nn.sigmoid(gl_ref[:, g * D_MODEL:(g + 1) * D_MODEL].astype(F32) + bgate_ref[g:g + 1, :])
        merged = gate * pr if merged is None else merged + gate * pr
    y = jnp.dot(merged.astype(BF), wout_ref[...], preferred_element_type=F32)

    r = DN_ALPHA * x_ref[...] + mod_ref[0:1, :] * y
    x1 = _layer_norm_rows(r, ln1g_ref[...], ln1b_ref[...])
    x1_ref[...] = x1
    h2 = (x1 * (1.0 + mod_ref[1:2, :]) + mod_ref[2:3, :]).astype(BF)
    h2_ref[...] = h2
    logits = jnp.dot(h2, wr_ref[...], preferred_element_type=F32) + brt_ref[...]

    lane = lax.broadcasted_iota(jnp.int32, logits.shape, 1)
    vals, idxs = [], []
    cur = logits
    for _ in range(TOP_K):
        mx = jnp.max(cur, axis=-1, keepdims=True)
        ix = jnp.min(jnp.where(cur == mx, lane, LANE), axis=-1, keepdims=True)
        vals.append(mx)
        idxs.append(ix)
        cur = jnp.where(lane == ix, -jnp.inf, cur)
    es = [jnp.exp(vk - vals[0]) for vk in vals]
    den = es[0] + es[1] + es[2] + es[3]
    tidx = jnp.zeros(logits.shape, jnp.int32)
    tw = jnp.zeros(logits.shape, F32)
    for k in range(TOP_K):
        tidx = jnp.where(lane == k, idxs[k], tidx)
        tw = jnp.where(lane == k, es[k] / den, tw)
    tidx_ref[...] = tidx
    tw_ref[...] = tw


def _merge(p, br_c, br_d, x2d, mod3, lw, *, tm):
    n = p.shape[0]
    nt = n // tm
    r8 = tm // 8
    c512 = lambda col: col // 512
    kern = functools.partial(_merge_kernel, tm=tm)
    prev8 = lambda i: (jnp.maximum(i * r8 - 1, 0))
    next8 = lambda i: (jnp.minimum((i + 1) * r8, n // 8 - 1))
    seg = lambda col: pl.BlockSpec((tm, 512), lambda i: (i, c512(col)))
    in_specs = [
        pl.BlockSpec((tm, GL_W), lambda i: (i, 0)),
        seg(COL_AU), seg(COL_AV), seg(COL_BX), seg(COL_BB), seg(COL_BC),
        pl.BlockSpec((8, 512), lambda i: (prev8(i), c512(COL_BX))),
        pl.BlockSpec((8, 512), lambda i: (prev8(i), c512(COL_BC))),
        pl.BlockSpec((8, 512), lambda i: (next8(i), c512(COL_BX))),
        pl.BlockSpec((8, 512), lambda i: (next8(i), c512(COL_BC))),
        pl.BlockSpec((tm, 512), lambda i: (i, 0)),
        pl.BlockSpec((tm, 512), lambda i: (i, 0)),
        pl.BlockSpec((tm, D_MODEL), lambda i: (i, 0)),
        _const_spec((8, D_MODEL)),
        _const_spec((N_BRANCH, D_MODEL)),
        _const_spec((1, 512)), _const_spec((1, 512)),
        _const_spec((A_GROUPS, BLOCK, BLOCK)),
        _const_spec((BLOCK, 512)),
        _const_spec((3, 512)), _const_spec((1, 512)),
        _const_spec((N_BRANCH, BRANCH_W, D_MODEL)),
        _const_spec((D_MODEL, D_MODEL)),
        _const_spec((1, D_MODEL)), _const_spec((1, D_MODEL)),
        _const_spec((D_MODEL, LANE)), _const_spec((1, LANE)),
    ]
    out_specs = [pl.BlockSpec((tm, D_MODEL), lambda i: (i, 0)),
                 pl.BlockSpec((tm, D_MODEL), lambda i: (i, 0)),
                 pl.BlockSpec((tm, LANE), lambda i: (i, 0)),
                 pl.BlockSpec((tm, LANE), lambda i: (i, 0))]
    out_shape = [jax.ShapeDtypeStruct((n, D_MODEL), F32), jax.ShapeDtypeStruct((n, D_MODEL), BF),
                 jax.ShapeDtypeStruct((n, LANE), jnp.int32), jax.ShapeDtypeStruct((n, LANE), F32)]
    return pl.pallas_call(
        kern, grid=(nt,), in_specs=in_specs, out_specs=out_specs, out_shape=out_shape,
        compiler_params=_cparams(("arbitrary",), 56),
        name="merge",
    )(p, p, p, p, p, p, p, p, p, p, br_c, br_d, x2d, mod3, lw['b_gate'], lw['a_ln_g'], lw['a_ln_b'],
      lw['a_ws'], lw['a_bs_full'], lw['b_conv_w'], lw['b_conv_b'], lw['w_br'], lw['w_out'],
      lw['ln1_g'], lw['ln1_b'], lw['w_router'], lw['b_router'])


def _moe_kernel(be_ref, bv_ref, x_ref, rw_ref, wgu_ref, bgu_ref, wd_ref, bd_ref, o_ref):
    i = pl.program_id(0)

    @pl.when(bv_ref[i] > 0)
    def _():
        gu = jnp.dot(x_ref[...], wgu_ref[0], preferred_element_type=F32) + bgu_ref[0]
        g = jnp.minimum(gu[:, :D_FF], SWIGLU_LIMIT)
        u = jnp.clip(gu[:, D_FF:], -SWIGLU_LIMIT, SWIGLU_LIMIT)
        act = (u + 1.0) * (g * jax.nn.sigmoid(SWIGLU_ALPHA * g))
        out = jnp.dot(act.astype(BF), wd_ref[0], preferred_element_type=F32) + bd_ref[0]
        o_ref[...] = (out * rw_ref[...]).astype(BF)

    @pl.when(bv_ref[i] == 0)
    def _():
        o_ref[...] = jnp.zeros(o_ref.shape, BF)


def _moe_rows(blk_expert, blk_valid, xg, row_w, w_gu, b_gu, w_down, b_down):
    n_rows = xg.shape[0]
    n_blk = n_rows // MOE_BLK
    grid_spec = pltpu.PrefetchScalarGridSpec(
        num_scalar_prefetch=2,
        grid=(n_blk,),
        in_specs=[pl.BlockSpec((MOE_BLK, D_MODEL), lambda i, be, bv: (i, 0)),
                  pl.BlockSpec((MOE_BLK, 1), lambda i, be, bv: (i, 0)),
                  pl.BlockSpec((1, D_MODEL, 2 * D_FF), lambda i, be, bv: (be[i], 0, 0)),
                  pl.BlockSpec((1, 1, 2 * D_FF), lambda i, be, bv: (be[i], 0, 0)),
                  pl.BlockSpec((1, D_FF, D_MODEL), lambda i, be, bv: (be[i], 0, 0)),
                  pl.BlockSpec((1, 1, D_MODEL), lambda i, be, bv: (be[i], 0, 0))],
        out_specs=pl.BlockSpec((MOE_BLK, D_MODEL), lambda i, be, bv: (i, 0)),
    )
    return pl.pallas_call(
        _moe_kernel, grid_spec=grid_spec,
        out_shape=jax.ShapeDtypeStruct((n_rows, D_MODEL), BF),
        compiler_params=_cparams(("arbitrary",), 48),
        name="moe_experts",
    )(blk_expert, blk_valid, xg, row_w, w_gu, b_gu, w_down, b_down)


def _route(top_idx, top_w):
    n_tok = top_idx.shape[0]
    n_pairs = n_tok * TOP_K
    e_flat = top_idx.reshape(-1)
    order = jnp.argsort(e_flat)
    e_sorted = e_flat[order]
    counts = jnp.bincount(e_flat, length=N_EXPERTS)
    padded = (counts + MOE_BLK - 1) // MOE_BLK * MOE_BLK
    starts = jnp.cumsum(counts) - counts
    pends = jnp.cumsum(padded)
    pstarts = pends - padded
    dest = (pstarts[e_sorted] + (jnp.arange(n_pairs) - starts[e_sorted])).astype(jnp.int32)
    n_rows = (n_pairs + MOE_BLK - 1) // MOE_BLK * MOE_BLK + N_EXPERTS * MOE_BLK
    n_blk = n_rows // MOE_BLK
    row_tok = jnp.zeros((n_rows,), jnp.int32).at[dest].set((order // TOP_K).astype(jnp.int32))
    row_w = jnp.zeros((n_rows,), F32).at[dest].set(top_w.reshape(-1)[order])
    pair_row = jnp.zeros((n_pairs,), jnp.int32).at[order].set(dest)
    blk_start = jnp.arange(n_blk) * MOE_BLK
    blk_expert = jnp.minimum(jnp.searchsorted(pends, blk_start, side='right'), N_EXPERTS - 1).astype(jnp.int32)
    blk_valid = (blk_start < pends[-1]).astype(jnp.int32)
    return row_tok, row_w, pair_row, blk_expert, blk_valid


def _combine_kernel(x1_ref, f_ref, g2_ref, lng_ref, lnb_ref, o_ref):
    f = f_ref[0].astype(F32) + f_ref[1].astype(F32) + f_ref[2].astype(F32) + f_ref[3].astype(F32)
    r = DN_ALPHA * x1_ref[...] + g2_ref[...] * f
    o_ref[...] = _layer_norm_rows(r, lng_ref[...], lnb_ref[...])


def _combine(x1, f4, g2, ln_g, ln_b, *, tm):
    n = x1.shape[0]
    return pl.pallas_call(
        _combine_kernel, grid=(n // tm,),
        in_specs=[pl.BlockSpec((tm, D_MODEL), lambda i: (i, 0)),
                  pl.BlockSpec((TOP_K, tm, D_MODEL), lambda i: (0, i, 0)),
                  pl.BlockSpec((1, D_MODEL), lambda i: (0, 0)),
                  pl.BlockSpec((1, D_MODEL), lambda i: (0, 0)),
                  pl.BlockSpec((1, D_MODEL), lambda i: (0, 0))],
        out_specs=pl.BlockSpec((tm, D_MODEL), lambda i: (i, 0)),
        out_shape=jax.ShapeDtypeStruct((n, D_MODEL), F32),
        compiler_params=_cparams(("arbitrary",), 40),
        name="combine_ln2",
    )(x1, f4, g2, ln_g, ln_b)


def _rope_tables(n_tok):
    n_freq = ROPE_DIM // 4
    inv = ROPE_BASE ** (-jnp.arange(n_freq, dtype=F32) / n_freq)
    t = jnp.arange(n_tok)
    ang_r = (t // GRID_W).astype(F32)[:, None] * inv
    ang_c = (t % GRID_W).astype(F32)[:, None] * inv
    cos64 = jnp.concatenate([jnp.cos(ang_r), jnp.cos(ang_r), jnp.cos(ang_c), jnp.cos(ang_c)], axis=1)
    sin64 = jnp.concatenate([-jnp.sin(ang_r), jnp.sin(ang_r), -jnp.sin(ang_c), jnp.sin(ang_c)], axis=1)
    return jnp.concatenate([cos64, cos64], axis=1), jnp.concatenate([sin64, sin64], axis=1)


def _reorder_w_in(w):
    n9 = 9 * 512
    return jnp.concatenate([w[:, n9 + 256:], w[:, :n9], w[:, n9:n9 + 256],
                            jnp.zeros((w.shape[0], 256), w.dtype)], axis=1).astype(BF)


def _row_tile(n, pref):
    return pref if n % pref == 0 else n


def kernel(x, c, ctx, c_ctx, w_ada, b_ada, w_in, b_gate, a_ln_g, a_ln_b, a_ws, a_bs, b_conv_w, b_conv_b, c_lq1, c_lk1, c_lq2, c_lk2, c_subln_g, d_sink, w_br, w_out, ln1_g, ln1_b, w_router, b_router, e_w_gu, e_b_gu, e_w_down, e_b_down, ln2_g, ln2_b):
    assert x.shape[0] == 1 and ctx.shape[0] == 1
    n, m = x.shape[1], ctx.shape[1]
    assert n % 256 == 0 and m % BLOCK == 0
    xl = x[0]
    xc = ctx[0]

    c_rows = jnp.zeros((8, D_MODEL), F32).at[0].set(c[0]).at[1].set(c_ctx)
    mods = _ada(c_rows, w_ada, b_ada)
    cos_t, sin_t = _rope_tables(n)
    cos_c = jnp.ones((m, LANE), F32)
    sin_c = jnp.zeros((m, LANE), F32)
    row2 = lambda v: v.reshape(1, -1)

    for l in range(DEPTH):
        need_ctx = l < DEPTH - 1
        lam_init = 0.8 - 0.6 * math.exp(-0.3 * l)
        sh1, sc1, g1, sh2, sc2, g2 = [row2(t) for t in jnp.split(mods[l, 0], 6)]
        csh1, csc1, cg1, csh2, csc2, cg2 = [row2(t) for t in jnp.split(mods[l, 1], 6)]
        lam = (jnp.exp(jnp.sum(c_lq1[l] * c_lk1[l])) - jnp.exp(jnp.sum(c_lq2[l] * c_lk2[l]))
               + lam_init).astype(F32).reshape(1)
        w_r = _reorder_w_in(w_in[l])
        lw = {
            'b_gate': b_gate[l], 'a_ln_g': row2(a_ln_g[l]), 'a_ln_b': row2(a_ln_b[l]),
            'a_ws': a_ws[l].astype(BF),
            'a_bs_full': jnp.repeat(a_bs[l].T, BLOCK, axis=1),
            'b_conv_w': b_conv_w[l], 'b_conv_b': row2(b_conv_b[l]),
            'w_br': w_br[l].astype(BF), 'w_out': w_out[l].astype(BF),
            'ln1_g': row2(ln1_g[l]), 'ln1_b': row2(ln1_b[l]),
            'w_router': jnp.pad(w_router[l], ((0, 0), (0, LANE - N_EXPERTS))).astype(BF),
            'b_router': jnp.pad(row2(b_router[l]), ((0, 0), (0, LANE - N_EXPERTS)), constant_values=NEG_BIG),
        }

        p_lat = _proj(xl, sc1, sh1, w_r, cos_t, sin_t, rope=True, tm=_row_tile(n, 1024))
        p_ctx = _proj(xc, csc1, csh1, w_r, cos_c, sin_c, rope=False, tm=m)

        sl = lambda arr, col, w: lax.slice_in_dim(arr, col, col + w, axis=1)
        k_all = jnp.concatenate([sl(p_lat, COL_CK, 512), sl(p_ctx, COL_CK, 512)], axis=0)
        v_all = jnp.concatenate([sl(p_lat, COL_CV, 512), sl(p_ctx, COL_CV, 512)], axis=0)
        br_c = _diff_attn(lam, sl(p_lat, COL_CQ, 512), k_all, v_all, c_subln_g[l], lam_init)
        br_d = _win_attn(d_sink[l], p_lat, p_ctx)

        mod3 = jnp.zeros((8, D_MODEL), F32).at[0].set(g1[0]).at[1].set(sc2[0]).at[2].set(sh2[0])
        x1, h2, tidx, tw = _merge(p_lat, br_c, br_d, xl, mod3, lw, tm=256)
        tidx, tw = tidx[:, :TOP_K], tw[:, :TOP_K]

        if need_ctx:
            cbr_c = _diff_attn(lam, sl(p_ctx, COL_CQ, 512), sl(p_ctx, COL_CK, 512), sl(p_ctx, COL_CV, 512),
                               c_subln_g[l], lam_init)
            cbr_d = _win_attn_ctx(d_sink[l], p_ctx)
            cmod3 = jnp.zeros((8, D_MODEL), F32).at[0].set(cg1[0]).at[1].set(csc2[0]).at[2].set(csh2[0])
            xc1, hc2, ctidx, ctw = _merge(p_ctx, cbr_c, cbr_d, xc, cmod3, lw, tm=_row_tile(m, 256))
            h2 = jnp.concatenate([h2, hc2], axis=0)
            tidx = jnp.concatenate([tidx, ctidx[:, :TOP_K]], axis=0)
            tw = jnp.concatenate([tw, ctw[:, :TOP_K]], axis=0)

        n_all = h2.shape[0]
        row_tok, row_w, pair_row, blk_expert, blk_valid = _route(tidx, tw)
        xg = jnp.take(h2, row_tok, axis=0)
        rows_out = _moe_rows(blk_expert, blk_valid, xg, row_w.reshape(-1, 1),
                             e_w_gu[l].astype(BF), e_b_gu[l].reshape(N_EXPERTS, 1, -1),
                             e_w_down[l].astype(BF), e_b_down[l].reshape(N_EXPERTS, 1, -1))
        f4 = jnp.take(rows_out, pair_row.reshape(n_all, TOP_K).T, axis=0)

        xl = _combine(x1, f4[:, :n], g2, row2(ln2_g[l]), row2(ln2_b[l]), tm=256)
        if need_ctx:
            xc = _combine(xc1, f4[:, n:], cg2, row2(ln2_g[l]), row2(ln2_b[l]), tm=_row_tile(m, 256))

    return xl[None]
```

```python
import functools
import math

import jax
import jax.numpy as jnp
from jax import lax
from jax.experimental import pallas as pl
from jax.experimental.pallas import tpu as pltpu

BF = jnp.bfloat16
F32 = jnp.float32

D_MODEL = 2048
DEPTH = 2
GRID_W = 64
BLOCK = 128
A_GROUPS = 4
C_HEADS = 4
C_HD = 64
C_VD = 128
D_HEADS = 8
D_KV_HEADS = 2
D_HD = 64
WINDOW = 128
N_BRANCH = 4
BRANCH_W = 512
PROJ_SIZES = (512, 512, 512, 512, 512, 512, 512, 512, 512, 128, 128, N_BRANCH * D_MODEL)
N_EXPERTS = 32
TOP_K = 4
D_FF = 1024
SWIGLU_LIMIT = 7.0
SWIGLU_ALPHA = 1.702
ROPE_DIM = 64
ROPE_BASE = 10000.0
LN_EPS = 1e-5
DN_ALPHA = (2 * DEPTH) ** 0.25

LANE = 128
NEG_BIG = -1e30

GL_W = N_BRANCH * D_MODEL
PROJ_TN = 256
SEG_W = 9 * 512 + 256
P_W = GL_W + SEG_W
N_SEG_TILES = SEG_W // PROJ_TN
COL_AU, COL_AV, COL_BX, COL_BB, COL_BC, COL_CQ, COL_CK, COL_CV, COL_DQ = [GL_W + 512 * s for s in range(9)]
COL_DK = GL_W + 9 * 512
COL_DV = COL_DK + 128
ROPE_FULL_TILES = (10, 11, 12, 13, 16, 17)
ROPE_PART_TILE = 18

MOE_BLK = 256


def _cparams(dims, vmem_mib):
    return pltpu.CompilerParams(dimension_semantics=dims, vmem_limit_bytes=vmem_mib * 1024 * 1024)


def _const_spec(shape):
    nd = len(shape)
    return pl.BlockSpec(shape, lambda *_: (0,) * nd, pipeline_mode=pl.Buffered(1))


def _layer_norm_rows(r, g, b):
    mu = jnp.mean(r, axis=-1, keepdims=True)
    d = r - mu
    var = jnp.mean(d * d, axis=-1, keepdims=True)
    return d * lax.rsqrt(var + LN_EPS) * g + b


def _ada_kernel(c_ref, w_ref, b_ref, o_ref):
    cs = c_ref[...]
    s = cs * jax.nn.sigmoid(cs)
    o_ref[0] = jnp.dot(s.astype(BF), w_ref[0].astype(BF), preferred_element_type=F32) + b_ref[0]


def _ada(c_rows, w_ada, b_ada):
    n_l, _, n_out = w_ada.shape
    tn = 1536
    return pl.pallas_call(
        _ada_kernel,
        grid=(n_l, n_out // tn),
        in_specs=[pl.BlockSpec((8, D_MODEL), lambda l, j: (0, 0)),
                  pl.BlockSpec((1, D_MODEL, tn), lambda l, j: (l, 0, j)),
                  pl.BlockSpec((1, 1, tn), lambda l, j: (l, 0, j))],
        out_specs=pl.BlockSpec((1, 8, tn), lambda l, j: (l, 0, j)),
        out_shape=jax.ShapeDtypeStruct((n_l, 8, n_out), F32),
        compiler_params=_cparams(("arbitrary", "arbitrary"), 40),
        name="ada",
    )(c_rows, w_ada, b_ada.reshape(n_l, 1, n_out))


def _rope_rotate(a, cos_ref, sin_ref):
    w = a.shape[1]
    lane = lax.broadcasted_iota(jnp.int32, a.shape, 1)
    first = jnp.bitwise_and(lane, 16) == 0
    swapped = jnp.where(first, pltpu.roll(a, w - 16, 1), pltpu.roll(a, 16, 1))
    reps = w // LANE
    cos = cos_ref[...]
    sin = sin_ref[...]
    if reps > 1:
        cos = jnp.concatenate([cos] * reps, axis=1)
        sin = jnp.concatenate([sin] * reps, axis=1)
    return a * cos + swapped * sin


def _proj_kernel(x_ref, sc_ref, sh_ref, w_ref, cos_ref, sin_ref, o_ref, h_scr, *, rope):
    j = pl.program_id(1)

    @pl.when(j == 0)
    def _():
        h_scr[...] = (x_ref[...] * (1.0 + sc_ref[...]) + sh_ref[...]).astype(BF)

    acc = jnp.dot(h_scr[...], w_ref[...].astype(BF), preferred_element_type=F32)
    if not rope:
        o_ref[...] = acc.astype(BF)
        return

    full = functools.reduce(jnp.logical_or, [j == t for t in ROPE_FULL_TILES])
    part = j == ROPE_PART_TILE

    @pl.when(full)
    def _():
        o_ref[...] = _rope_rotate(acc, cos_ref, sin_ref).astype(BF)

    @pl.when(part)
    def _():
        o_ref[:, :LANE] = _rope_rotate(acc[:, :LANE], cos_ref, sin_ref).astype(BF)
        o_ref[:, LANE:] = acc[:, LANE:].astype(BF)

    @pl.when(jnp.logical_not(jnp.logical_or(full, part)))
    def _():
        o_ref[...] = acc.astype(BF)


def _proj(x2d, sc, sh, w_in_l, cos_t, sin_t, *, rope, tm):
    n = x2d.shape[0]
    kern = functools.partial(_proj_kernel, rope=rope)
    n_gl_tiles = GL_W // PROJ_TN
    out_tile = lambda j: jnp.where(j < N_SEG_TILES, j + n_gl_tiles, j - N_SEG_TILES)
    return pl.pallas_call(
        kern,
        grid=(n // tm, P_W // PROJ_TN),
        in_specs=[pl.BlockSpec((tm, D_MODEL), lambda i, j: (i, 0), pipeline_mode=pl.Buffered(1)),
                  pl.BlockSpec((1, D_MODEL), lambda i, j: (0, 0)),
                  pl.BlockSpec((1, D_MODEL), lambda i, j: (0, 0)),
                  pl.BlockSpec((D_MODEL, PROJ_TN), lambda i, j: (0, j)),
                  pl.BlockSpec((tm, LANE), lambda i, j: (i, 0)),
                  pl.BlockSpec((tm, LANE), lambda i, j: (i, 0))],
        out_specs=pl.BlockSpec((tm, PROJ_TN), lambda i, j: (i, out_tile(j))),
        out_shape=jax.ShapeDtypeStruct((n, P_W), BF),
        scratch_shapes=[pltpu.VMEM((tm, D_MODEL), BF)],
        compiler_params=_cparams(("arbitrary", "arbitrary"), 56),
        name="proj_rope" if rope else "proj_ctx",
    )(x2d, sc, sh, w_in_l, cos_t, sin_t)


_ACC_ROWS = 32
LOG2E = math.log2(math.e)


def _diff_attn_kernel(lam_ref, qt_ref, k_ref, vt_ref, g_ref, o_ref, qbd_scr, m_scr, l_scr, acc_scr, s_scr, mx_scr, *,
                      tq, tk, n_chunks, post_scale):
    w = 2 * tq
    qt = qt_ref[...].astype(F32) * (C_HD ** -0.5 * LOG2E)
    row = lax.broadcasted_iota(jnp.int32, qt.shape, 0)
    qbd_scr[...] = jnp.concatenate([jnp.where(row < C_HD, qt, 0.0), jnp.where(row >= C_HD, qt, 0.0)],
                                   axis=1).astype(BF)
    m_scr[...] = jnp.full(m_scr.shape, -jnp.inf, F32)
    l_scr[...] = jnp.zeros(l_scr.shape, F32)
    acc_scr[...] = jnp.zeros(acc_scr.shape, F32)

    def scores(c, slot):
        k = k_ref[pl.ds(pl.multiple_of(c * tk, tk), tk), :]
        s = jnp.dot(k, qbd_scr[...], preferred_element_type=F32)
        s_scr[slot] = s
        mx_scr[slot] = jnp.max(s.reshape(tk // _ACC_ROWS, _ACC_ROWS, w), axis=0)

    def softmax_pv(c, slot):
        m_old = m_scr[...]
        m_new = jnp.maximum(m_old, jnp.max(mx_scr[slot], axis=0, keepdims=True))
        alpha = jnp.exp2(m_old - m_new)
        p = jnp.exp2(s_scr[slot] - m_new)
        l_scr[...] = alpha * l_scr[...] + jnp.sum(p.reshape(tk // _ACC_ROWS, _ACC_ROWS, w), axis=0)
        acc_scr[...] = acc_scr[...] * alpha + jnp.dot(vt_ref[0, c], p.astype(BF), preferred_element_type=F32)
        m_scr[...] = m_new

    scores(0, 0)
    n_pairs = (n_chunks - 1) // 2

    def pair(i, carry):
        c = 2 * i
        softmax_pv(c, 0)
        scores(c + 1, 1)
        softmax_pv(c + 1, 1)
        scores(c + 2, 0)
        return carry

    if n_pairs > 0:
        lax.fori_loop(0, n_pairs, pair, 0)
    for c in range(2 * n_pairs, n_chunks):
        softmax_pv(c, c % 2)
        if c + 1 < n_chunks:
            scores(c + 1, (c + 1) % 2)

    o = acc_scr[...] / jnp.sum(l_scr[...], axis=0, keepdims=True)
    od = o[:, :tq] - lam_ref[0] * o[:, tq:]
    ms = jnp.mean(od * od, axis=0, keepdims=True)
    on = od * lax.rsqrt(ms + LN_EPS) * g_ref[...] * post_scale
    o_ref[...] = on.T.astype(BF)


def _pick_tk(n_k):
    for tk in (1280, 1024, 768, 512, 256, 128):
        if n_k % tk == 0:
            return tk
    raise ValueError(f"unsupported key count {n_k}")


def _diff_attn(lam, q, k_all, v_all, subln_g, lam_init):
    n = q.shape[0]
    n_k = k_all.shape[0]
    tq = 256
    tk = _pick_tk(n_k)
    n_chunks = n_k // tk
    qt = q.T
    vt = v_all.reshape(n_chunks, tk, C_HEADS, C_VD).transpose(2, 0, 3, 1)
    kern = functools.partial(_diff_attn_kernel, tq=tq, tk=tk, n_chunks=n_chunks, post_scale=1.0 - lam_init)
    return pl.pallas_call(
        kern,
        grid=(C_HEADS, n // tq),
        in_specs=[pl.BlockSpec(memory_space=pltpu.SMEM),
                  pl.BlockSpec((C_VD, tq), lambda h, i: (h, i)),
                  pl.BlockSpec((n_k, C_VD), lambda h, i: (0, h)),
                  pl.BlockSpec((1, n_chunks, C_VD, tk), lambda h, i: (h, 0, 0, 0)),
                  pl.BlockSpec((C_VD, 1), lambda h, i: (0, 0))],
        out_specs=pl.BlockSpec((tq, C_VD), lambda h, i: (i, h)),
        out_shape=jax.ShapeDtypeStruct((n, C_HEADS * C_VD), BF),
        scratch_shapes=[pltpu.VMEM((C_VD, 2 * tq), BF), pltpu.VMEM((1, 2 * tq), F32),
                        pltpu.VMEM((_ACC_ROWS, 2 * tq), F32), pltpu.VMEM((C_VD, 2 * tq), F32),
                        pltpu.VMEM((2, tk, 2 * tq), F32), pltpu.VMEM((2, _ACC_ROWS, 2 * tq), F32)],
        compiler_params=_cparams(("arbitrary", "arbitrary"), 48),
        name="diff_attn",
    )(lam, qt, k_all, vt, subln_g.reshape(C_VD, 1))


def _win_heads(q, kb, vb, valid, sink_ref):
    outs = []
    grp = D_HEADS // D_KV_HEADS
    for h in range(D_HEADS):
        kh = h // grp
        qh = q[:, h * D_HD:(h + 1) * D_HD]
        k_h = kb[:, kh * D_HD:(kh + 1) * D_HD]
        v_h = vb[:, kh * D_HD:(kh + 1) * D_HD]
        s = lax.dot_general(qh, k_h, (((1,), (1,)), ((), ())), preferred_element_type=F32) * (D_HD ** -0.5)
        if valid is not None:
            s = jnp.where(valid, s, NEG_BIG)
        sk = sink_ref[h]
        m = jnp.maximum(jnp.max(s, axis=-1, keepdims=True), sk)
        e = jnp.exp(s - m)
        l = jnp.sum(e, axis=-1, keepdims=True) + jnp.exp(sk - m)
        p = (e / l).astype(BF)
        outs.append(jnp.dot(p, v_h, preferred_element_type=F32))
    return jnp.concatenate(outs, axis=1).astype(BF)


def _win_attn_kernel(sink_ref, q_ref, kp_ref, kc_ref, kn_ref, vp_ref, vc_ref, vn_ref, kx_ref, vx_ref, o_ref, *,
                     n_tok, n_ctx):
    i = pl.program_id(0)
    kb = jnp.concatenate([kp_ref[...], kc_ref[...], kn_ref[...], kx_ref[...]], axis=0)
    vb = jnp.concatenate([vp_ref[...], vc_ref[...], vn_ref[...], vx_ref[...]], axis=0)
    n_keys = 3 * BLOCK + n_ctx
    r = lax.broadcasted_iota(jnp.int32, (BLOCK, n_keys), 0)
    j = lax.broadcasted_iota(jnp.int32, (BLOCK, n_keys), 1)
    kpos = (i - 1) * BLOCK + j
    dist = j - BLOCK - r
    in_band = (jnp.abs(dist) <= WINDOW) & (kpos >= 0) & (kpos < n_tok)
    valid = (j >= 3 * BLOCK) | in_band
    o_ref[...] = _win_heads(q_ref[...], kb, vb, valid, sink_ref)


def _win_attn_ctx_kernel(sink_ref, q_ref, kx_ref, vx_ref, o_ref):
    o_ref[...] = _win_heads(q_ref[...], kx_ref[...], vx_ref[...], None, sink_ref)


def _win_attn(sink, p_lat, p_ctx):
    n = p_lat.shape[0]
    m = p_ctx.shape[0]
    nb = n // BLOCK
    ck, cv, cq = COL_DK // LANE, COL_DV // LANE, COL_DQ // 512
    kern = functools.partial(_win_attn_kernel, n_tok=n, n_ctx=m)
    prev = lambda i: jnp.maximum(i - 1, 0)
    nxt = lambda i: jnp.minimum(i + 1, nb - 1)
    return pl.pallas_call(
        kern,
        grid=(nb,),
        in_specs=[pl.BlockSpec(memory_space=pltpu.SMEM),
                  pl.BlockSpec((BLOCK, 512), lambda i: (i, cq)),
                  pl.BlockSpec((BLOCK, LANE), lambda i: (prev(i), ck)),
                  pl.BlockSpec((BLOCK, LANE), lambda i: (i, ck)),
                  pl.BlockSpec((BLOCK, LANE), lambda i: (nxt(i), ck)),
                  pl.BlockSpec((BLOCK, LANE), lambda i: (prev(i), cv)),
                  pl.BlockSpec((BLOCK, LANE), lambda i: (i, cv)),
                  pl.BlockSpec((BLOCK, LANE), lambda i: (nxt(i), cv)),
                  pl.BlockSpec((m, LANE), lambda i: (0, ck)),
                  pl.BlockSpec((m, LANE), lambda i: (0, cv))],
        out_specs=pl.BlockSpec((BLOCK, 512), lambda i: (i, 0)),
        out_shape=jax.ShapeDtypeStruct((n, 512), BF),
        compiler_params=_cparams(("arbitrary",), 32),
        name="win_attn",
    )(sink, p_lat, p_lat, p_lat, p_lat, p_lat, p_lat, p_lat, p_ctx, p_ctx)


def _win_attn_ctx(sink, p_ctx):
    m = p_ctx.shape[0]
    ck, cv, cq = COL_DK // LANE, COL_DV // LANE, COL_DQ // 512
    return pl.pallas_call(
        _win_attn_ctx_kernel,
        grid=(m // BLOCK,),
        in_specs=[pl.BlockSpec(memory_space=pltpu.SMEM),
                  pl.BlockSpec((BLOCK, 512), lambda i: (i, cq)),
                  pl.BlockSpec((m, LANE), lambda i: (0, ck)),
                  pl.BlockSpec((m, LANE), lambda i: (0, cv))],
        out_specs=pl.BlockSpec((BLOCK, 512), lambda i: (i, 0)),
        out_shape=jax.ShapeDtypeStruct((m, 512), BF),
        compiler_params=_cparams(("arbitrary",), 32),
        name="win_attn_ctx",
    )(sink, p_ctx, p_ctx, p_ctx)


def _merge_kernel(gl_ref, au_ref, av_ref, bx_ref, bb_ref, bc_ref, bxp_ref, bcp_ref, bxn_ref, bcn_ref,
                  brc_ref, brd_ref, x_ref, mod_ref, bgate_ref, alng_ref, alnb_ref, ws_ref, bs_ref, cw_ref, cb_ref,
                  wbr_ref, wout_ref, ln1g_ref, ln1b_ref, wr_ref, brt_ref,
                  x1_ref, h2_ref, tidx_ref, tw_ref, *, tm):
    i = pl.program_id(0)
    last = pl.num_programs(0) - 1

    u = jax.nn.gelu(au_ref[...].astype(F32), approximate=True)
    v = jax.nn.gelu(av_ref[...].astype(F32), approximate=True)
    vn = _layer_norm_rows(v, alng_ref[...], alnb_ref[...]).astype(BF)
    blocks = []
    for b in range(tm // BLOCK):
        cols = []
        for g in range(A_GROUPS):
            vbg = vn[b * BLOCK:(b + 1) * BLOCK, g * LANE:(g + 1) * LANE]
            cols.append(jnp.dot(ws_ref[g], vbg, preferred_element_type=F32))
        blocks.append(jnp.concatenate(cols, axis=1) + bs_ref[...])
    mixed = jnp.concatenate(blocks, axis=0) if len(blocks) > 1 else blocks[0]
    br_a = u * mixed

    z = bc_ref[...].astype(F32) * bx_ref[...].astype(F32)
    z_prev = bcp_ref[7:8, :].astype(F32) * bxp_ref[7:8, :].astype(F32) * (i > 0).astype(F32)
    z_next = bcn_ref[0:1, :].astype(F32) * bxn_ref[0:1, :].astype(F32) * (i < last).astype(F32)
    row = lax.broadcasted_iota(jnp.int32, z.shape, 0)
    z_up = jnp.where(row == 0, z_prev, pltpu.roll(z, 1, 0))
    z_dn = jnp.where(row == tm - 1, z_next, pltpu.roll(z, tm - 1, 0))
    y_conv = cw_ref[0:1, :] * z_up + cw_ref[1:2, :] * z + cw_ref[2:3, :] * z_dn + cb_ref[...]
    br_b = bb_ref[...].astype(F32) * y_conv

    branches = (br_a.astype(BF), br_b.astype(BF), brc_ref[...], brd_ref[...])
    merged = None
    for g in range(N_BRANCH):
        pr = jnp.dot(branches[g], wbr_ref[g], preferred_element_type=F32)
        gate = jax.nn.sigmoid(gl_ref[:, g * D_MODEL:(g + 1) * D_MODEL].astype(F32) + bgate_ref[g:g + 1, :])
        merged = gate * pr if merged is None else merged + gate * pr
    y = jnp.dot(merged.astype(BF), wout_ref[...], preferred_element_type=F32)

    r = DN_ALPHA * x_ref[...] + mod_ref[0:1, :] * y
    x1 = _layer_norm_rows(r, ln1g_ref[...], ln1b_ref[...])
    x1_ref[...] = x1
    h2 = (x1 * (1.0 + mod_ref[1:2, :]) + mod_ref[2:3, :]).astype(BF)
    h2_ref[...] = h2
    logits = jnp.dot(h2, wr_ref[...], preferred_element_type=F32) + brt_ref[...]

    lane = lax.broadcasted_iota(jnp.int32, logits.shape, 1)
    vals, idxs = [], []
    cur = logits
    for _ in range(TOP_K):
        mx = jnp.max(cur, axis=-1, keepdims=True)
        ix = jnp.min(jnp.where(cur == mx, lane, LANE), axis=-1, keepdims=True)
        vals.append(mx)
        idxs.append(ix)
        cur = jnp.where(lane == ix, -jnp.inf, cur)
    es = [jnp.exp(vk - vals[0]) for vk in vals]
    den = es[0] + es[1] + es[2] + es[3]
    tidx = jnp.zeros(logits.shape, jnp.int32)
    tw = jnp.zeros(logits.shape, F32)
    for k in range(TOP_K):
        tidx = jnp.where(lane == k, idxs[k], tidx)
        tw = jnp.where(lane == k, es[k] / den, tw)
    tidx_ref[...] = tidx
    tw_ref[...] = tw


def _merge(p, br_c, br_d, x2d, mod3, lw, *, tm):
    n = p.shape[0]
    nt = n // tm
    r8 = tm // 8
    c512 = lambda col: col // 512
    kern = functools.partial(_merge_kernel, tm=tm)
    prev8 = lambda i: (jnp.maximum(i * r8 - 1, 0))
    next8 = lambda i: (jnp.minimum((i + 1) * r8, n // 8 - 1))
    seg = lambda col: pl.BlockSpec((tm, 512), lambda i: (i, c512(col)))
    in_specs = [
        pl.BlockSpec((tm, GL_W), lambda i: (i, 0)),
        seg(COL_AU), seg(COL_AV), seg(COL_BX), seg(COL_BB), seg(COL_BC),
        pl.BlockSpec((8, 512), lambda i: (prev8(i), c512(COL_BX))),
        pl.BlockSpec((8, 512), lambda i: (prev8(i), c512(COL_BC))),
        pl.BlockSpec((8, 512), lambda i: (next8(i), c512(COL_BX))),
        pl.BlockSpec((8, 512), lambda i: (next8(i), c512(COL_BC))),
        pl.BlockSpec((tm, 512), lambda i: (i, 0)),
        pl.BlockSpec((tm, 512), lambda i: (i, 0)),
        pl.BlockSpec((tm, D_MODEL), lambda i: (i, 0)),
        _const_spec((8, D_MODEL)),
        _const_spec((N_BRANCH, D_MODEL)),
        _const_spec((1, 512)), _const_spec((1, 512)),
        _const_spec((A_GROUPS, BLOCK, BLOCK)),
        _const_spec((BLOCK, 512)),
        _const_spec((3, 512)), _const_spec((1, 512)),
        _const_spec((N_BRANCH, BRANCH_W, D_MODEL)),
        _const_spec((D_MODEL, D_MODEL)),
        _const_spec((1, D_MODEL)), _const_spec((1, D_MODEL)),
        _const_spec((D_MODEL, LANE)), _const_spec((1, LANE)),
    ]
    out_specs = [pl.BlockSpec((tm, D_MODEL), lambda i: (i, 0)),
                 pl.BlockSpec((tm, D_MODEL), lambda i: (i, 0)),
                 pl.BlockSpec((tm, LANE), lambda i: (i, 0)),
                 pl.BlockSpec((tm, LANE), lambda i: (i, 0))]
    out_shape = [jax.ShapeDtypeStruct((n, D_MODEL), F32), jax.ShapeDtypeStruct((n, D_MODEL), BF),
                 jax.ShapeDtypeStruct((n, LANE), jnp.int32), jax.ShapeDtypeStruct((n, LANE), F32)]
    return pl.pallas_call(
        kern, grid=(nt,), in_specs=in_specs, out_specs=out_specs, out_shape=out_shape,
        compiler_params=_cparams(("arbitrary",), 56),
        name="merge",
    )(p, p, p, p, p, p, p, p, p, p, br_c, br_d, x2d, mod3, lw['b_gate'], lw['a_ln_g'], lw['a_ln_b'],
      lw['a_ws'], lw['a_bs_full'], lw['b_conv_w'], lw['b_conv_b'], lw['w_br'], lw['w_out'],
      lw['ln1_g'], lw['ln1_b'], lw['w_router'], lw['b_router'])


def _moe_kernel(be_ref, bv_ref, x_ref, rw_ref, wgu_ref, bgu_ref, wd_ref, bd_ref, o_ref):
    i = pl.program_id(0)

    @pl.when(bv_ref[i] > 0)
    def _():
        gu = jnp.dot(x_ref[...], wgu_ref[0], preferred_element_type=F32) + bgu_ref[0]
        g = jnp.minimum(gu[:, :D_FF], SWIGLU_LIMIT)
        u = jnp.clip(gu[:, D_FF:], -SWIGLU_LIMIT, SWIGLU_LIMIT)
        act = (u + 1.0) * (g * jax.nn.sigmoid(SWIGLU_ALPHA * g))
        out = jnp.dot(act.astype(BF), wd_ref[0], preferred_element_type=F32) + bd_ref[0]
        o_ref[...] = (out * rw_ref[...]).astype(BF)

    @pl.when(bv_ref[i] == 0)
    def _():
        o_ref[...] = jnp.zeros(o_ref.shape, BF)


def _moe_rows(blk_expert, blk_valid, xg, row_w, w_gu, b_gu, w_down, b_down):
    n_rows = xg.shape[0]
    n_blk = n_rows // MOE_BLK
    grid_spec = pltpu.PrefetchScalarGridSpec(
        num_scalar_prefetch=2,
        grid=(n_blk,),
        in_specs=[pl.BlockSpec((MOE_BLK, D_MODEL), lambda i, be, bv: (i, 0)),
                  pl.BlockSpec((MOE_BLK, 1), lambda i, be, bv: (i, 0)),
                  pl.BlockSpec((1, D_MODEL, 2 * D_FF), lambda i, be, bv: (be[i], 0, 0)),
                  pl.BlockSpec((1, 1, 2 * D_FF), lambda i, be, bv: (be[i], 0, 0)),
                  pl.BlockSpec((1, D_FF, D_MODEL), lambda i, be, bv: (be[i], 0, 0)),
                  pl.BlockSpec((1, 1, D_MODEL), lambda i, be, bv: (be[i], 0, 0))],
        out_specs=pl.BlockSpec((MOE_BLK, D_MODEL), lambda i, be, bv: (i, 0)),
    )
    return pl.pallas_call(
        _moe_kernel, grid_spec=grid_spec,
        out_shape=jax.ShapeDtypeStruct((n_rows, D_MODEL), BF),
        compiler_params=_cparams(("arbitrary",), 48),
        name="moe_experts",
    )(blk_expert, blk_valid, xg, row_w, w_gu, b_gu, w_down, b_down)


def _route(top_idx, top_w):
    n_tok = top_idx.shape[0]
    n_pairs = n_tok * TOP_K
    e_flat = top_idx.reshape(-1)
    onehot = e_flat[:, None] == jnp.arange(N_EXPERTS, dtype=jnp.int32)[None, :]
    counts = jnp.sum(onehot, axis=0, dtype=jnp.int32)
    padded = (counts + MOE_BLK - 1) // MOE_BLK * MOE_BLK
    starts = jnp.cumsum(counts) - counts
    pends = jnp.cumsum(padded)
    pstarts = pends - padded
    n_rows = (n_pairs + MOE_BLK - 1) // MOE_BLK * MOE_BLK + N_EXPERTS * MOE_BLK
    n_blk = n_rows // MOE_BLK
    blk_start = jnp.arange(n_blk, dtype=jnp.int32) * MOE_BLK
    blk_expert = jnp.minimum(jnp.sum(blk_start[:, None] >= pends[None, :], axis=1, dtype=jnp.int32), N_EXPERTS - 1)
    blk_valid = (blk_start < pends[-1]).astype(jnp.int32)
    running = jnp.cumsum(onehot.astype(jnp.int32), axis=0)
    pair_row = jnp.sum(jnp.where(onehot, running - 1 + pstarts[None, :], 0), axis=1, dtype=jnp.int32)
    order = jnp.argsort(e_flat).astype(jnp.int32)
    j_in_blk = jnp.arange(MOE_BLK, dtype=jnp.int32)[None, :]
    j_in_e = (blk_start - pstarts[blk_expert])[:, None] + j_in_blk
    row_valid = (j_in_e < counts[blk_expert][:, None]) & (blk_valid[:, None] > 0)
    src = jnp.clip(starts[blk_expert][:, None] + j_in_e, 0, n_pairs - 1).reshape(-1)
    row_pair = order[src]
    row_valid = row_valid.reshape(-1)
    row_tok = jnp.where(row_valid, row_pair // TOP_K, 0)
    row_w = jnp.where(row_valid, top_w.reshape(-1)[row_pair], 0.0)
    return row_tok, row_w, pair_row, blk_expert, blk_valid


def _combine_kernel(x1_ref, f_ref, g2_ref, lng_ref, lnb_ref, o_ref):
    f = f_ref[0].astype(F32) + f_ref[1].astype(F32) + f_ref[2].astype(F32) + f_ref[3].astype(F32)
    r = DN_ALPHA * x1_ref[...] + g2_ref[...] * f
    o_ref[...] = _layer_norm_rows(r, lng_ref[...], lnb_ref[...])


def _combine(x1, f4, g2, ln_g, ln_b, *, tm):
    n = x1.shape[0]
    return pl.pallas_call(
        _combine_kernel, grid=(n // tm,),
        in_specs=[pl.BlockSpec((tm, D_MODEL), lambda i: (i, 0)),
                  pl.BlockSpec((TOP_K, tm, D_MODEL), lambda i: (0, i, 0)),
                  pl.BlockSpec((1, D_MODEL), lambda i: (0, 0)),
                  pl.BlockSpec((1, D_MODEL), lambda i: (0, 0)),
                  pl.BlockSpec((1, D_MODEL), lambda i: (0, 0))],
        out_specs=pl.BlockSpec((tm, D_MODEL), lambda i: (i, 0)),
        out_shape=jax.ShapeDtypeStruct((n, D_MODEL), F32),
        compiler_params=_cparams(("arbitrary",), 40),
        name="combine_ln2",
    )(x1, f4, g2, ln_g, ln_b)


def _rope_tables(n_tok):
    n_freq = ROPE_DIM // 4
    inv = ROPE_BASE ** (-jnp.arange(n_freq, dtype=F32) / n_freq)
    t = jnp.arange(n_tok)
    ang_r = (t // GRID_W).astype(F32)[:, None] * inv
    ang_c = (t % GRID_W).astype(F32)[:, None] * inv
    cos64 = jnp.concatenate([jnp.cos(ang_r), jnp.cos(ang_r), jnp.cos(ang_c), jnp.cos(ang_c)], axis=1)
    sin64 = jnp.concatenate([-jnp.sin(ang_r), jnp.sin(ang_r), -jnp.sin(ang_c), jnp.sin(ang_c)], axis=1)
    return jnp.concatenate([cos64, cos64], axis=1), jnp.concatenate([sin64, sin64], axis=1)


def _row_tile(n, pref):
    return pref if n % pref == 0 else n


def kernel(x, c, ctx, c_ctx, w_ada, b_ada, w_in, b_gate, a_ln_g, a_ln_b, a_ws, a_bs, b_conv_w, b_conv_b, c_lq1, c_lk1, c_lq2, c_lk2, c_subln_g, d_sink, w_br, w_out, ln1_g, ln1_b, w_router, b_router, e_w_gu, e_b_gu, e_w_down, e_b_down, ln2_g, ln2_b):
    assert x.shape[0] == 1 and ctx.shape[0] == 1
    n, m = x.shape[1], ctx.shape[1]
    assert n % 256 == 0 and m % BLOCK == 0
    xl = x[0]
    xc = ctx[0]

    c_rows = jnp.zeros((8, D_MODEL), F32).at[0].set(c[0]).at[1].set(c_ctx)
    mods = _ada(c_rows, w_ada, b_ada)
    cos_t, sin_t = _rope_tables(n)
    cos_c = jnp.ones((m, LANE), F32)
    sin_c = jnp.zeros((m, LANE), F32)
    row2 = lambda v: v.reshape(1, -1)

    for l in range(DEPTH):
        need_ctx = l < DEPTH - 1
        lam_init = 0.8 - 0.6 * math.exp(-0.3 * l)
        sh1, sc1, g1, sh2, sc2, g2 = [row2(t) for t in jnp.split(mods[l, 0], 6)]
        csh1, csc1, cg1, csh2, csc2, cg2 = [row2(t) for t in jnp.split(mods[l, 1], 6)]
        lam = (jnp.exp(jnp.sum(c_lq1[l] * c_lk1[l])) - jnp.exp(jnp.sum(c_lq2[l] * c_lk2[l]))
               + lam_init).astype(F32).reshape(1)
        lw = {
            'b_gate': b_gate[l], 'a_ln_g': row2(a_ln_g[l]), 'a_ln_b': row2(a_ln_b[l]),
            'a_ws': a_ws[l].astype(BF),
            'a_bs_full': jnp.repeat(a_bs[l].T, BLOCK, axis=1),
            'b_conv_w': b_conv_w[l], 'b_conv_b': row2(b_conv_b[l]),
            'w_br': w_br[l].astype(BF), 'w_out': w_out[l].astype(BF),
            'ln1_g': row2(ln1_g[l]), 'ln1_b': row2(ln1_b[l]),
            'w_router': jnp.pad(w_router[l], ((0, 0), (0, LANE - N_EXPERTS))).astype(BF),
            'b_router': jnp.pad(row2(b_router[l]), ((0, 0), (0, LANE - N_EXPERTS)), constant_values=NEG_BIG),
        }

        p_lat = _proj(xl, sc1, sh1, w_in[l], cos_t, sin_t, rope=True, tm=_row_tile(n, 2048))
        p_ctx = _proj(xc, csc1, csh1, w_in[l], cos_c, sin_c, rope=False, tm=m)

        sl = lambda arr, col, w: lax.slice_in_dim(arr, col, col + w, axis=1)
        k_all = jnp.concatenate([sl(p_lat, COL_CK, 512), sl(p_ctx, COL_CK, 512)], axis=0)
        v_all = jnp.concatenate([sl(p_lat, COL_CV, 512), sl(p_ctx, COL_CV, 512)], axis=0)
        br_c = _diff_attn(lam, sl(p_lat, COL_CQ, 512), k_all, v_all, c_subln_g[l], lam_init)
        br_d = _win_attn(d_sink[l], p_lat, p_ctx)

        mod3 = jnp.zeros((8, D_MODEL), F32).at[0].set(g1[0]).at[1].set(sc2[0]).at[2].set(sh2[0])
        x1, h2, tidx, tw = _merge(p_lat, br_c, br_d, xl, mod3, lw, tm=256)
        tidx, tw = tidx[:, :TOP_K], tw[:, :TOP_K]

        if need_ctx:
            cbr_c = _diff_attn(lam, sl(p_ctx, COL_CQ, 512), sl(p_ctx, COL_CK, 512), sl(p_ctx, COL_CV, 512),
                               c_subln_g[l], lam_init)
            cbr_d = _win_attn_ctx(d_sink[l], p_ctx)
            cmod3 = jnp.zeros((8, D_MODEL), F32).at[0].set(cg1[0]).at[1].set(csc2[0]).at[2].set(csh2[0])
            xc1, hc2, ctidx, ctw = _merge(p_ctx, cbr_c, cbr_d, xc, cmod3, lw, tm=_row_tile(m, 256))
            h2 = jnp.concatenate([h2, hc2], axis=0)
            tidx = jnp.concatenate([tidx, ctidx[:, :TOP_K]], axis=0)
            tw = jnp.concatenate([tw, ctw[:, :TOP_K]], axis=0)

        n_all = h2.shape[0]
        row_tok, row_w, pair_row, blk_expert, blk_valid = _route(tidx, tw)
        xg = jnp.take(h2, row_tok, axis=0)
        rows_out = _moe_rows(blk_expert, blk_valid, xg, row_w.reshape(-1, 1),
                             e_w_gu[l].astype(BF), e_b_gu[l].reshape(N_EXPERTS, 1, -1),
                             e_w_down[l].astype(BF), e_b_down[l].reshape(N_EXPERTS, 1, -1))
        f4 = jnp.take(rows_out, pair_row.reshape(n_all, TOP_K).T, axis=0)

        xl = _combine(x1, f4[:, :n], g2, row2(ln2_g[l]), row2(ln2_b[l]), tm=256)
        if need_ctx:
            xc = _combine(xc1, f4[:, n:], cg2, row2(ln2_g[l]), row2(ln2_b[l]), tm=_row_tile(m, 256))

    return xl[None]
```

```python
import functools
import math

import jax
import jax.numpy as jnp
from jax import lax
from jax.experimental import pallas as pl
from jax.experimental.pallas import tpu as pltpu

BF = jnp.bfloat16
F32 = jnp.float32

D_MODEL = 2048
DEPTH = 2
GRID_W = 64
BLOCK = 128
A_GROUPS = 4
C_HEADS = 4
C_HD = 64
C_VD = 128
D_HEADS = 8
D_KV_HEADS = 2
D_HD = 64
WINDOW = 128
N_BRANCH = 4
BRANCH_W = 512
PROJ_SIZES = (512, 512, 512, 512, 512, 512, 512, 512, 512, 128, 128, N_BRANCH * D_MODEL)
N_EXPERTS = 32
TOP_K = 4
D_FF = 1024
SWIGLU_LIMIT = 7.0
SWIGLU_ALPHA = 1.702
ROPE_DIM = 64
ROPE_BASE = 10000.0
LN_EPS = 1e-5
DN_ALPHA = (2 * DEPTH) ** 0.25

LANE = 128
NEG_BIG = -1e30

GL_W = N_BRANCH * D_MODEL
PROJ_TN = 256
SEG_W = 9 * 512 + 256
P_W = GL_W + SEG_W
N_SEG_TILES = SEG_W // PROJ_TN
COL_AU, COL_AV, COL_BX, COL_BB, COL_BC, COL_CQ, COL_CK, COL_CV, COL_DQ = [GL_W + 512 * s for s in range(9)]
COL_DK = GL_W + 9 * 512
COL_DV = COL_DK + 128
ROPE_FULL_TILES = (10, 11, 12, 13, 16, 17)
ROPE_PART_TILE = 18

MOE_BLK = 256


def _cparams(dims, vmem_mib):
    return pltpu.CompilerParams(dimension_semantics=dims, vmem_limit_bytes=vmem_mib * 1024 * 1024)


def _const_spec(shape):
    nd = len(shape)
    return pl.BlockSpec(shape, lambda *_: (0,) * nd, pipeline_mode=pl.Buffered(1))


def _layer_norm_rows(r, g, b):
    mu = jnp.mean(r, axis=-1, keepdims=True)
    d = r - mu
    var = jnp.mean(d * d, axis=-1, keepdims=True)
    return d * lax.rsqrt(var + LN_EPS) * g + b


def _ada_kernel(c_ref, w_ref, b_ref, o_ref):
    cs = c_ref[...]
    s = cs * jax.nn.sigmoid(cs)
    o_ref[0] = jnp.dot(s.astype(BF), w_ref[0].astype(BF), preferred_element_type=F32) + b_ref[0]


def _ada(c_rows, w_ada, b_ada):
    n_l, _, n_out = w_ada.shape
    tn = 1536
    return pl.pallas_call(
        _ada_kernel,
        grid=(n_l, n_out // tn),
        in_specs=[pl.BlockSpec((8, D_MODEL), lambda l, j: (0, 0)),
                  pl.BlockSpec((1, D_MODEL, tn), lambda l, j: (l, 0, j)),
                  pl.BlockSpec((1, 1, tn), lambda l, j: (l, 0, j))],
        out_specs=pl.BlockSpec((1, 8, tn), lambda l, j: (l, 0, j)),
        out_shape=jax.ShapeDtypeStruct((n_l, 8, n_out), F32),
        compiler_params=_cparams(("arbitrary", "arbitrary"), 40),
        name="ada",
    )(c_rows, w_ada, b_ada.reshape(n_l, 1, n_out))


def _rope_rotate(a, cos_ref, sin_ref):
    w = a.shape[1]
    lane = lax.broadcasted_iota(jnp.int32, a.shape, 1)
    first = jnp.bitwise_and(lane, 16) == 0
    swapped = jnp.where(first, pltpu.roll(a, w - 16, 1), pltpu.roll(a, 16, 1))
    reps = w // LANE
    cos = cos_ref[...]
    sin = sin_ref[...]
    if reps > 1:
        cos = jnp.concatenate([cos] * reps, axis=1)
        sin = jnp.concatenate([sin] * reps, axis=1)
    return a * cos + swapped * sin


def _proj_kernel(x_ref, sc_ref, sh_ref, w_ref, cos_ref, sin_ref, o_ref, h_scr, *, rope):
    j = pl.program_id(1)

    @pl.when(j == 0)
    def _():
        h_scr[...] = (x_ref[...] * (1.0 + sc_ref[...]) + sh_ref[...]).astype(BF)

    acc = jnp.dot(h_scr[...], w_ref[0].astype(BF), preferred_element_type=F32)
    if not rope:
        o_ref[...] = acc.astype(BF)
        return

    full = functools.reduce(jnp.logical_or, [j == t for t in ROPE_FULL_TILES])
    part = j == ROPE_PART_TILE

    @pl.when(full)
    def _():
        o_ref[...] = _rope_rotate(acc, cos_ref, sin_ref).astype(BF)

    @pl.when(part)
    def _():
        o_ref[:, :LANE] = _rope_rotate(acc[:, :LANE], cos_ref, sin_ref).astype(BF)
        o_ref[:, LANE:] = acc[:, LANE:].astype(BF)

    @pl.when(jnp.logical_not(jnp.logical_or(full, part)))
    def _():
        o_ref[...] = acc.astype(BF)


def _proj(layer, x2d, sc, sh, w_in, cos_t, sin_t, *, rope, tm):
    n = x2d.shape[0]
    kern = functools.partial(_proj_kernel, rope=rope)
    n_gl_tiles = GL_W // PROJ_TN
    out_tile = lambda j: jnp.where(j < N_SEG_TILES, j + n_gl_tiles, j - N_SEG_TILES)
    return pl.pallas_call(
        kern,
        grid=(n // tm, P_W // PROJ_TN),
        in_specs=[pl.BlockSpec((tm, D_MODEL), lambda i, j: (i, 0), pipeline_mode=pl.Buffered(1)),
                  pl.BlockSpec((1, D_MODEL), lambda i, j: (0, 0)),
                  pl.BlockSpec((1, D_MODEL), lambda i, j: (0, 0)),
                  pl.BlockSpec((1, D_MODEL, PROJ_TN), lambda i, j: (layer, 0, j)),
                  pl.BlockSpec((tm, LANE), lambda i, j: (i, 0)),
                  pl.BlockSpec((tm, LANE), lambda i, j: (i, 0))],
        out_specs=pl.BlockSpec((tm, PROJ_TN), lambda i, j: (i, out_tile(j))),
        out_shape=jax.ShapeDtypeStruct((n, P_W), BF),
        scratch_shapes=[pltpu.VMEM((tm, D_MODEL), BF)],
        compiler_params=_cparams(("arbitrary", "arbitrary"), 56),
        name="proj_rope" if rope else "proj_ctx",
    )(x2d, sc, sh, w_in, cos_t, sin_t)


_ACC_ROWS = 32
_ONES_ROWS = 16
LOG2E = math.log2(math.e)


def _diff_attn_kernel(lam_ref, qt_ref, k_ref, vt_ref, g_ref, o_ref, qbd_scr, m_scr, acc_scr, s_scr, mx_scr, *,
                      tq, tk, n_chunks, post_scale):
    w = 2 * tq
    qt = qt_ref[...].astype(F32) * (C_HD ** -0.5 * LOG2E)
    row = lax.broadcasted_iota(jnp.int32, qt.shape, 0)
    qbd_scr[...] = jnp.concatenate([jnp.where(row < C_HD, qt, 0.0), jnp.where(row >= C_HD, qt, 0.0)],
                                   axis=1).astype(BF)
    m_scr[...] = jnp.full(m_scr.shape, -jnp.inf, F32)
    acc_scr[...] = jnp.zeros(acc_scr.shape, F32)

    def scores(c, slot):
        k = k_ref[pl.ds(pl.multiple_of(c * tk, tk), tk), :]
        s = jnp.dot(k, qbd_scr[...], preferred_element_type=F32)
        s_scr[slot] = s
        mx_scr[slot] = jnp.max(s.reshape(tk // _ACC_ROWS, _ACC_ROWS, w), axis=0)

    def softmax_pv(c, slot):
        m_old = m_scr[...]
        m_new = jnp.maximum(m_old, jnp.max(mx_scr[slot], axis=0, keepdims=True))
        alpha = jnp.exp2(m_old - m_new)
        p = jnp.exp2(s_scr[slot] - m_new)
        acc_scr[...] = acc_scr[...] * alpha + jnp.dot(vt_ref[0, c], p.astype(BF), preferred_element_type=F32)
        m_scr[...] = m_new

    scores(0, 0)
    n_pairs = (n_chunks - 1) // 2

    def pair(i, carry):
        c = 2 * i
        softmax_pv(c, 0)
        scores(c + 1, 1)
        softmax_pv(c + 1, 1)
        scores(c + 2, 0)
        return carry

    if n_pairs > 0:
        lax.fori_loop(0, n_pairs, pair, 0)
    for c in range(2 * n_pairs, n_chunks):
        softmax_pv(c, c % 2)
        if c + 1 < n_chunks:
            scores(c + 1, (c + 1) % 2)

    o = acc_scr[:C_VD, :] / acc_scr[C_VD:C_VD + 1, :]
    od = o[:, :tq] - lam_ref[0] * o[:, tq:]
    ms = jnp.mean(od * od, axis=0, keepdims=True)
    on = od * lax.rsqrt(ms + LN_EPS) * g_ref[...] * post_scale
    o_ref[...] = on.T.astype(BF)


def _pick_tk(n_k):
    for tk in (1280, 1024, 768, 512, 256, 128):
        if n_k % tk == 0:
            return tk
    raise ValueError(f"unsupported key count {n_k}")


def _diff_attn(lam, q, k_all, v_all, subln_g, lam_init):
    n = q.shape[0]
    n_k = k_all.shape[0]
    tq = 256
    tk = _pick_tk(n_k)
    n_chunks = n_k // tk
    qt = q.T
    vt = v_all.reshape(n_chunks, tk, C_HEADS, C_VD).transpose(2, 0, 3, 1)
    vt = jnp.concatenate([vt, jnp.ones((C_HEADS, n_chunks, _ONES_ROWS, tk), BF)], axis=2)
    kern = functools.partial(_diff_attn_kernel, tq=tq, tk=tk, n_chunks=n_chunks, post_scale=1.0 - lam_init)
    return pl.pallas_call(
        kern,
        grid=(C_HEADS, n // tq),
        in_specs=[pl.BlockSpec(memory_space=pltpu.SMEM),
                  pl.BlockSpec((C_VD, tq), lambda h, i: (h, i)),
                  pl.BlockSpec((n_k, C_VD), lambda h, i: (0, h)),
                  pl.BlockSpec((1, n_chunks, C_VD + _ONES_ROWS, tk), lambda h, i: (h, 0, 0, 0)),
                  pl.BlockSpec((C_VD, 1), lambda h, i: (0, 0))],
        out_specs=pl.BlockSpec((tq, C_VD), lambda h, i: (i, h)),
        out_shape=jax.ShapeDtypeStruct((n, C_HEADS * C_VD), BF),
        scratch_shapes=[pltpu.VMEM((C_VD, 2 * tq), BF), pltpu.VMEM((1, 2 * tq), F32),
                        pltpu.VMEM((C_VD + _ONES_ROWS, 2 * tq), F32),
                        pltpu.VMEM((2, tk, 2 * tq), F32), pltpu.VMEM((2, _ACC_ROWS, 2 * tq), F32)],
        compiler_params=_cparams(("arbitrary", "arbitrary"), 48),
        name="diff_attn",
    )(lam, qt, k_all, vt, subln_g.reshape(C_VD, 1))


def _win_heads(q, kb, vb, valid, sink_ref):
    outs = []
    grp = D_HEADS // D_KV_HEADS
    for h in range(D_HEADS):
        kh = h // grp
        qh = q[:, h * D_HD:(h + 1) * D_HD]
        k_h = kb[:, kh * D_HD:(kh + 1) * D_HD]
        v_h = vb[:, kh * D_HD:(kh + 1) * D_HD]
        s = lax.dot_general(qh, k_h, (((1,), (1,)), ((), ())), preferred_element_type=F32) * (D_HD ** -0.5)
        if valid is not None:
            s = jnp.where(valid, s, NEG_BIG)
        sk = sink_ref[h]
        m = jnp.maximum(jnp.max(s, axis=-1, keepdims=True), sk)
        e = jnp.exp(s - m)
        l = jnp.sum(e, axis=-1, keepdims=True) + jnp.exp(sk - m)
        p = (e / l).astype(BF)
        outs.append(jnp.dot(p, v_h, preferred_element_type=F32))
    return jnp.concatenate(outs, axis=1).astype(BF)


def _win_attn_kernel(sink_ref, q_ref, kp_ref, kc_ref, kn_ref, vp_ref, vc_ref, vn_ref, kx_ref, vx_ref, o_ref, *,
                     n_tok, n_ctx):
    i = pl.program_id(0)
    kb = jnp.concatenate([kp_ref[...], kc_ref[...], kn_ref[...], kx_ref[...]], axis=0)
    vb = jnp.concatenate([vp_ref[...], vc_ref[...], vn_ref[...], vx_ref[...]], axis=0)
    n_keys = 3 * BLOCK + n_ctx
    r = lax.broadcasted_iota(jnp.int32, (BLOCK, n_keys), 0)
    j = lax.broadcasted_iota(jnp.int32, (BLOCK, n_keys), 1)
    kpos = (i - 1) * BLOCK + j
    dist = j - BLOCK - r
    in_band = (jnp.abs(dist) <= WINDOW) & (kpos >= 0) & (kpos < n_tok)
    valid = (j >= 3 * BLOCK) | in_band
    o_ref[...] = _win_heads(q_ref[...], kb, vb, valid, sink_ref)


def _win_attn_ctx_kernel(sink_ref, q_ref, kx_ref, vx_ref, o_ref):
    o_ref[...] = _win_heads(q_ref[...], kx_ref[...], vx_ref[...], None, sink_ref)


def _win_attn(sink, p_lat, p_ctx):
    n = p_lat.shape[0]
    m = p_ctx.shape[0]
    nb = n // BLOCK
    ck, cv, cq = COL_DK // LANE, COL_DV // LANE, COL_DQ // 512
    kern = functools.partial(_win_attn_kernel, n_tok=n, n_ctx=m)
    prev = lambda i: jnp.maximum(i - 1, 0)
    nxt = lambda i: jnp.minimum(i + 1, nb - 1)
    return pl.pallas_call(
        kern,
        grid=(nb,),
        in_specs=[pl.BlockSpec(memory_space=pltpu.SMEM),
                  pl.BlockSpec((BLOCK, 512), lambda i: (i, cq)),
                  pl.BlockSpec((BLOCK, LANE), lambda i: (prev(i), ck)),
                  pl.BlockSpec((BLOCK, LANE), lambda i: (i, ck)),
                  pl.BlockSpec((BLOCK, LANE), lambda i: (nxt(i), ck)),
                  pl.BlockSpec((BLOCK, LANE), lambda i: (prev(i), cv)),
                  pl.BlockSpec((BLOCK, LANE), lambda i: (i, cv)),
                  pl.BlockSpec((BLOCK, LANE), lambda i: (nxt(i), cv)),
                  pl.BlockSpec((m, LANE), lambda i: (0, ck)),
                  pl.BlockSpec((m, LANE), lambda i: (0, cv))],
        out_specs=pl.BlockSpec((BLOCK, 512), lambda i: (i, 0)),
        out_shape=jax.ShapeDtypeStruct((n, 512), BF),
        compiler_params=_cparams(("arbitrary",), 32),
        name="win_attn",
    )(sink, p_lat, p_lat, p_lat, p_lat, p_lat, p_lat, p_lat, p_ctx, p_ctx)


def _win_attn_ctx(sink, p_ctx):
    m = p_ctx.shape[0]
    ck, cv, cq = COL_DK // LANE, COL_DV // LANE, COL_DQ // 512
    return pl.pallas_call(
        _win_attn_ctx_kernel,
        grid=(m // BLOCK,),
        in_specs=[pl.BlockSpec(memory_space=pltpu.SMEM),
                  pl.BlockSpec((BLOCK, 512), lambda i: (i, cq)),
                  pl.BlockSpec((m, LANE), lambda i: (0, ck)),
                  pl.BlockSpec((m, LANE), lambda i: (0, cv))],
        out_specs=pl.BlockSpec((BLOCK, 512), lambda i: (i, 0)),
        out_shape=jax.ShapeDtypeStruct((m, 512), BF),
        compiler_params=_cparams(("arbitrary",), 32),
        name="win_attn_ctx",
    )(sink, p_ctx, p_ctx, p_ctx)


def _merge_kernel(gl_ref, au_ref, av_ref, bx_ref, bb_ref, bc_ref, bxp_ref, bcp_ref, bxn_ref, bcn_ref,
                  brc_ref, brd_ref, x_ref, mod_ref, bgate_ref, alng_ref, alnb_ref, ws_ref, bs_ref, cw_ref, cb_ref,
                  wbr_ref, wout_ref, ln1g_ref, ln1b_ref, wr_ref, brt_ref,
                  x1_ref, h2_ref, tidx_ref, tw_ref, *, tm):
    i = pl.program_id(0)
    last = pl.num_programs(0) - 1

    u = jax.nn.gelu(au_ref[...].astype(F32), approximate=True)
    v = jax.nn.gelu(av_ref[...].astype(F32), approximate=True)
    vn = _layer_norm_rows(v, alng_ref[...], alnb_ref[...]).astype(BF)
    blocks = []
    for b in range(tm // BLOCK):
        cols = []
        for g in range(A_GROUPS):
            vbg = vn[b * BLOCK:(b + 1) * BLOCK, g * LANE:(g + 1) * LANE]
            cols.append(jnp.dot(ws_ref[g], vbg, preferred_element_type=F32))
        blocks.append(jnp.concatenate(cols, axis=1) + bs_ref[...])
    mixed = jnp.concatenate(blocks, axis=0) if len(blocks) > 1 else blocks[0]
    br_a = u * mixed

    z = bc_ref[...].astype(F32) * bx_ref[...].astype(F32)
    z_prev = bcp_ref[7:8, :].astype(F32) * bxp_ref[7:8, :].astype(F32) * (i > 0).astype(F32)
    z_next = bcn_ref[0:1, :].astype(F32) * bxn_ref[0:1, :].astype(F32) * (i < last).astype(F32)
    row = lax.broadcasted_iota(jnp.int32, z.shape, 0)
    z_up = jnp.where(row == 0, z_prev, pltpu.roll(z, 1, 0))
    z_dn = jnp.where(row == tm - 1, z_next, pltpu.roll(z, tm - 1, 0))
    y_conv = cw_ref[0:1, :] * z_up + cw_ref[1:2, :] * z + cw_ref[2:3, :] * z_dn + cb_ref[...]
    br_b = bb_ref[...].astype(F32) * y_conv

    branches = (br_a.astype(BF), br_b.astype(BF), brc_ref[...], brd_ref[...])
    merged = None
    for g in range(N_BRANCH):
        pr = jnp.dot(branches[g], wbr_ref[g], preferred_element_type=F32)
        gate = jax.nn.sigmoid(gl_ref[:, g * D_MODEL:(g + 1) * D_MODEL].astype(F32) + bgate_ref[g:g + 1, :])
        merged = gate * pr if merged is None else merged + gate * pr
    y = jnp.dot(merged.astype(BF), wout_ref[...], preferred_element_type=F32)

    r = DN_ALPHA * x_ref[...] + mod_ref[0:1, :] * y
    x1 = _layer_norm_rows(r, ln1g_ref[...], ln1b_ref[...])
    x1_ref[...] = x1
    h2 = (x1 * (1.0 + mod_ref[1:2, :]) + mod_ref[2:3, :]).astype(BF)
    h2_ref[...] = h2
    logits = jnp.dot(h2, wr_ref[...], preferred_element_type=F32) + brt_ref[...]

    lane = lax.broadcasted_iota(jnp.int32, logits.shape, 1)
    vals, idxs = [], []
    cur = logits
    for _ in range(TOP_K):
        mx = jnp.max(cur, axis=-1, keepdims=True)
        ix = jnp.min(jnp.where(cur == mx, lane, LANE), axis=-1, keepdims=True)
        vals.append(mx)
        idxs.append(ix)
        cur = jnp.where(lane == ix, -jnp.inf, cur)
    es = [jnp.exp(vk - vals[0]) for vk in vals]
    den = es[0] + es[1] + es[2] + es[3]
    tidx = jnp.zeros(logits.shape, jnp.int32)
    tw = jnp.zeros(logits.shape, F32)
    for k in range(TOP_K):
        tidx = jnp.where(lane == k, idxs[k], tidx)
        tw = jnp.where(lane == k, es[k] / den, tw)
    tidx_ref[...] = tidx
    tw_ref[...] = tw


def _merge(p, br_c, br_d, x2d, mod3, lw, *, tm):
    n = p.shape[0]
    nt = n // tm
    r8 = tm // 8
    c512 = lambda col: col // 512
    kern = functools.partial(_merge_kernel, tm=tm)
    prev8 = lambda i: (jnp.maximum(i * r8 - 1, 0))
    next8 = lambda i: (jnp.minimum((i + 1) * r8, n // 8 - 1))
    seg = lambda col: pl.BlockSpec((tm, 512), lambda i: (i, c512(col)))
    in_specs = [
        pl.BlockSpec((tm, GL_W), lambda i: (i, 0)),
        seg(COL_AU), seg(COL_AV), seg(COL_BX), seg(COL_BB), seg(COL_BC),
        pl.BlockSpec((8, 512), lambda i: (prev8(i), c512(COL_BX))),
        pl.BlockSpec((8, 512), lambda i: (prev8(i), c512(COL_BC))),
        pl.BlockSpec((8, 512), lambda i: (next8(i), c512(COL_BX))),
        pl.BlockSpec((8, 512), lambda i: (next8(i), c512(COL_BC))),
        pl.BlockSpec((tm, 512), lambda i: (i, 0)),
        pl.BlockSpec((tm, 512), lambda i: (i, 0)),
        pl.BlockSpec((tm, D_MODEL), lambda i: (i, 0)),
        _const_spec((8, D_MODEL)),
        _const_spec((N_BRANCH, D_MODEL)),
        _const_spec((1, 512)), _const_spec((1, 512)),
        _const_spec((A_GROUPS, BLOCK, BLOCK)),
        _const_spec((BLOCK, 512)),
        _const_spec((3, 512)), _const_spec((1, 512)),
        _const_spec((N_BRANCH, BRANCH_W, D_MODEL)),
        _const_spec((D_MODEL, D_MODEL)),
        _const_spec((1, D_MODEL)), _const_spec((1, D_MODEL)),
        _const_spec((D_MODEL, LANE)), _const_spec((1, LANE)),
    ]
    out_specs = [pl.BlockSpec((tm, D_MODEL), lambda i: (i, 0)),
                 pl.BlockSpec((tm, D_MODEL), lambda i: (i, 0)),
                 pl.BlockSpec((tm, LANE), lambda i: (i, 0)),
                 pl.BlockSpec((tm, LANE), lambda i: (i, 0))]
    out_shape = [jax.ShapeDtypeStruct((n, D_MODEL), F32), jax.ShapeDtypeStruct((n, D_MODEL), BF),
                 jax.ShapeDtypeStruct((n, LANE), jnp.int32), jax.ShapeDtypeStruct((n, LANE), F32)]
    return pl.pallas_call(
        kern, grid=(nt,), in_specs=in_specs, out_specs=out_specs, out_shape=out_shape,
        compiler_params=_cparams(("arbitrary",), 56),
        name="merge",
    )(p, p, p, p, p, p, p, p, p, p, br_c, br_d, x2d, mod3, lw['b_gate'], lw['a_ln_g'], lw['a_ln_b'],
      lw['a_ws'], lw['a_bs_full'], lw['b_conv_w'], lw['b_conv_b'], lw['w_br'], lw['w_out'],
      lw['ln1_g'], lw['ln1_b'], lw['w_router'], lw['b_router'])


def _moe_kernel(be_ref, bv_ref, x_ref, rw_ref, wgu_ref, bgu_ref, wd_ref, bd_ref, o_ref, wgu_bf, wd_bf):
    i = pl.program_id(0)
    changed = jnp.logical_or(i == 0, be_ref[i] != be_ref[jnp.maximum(i - 1, 0)])

    @pl.when(changed)
    def _():
        wgu_bf[...] = wgu_ref[0, 0].astype(BF)
        wd_bf[...] = wd_ref[0, 0].astype(BF)

    @pl.when(bv_ref[i] > 0)
    def _():
        gu = jnp.dot(x_ref[...], wgu_bf[...], preferred_element_type=F32) + bgu_ref[0, 0]
        g = jnp.minimum(gu[:, :D_FF], SWIGLU_LIMIT)
        u = jnp.clip(gu[:, D_FF:], -SWIGLU_LIMIT, SWIGLU_LIMIT)
        act = (u + 1.0) * (g * jax.nn.sigmoid(SWIGLU_ALPHA * g))
        out = jnp.dot(act.astype(BF), wd_bf[...], preferred_element_type=F32) + bd_ref[0, 0]
        o_ref[...] = (out * rw_ref[...]).astype(BF)

    @pl.when(bv_ref[i] == 0)
    def _():
        o_ref[...] = jnp.zeros(o_ref.shape, BF)


def _moe_rows(layer, blk_expert, blk_valid, xg, row_w, e_w_gu, e_b_gu, e_w_down, e_b_down):
    n_rows = xg.shape[0]
    n_blk = n_rows // MOE_BLK
    once = pl.Buffered(1)
    grid_spec = pltpu.PrefetchScalarGridSpec(
        num_scalar_prefetch=2,
        grid=(n_blk,),
        in_specs=[pl.BlockSpec((MOE_BLK, D_MODEL), lambda i, be, bv: (i, 0)),
                  pl.BlockSpec((MOE_BLK, 1), lambda i, be, bv: (i, 0)),
                  pl.BlockSpec((1, 1, D_MODEL, 2 * D_FF), lambda i, be, bv: (layer, be[i], 0, 0), pipeline_mode=once),
                  pl.BlockSpec((1, 1, 1, 2 * D_FF), lambda i, be, bv: (layer, be[i], 0, 0)),
                  pl.BlockSpec((1, 1, D_FF, D_MODEL), lambda i, be, bv: (layer, be[i], 0, 0), pipeline_mode=once),
                  pl.BlockSpec((1, 1, 1, D_MODEL), lambda i, be, bv: (layer, be[i], 0, 0))],
        out_specs=pl.BlockSpec((MOE_BLK, D_MODEL), lambda i, be, bv: (i, 0)),
        scratch_shapes=[pltpu.VMEM((D_MODEL, 2 * D_FF), BF), pltpu.VMEM((D_FF, D_MODEL), BF)],
    )
    return pl.pallas_call(
        _moe_kernel, grid_spec=grid_spec,
        out_shape=jax.ShapeDtypeStruct((n_rows, D_MODEL), BF),
        compiler_params=_cparams(("arbitrary",), 56),
        name="moe_experts",
    )(blk_expert, blk_valid, xg, row_w, e_w_gu, e_b_gu, e_w_down, e_b_down)


def _route(top_idx, top_w):
    n_tok = top_idx.shape[0]
    n_pairs = n_tok * TOP_K
    e_flat = top_idx.reshape(-1)
    onehot = e_flat[:, None] == jnp.arange(N_EXPERTS, dtype=jnp.int32)[None, :]
    counts = jnp.sum(onehot, axis=0, dtype=jnp.int32)
    padded = (counts + MOE_BLK - 1) // MOE_BLK * MOE_BLK
    starts = jnp.cumsum(counts) - counts
    pends = jnp.cumsum(padded)
    pstarts = pends - padded
    n_rows = (n_pairs + MOE_BLK - 1) // MOE_BLK * MOE_BLK + N_EXPERTS * MOE_BLK
    n_blk = n_rows // MOE_BLK
    blk_start = jnp.arange(n_blk, dtype=jnp.int32) * MOE_BLK
    blk_expert = jnp.minimum(jnp.sum(blk_start[:, None] >= pends[None, :], axis=1, dtype=jnp.int32), N_EXPERTS - 1)
    blk_valid = (blk_start < pends[-1]).astype(jnp.int32)
    running = jnp.cumsum(onehot.astype(jnp.int32), axis=0)
    pair_row = jnp.sum(jnp.where(onehot, running - 1 + pstarts[None, :], 0), axis=1, dtype=jnp.int32)
    order = jnp.argsort(e_flat).astype(jnp.int32)
    j_in_blk = jnp.arange(MOE_BLK, dtype=jnp.int32)[None, :]
    j_in_e = (blk_start - pstarts[blk_expert])[:, None] + j_in_blk
    row_valid = (j_in_e < counts[blk_expert][:, None]) & (blk_valid[:, None] > 0)
    src = jnp.clip(starts[blk_expert][:, None] + j_in_e, 0, n_pairs - 1).reshape(-1)
    row_pair = order[src]
    row_valid = row_valid.reshape(-1)
    row_tok = jnp.where(row_valid, row_pair // TOP_K, 0)
    row_w = jnp.where(row_valid, top_w.reshape(-1)[row_pair], 0.0)
    return row_tok, row_w, pair_row, blk_expert, blk_valid


def _combine_kernel(x1_ref, f_ref, g2_ref, lng_ref, lnb_ref, o_ref):
    f = f_ref[0].astype(F32) + f_ref[1].astype(F32) + f_ref[2].astype(F32) + f_ref[3].astype(F32)
    r = DN_ALPHA * x1_ref[...] + g2_ref[...] * f
    o_ref[...] = _layer_norm_rows(r, lng_ref[...], lnb_ref[...])


def _combine(x1, f4, g2, ln_g, ln_b, *, tm, row_off=0):
    n = x1.shape[0]
    assert row_off % tm == 0
    blk_off = row_off // tm
    return pl.pallas_call(
        _combine_kernel, grid=(n // tm,),
        in_specs=[pl.BlockSpec((tm, D_MODEL), lambda i: (i, 0)),
                  pl.BlockSpec((TOP_K, tm, D_MODEL), lambda i: (0, i + blk_off, 0)),
                  pl.BlockSpec((1, D_MODEL), lambda i: (0, 0)),
                  pl.BlockSpec((1, D_MODEL), lambda i: (0, 0)),
                  pl.BlockSpec((1, D_MODEL), lambda i: (0, 0))],
        out_specs=pl.BlockSpec((tm, D_MODEL), lambda i: (i, 0)),
        out_shape=jax.ShapeDtypeStruct((n, D_MODEL), F32),
        compiler_params=_cparams(("arbitrary",), 40),
        name="combine_ln2",
    )(x1, f4, g2, ln_g, ln_b)


def _rope_tables(n_tok):
    n_freq = ROPE_DIM // 4
    inv = ROPE_BASE ** (-jnp.arange(n_freq, dtype=F32) / n_freq)
    t = jnp.arange(n_tok)
    ang_r = (t // GRID_W).astype(F32)[:, None] * inv
    ang_c = (t % GRID_W).astype(F32)[:, None] * inv
    cos64 = jnp.concatenate([jnp.cos(ang_r), jnp.cos(ang_r), jnp.cos(ang_c), jnp.cos(ang_c)], axis=1)
    sin64 = jnp.concatenate([-jnp.sin(ang_r), jnp.sin(ang_r), -jnp.sin(ang_c), jnp.sin(ang_c)], axis=1)
    return jnp.concatenate([cos64, cos64], axis=1), jnp.concatenate([sin64, sin64], axis=1)


def _row_tile(n, pref):
    return pref if n % pref == 0 else n


def kernel(x, c, ctx, c_ctx, w_ada, b_ada, w_in, b_gate, a_ln_g, a_ln_b, a_ws, a_bs, b_conv_w, b_conv_b, c_lq1, c_lk1, c_lq2, c_lk2, c_subln_g, d_sink, w_br, w_out, ln1_g, ln1_b, w_router, b_router, e_w_gu, e_b_gu, e_w_down, e_b_down, ln2_g, ln2_b):
    assert x.shape[0] == 1 and ctx.shape[0] == 1
    n, m = x.shape[1], ctx.shape[1]
    assert n % 256 == 0 and m % BLOCK == 0
    xl = x[0]
    xc = ctx[0]

    c_rows = jnp.zeros((8, D_MODEL), F32).at[0].set(c[0]).at[1].set(c_ctx)
    mods = _ada(c_rows, w_ada, b_ada)
    cos_t, sin_t = _rope_tables(n)
    cos_c = jnp.ones((m, LANE), F32)
    sin_c = jnp.zeros((m, LANE), F32)
    row2 = lambda v: v.reshape(1, -1)

    for l in range(DEPTH):
        need_ctx = l < DEPTH - 1
        lam_init = 0.8 - 0.6 * math.exp(-0.3 * l)
        sh1, sc1, g1, sh2, sc2, g2 = [row2(t) for t in jnp.split(mods[l, 0], 6)]
        csh1, csc1, cg1, csh2, csc2, cg2 = [row2(t) for t in jnp.split(mods[l, 1], 6)]
        lam = (jnp.exp(jnp.sum(c_lq1[l] * c_lk1[l])) - jnp.exp(jnp.sum(c_lq2[l] * c_lk2[l]))
               + lam_init).astype(F32).reshape(1)
        lw = {
            'b_gate': b_gate[l], 'a_ln_g': row2(a_ln_g[l]), 'a_ln_b': row2(a_ln_b[l]),
            'a_ws': a_ws[l].astype(BF),
            'a_bs_full': jnp.repeat(a_bs[l].T, BLOCK, axis=1),
            'b_conv_w': b_conv_w[l], 'b_conv_b': row2(b_conv_b[l]),
            'w_br': w_br[l].astype(BF), 'w_out': w_out[l].astype(BF),
            'ln1_g': row2(ln1_g[l]), 'ln1_b': row2(ln1_b[l]),
            'w_router': jnp.pad(w_router[l], ((0, 0), (0, LANE - N_EXPERTS))).astype(BF),
            'b_router': jnp.pad(row2(b_router[l]), ((0, 0), (0, LANE - N_EXPERTS)), constant_values=NEG_BIG),
        }

        p_lat = _proj(l, xl, sc1, sh1, w_in, cos_t, sin_t, rope=True, tm=_row_tile(n, 2048))
        p_ctx = _proj(l, xc, csc1, csh1, w_in, cos_c, sin_c, rope=False, tm=m)

        sl = lambda arr, col, w: lax.slice_in_dim(arr, col, col + w, axis=1)
        k_all = jnp.concatenate([sl(p_lat, COL_CK, 512), sl(p_ctx, COL_CK, 512)], axis=0)
        v_all = jnp.concatenate([sl(p_lat, COL_CV, 512), sl(p_ctx, COL_CV, 512)], axis=0)
        br_c = _diff_attn(lam, sl(p_lat, COL_CQ, 512), k_all, v_all, c_subln_g[l], lam_init)
        br_d = _win_attn(d_sink[l], p_lat, p_ctx)

        mod3 = jnp.zeros((8, D_MODEL), F32).at[0].set(g1[0]).at[1].set(sc2[0]).at[2].set(sh2[0])
        x1, h2, tidx, tw = _merge(p_lat, br_c, br_d, xl, mod3, lw, tm=256)
        tidx, tw = tidx[:, :TOP_K], tw[:, :TOP_K]

        if need_ctx:
            cbr_c = _diff_attn(lam, sl(p_ctx, COL_CQ, 512), sl(p_ctx, COL_CK, 512), sl(p_ctx, COL_CV, 512),
                               c_subln_g[l], lam_init)
            cbr_d = _win_attn_ctx(d_sink[l], p_ctx)
            cmod3 = jnp.zeros((8, D_MODEL), F32).at[0].set(cg1[0]).at[1].set(csc2[0]).at[2].set(csh2[0])
            xc1, hc2, ctidx, ctw = _merge(p_ctx, cbr_c, cbr_d, xc, cmod3, lw, tm=_row_tile(m, 256))
            h2 = jnp.concatenate([h2, hc2], axis=0)
            tidx = jnp.concatenate([tidx, ctidx[:, :TOP_K]], axis=0)
            tw = jnp.concatenate([tw, ctw[:, :TOP_K]], axis=0)

        n_all = h2.shape[0]
        row_tok, row_w, pair_row, blk_expert, blk_valid = _route(tidx, tw)
        xg = h2.at[row_tok].get(mode='promise_in_bounds')
        rows_out = _moe_rows(l, blk_expert, blk_valid, xg, row_w.reshape(-1, 1),
                             e_w_gu, e_b_gu.reshape(DEPTH, N_EXPERTS, 1, -1),
                             e_w_down, e_b_down.reshape(DEPTH, N_EXPERTS, 1, -1))
        f4 = rows_out.at[pair_row.reshape(n_all, TOP_K).T].get(mode='promise_in_bounds')

        xl = _combine(x1, f4, g2, row2(ln2_g[l]), row2(ln2_b[l]), tm=256)
        if need_ctx:
            xc = _combine(xc1, f4, cg2, row2(ln2_g[l]), row2(ln2_b[l]), tm=_row_tile(m, 256), row_off=n)

    return xl[None]
```

```python
import functools
import math

import jax
import jax.numpy as jnp
from jax import lax
from jax.experimental import pallas as pl
from jax.experimental.pallas import tpu as pltpu

BF = jnp.bfloat16
F32 = jnp.float32

D_MODEL = 2048
DEPTH = 2
GRID_W = 64
BLOCK = 128
A_GROUPS = 4
C_HEADS = 4
C_HD = 64
C_VD = 128
D_HEADS = 8
D_KV_HEADS = 2
D_HD = 64
WINDOW = 128
N_BRANCH = 4
BRANCH_W = 512
PROJ_SIZES = (512, 512, 512, 512, 512, 512, 512, 512, 512, 128, 128, N_BRANCH * D_MODEL)
N_EXPERTS = 32
TOP_K = 4
D_FF = 1024
SWIGLU_LIMIT = 7.0
SWIGLU_ALPHA = 1.702
ROPE_DIM = 64
ROPE_BASE = 10000.0
LN_EPS = 1e-5
DN_ALPHA = (2 * DEPTH) ** 0.25

LANE = 128
NEG_BIG = -1e30

GL_W = N_BRANCH * D_MODEL
PROJ_TN = 256
SEG_W = 9 * 512 + 256
P_W = GL_W + SEG_W
N_SEG_TILES = SEG_W // PROJ_TN
COL_AU, COL_AV, COL_BX, COL_BB, COL_BC, COL_CQ, COL_CK, COL_CV, COL_DQ = [GL_W + 512 * s for s in range(9)]
COL_DK = GL_W + 9 * 512
COL_DV = COL_DK + 128
ROPE_FULL_TILES = (10, 11, 12, 13, 16, 17)
ROPE_PART_TILE = 18

MOE_BLK = 256
ROW_SLAB = D_MODEL // LANE
BUF_PITCH = 24


def _cparams(dims, vmem_mib):
    return pltpu.CompilerParams(dimension_semantics=dims, vmem_limit_bytes=vmem_mib * 1024 * 1024)


def _const_spec(shape):
    nd = len(shape)
    return pl.BlockSpec(shape, lambda *_: (0,) * nd, pipeline_mode=pl.Buffered(1))


def _layer_norm_rows(r, g, b):
    mu = jnp.mean(r, axis=-1, keepdims=True)
    d = r - mu
    var = jnp.mean(d * d, axis=-1, keepdims=True)
    return d * lax.rsqrt(var + LN_EPS) * g + b


def _ada_kernel(c_ref, w_ref, b_ref, o_ref):
    cs = c_ref[...]
    s = cs * jax.nn.sigmoid(cs)
    o_ref[0] = jnp.dot(s.astype(BF), w_ref[0].astype(BF), preferred_element_type=F32) + b_ref[0]


def _ada(c_rows, w_ada, b_ada):
    n_l, _, n_out = w_ada.shape
    tn = 1536
    return pl.pallas_call(
        _ada_kernel,
        grid=(n_l, n_out // tn),
        in_specs=[pl.BlockSpec((8, D_MODEL), lambda l, j: (0, 0)),
                  pl.BlockSpec((1, D_MODEL, tn), lambda l, j: (l, 0, j)),
                  pl.BlockSpec((1, 1, tn), lambda l, j: (l, 0, j))],
        out_specs=pl.BlockSpec((1, 8, tn), lambda l, j: (l, 0, j)),
        out_shape=jax.ShapeDtypeStruct((n_l, 8, n_out), F32),
        compiler_params=_cparams(("arbitrary", "arbitrary"), 40),
        name="ada",
    )(c_rows, w_ada, b_ada.reshape(n_l, 1, n_out))


def _rope_rotate(a, cos_ref, sin_ref):
    w = a.shape[1]
    lane = lax.broadcasted_iota(jnp.int32, a.shape, 1)
    first = jnp.bitwise_and(lane, 16) == 0
    swapped = jnp.where(first, pltpu.roll(a, w - 16, 1), pltpu.roll(a, 16, 1))
    reps = w // LANE
    cos = cos_ref[...]
    sin = sin_ref[...]
    if reps > 1:
        cos = jnp.concatenate([cos] * reps, axis=1)
        sin = jnp.concatenate([sin] * reps, axis=1)
    return a * cos + swapped * sin


def _proj_kernel(x_ref, sc_ref, sh_ref, w_ref, cos_ref, sin_ref, o_ref, h_scr, *, rope):
    j = pl.program_id(1)

    @pl.when(j == 0)
    def _():
        h_scr[...] = (x_ref[...] * (1.0 + sc_ref[...]) + sh_ref[...]).astype(BF)

    acc = jnp.dot(h_scr[...], w_ref[0].astype(BF), preferred_element_type=F32)
    if not rope:
        o_ref[...] = acc.astype(BF)
        return

    full = functools.reduce(jnp.logical_or, [j == t for t in ROPE_FULL_TILES])
    part = j == ROPE_PART_TILE

    @pl.when(full)
    def _():
        o_ref[...] = _rope_rotate(acc, cos_ref, sin_ref).astype(BF)

    @pl.when(part)
    def _():
        o_ref[:, :LANE] = _rope_rotate(acc[:, :LANE], cos_ref, sin_ref).astype(BF)
        o_ref[:, LANE:] = acc[:, LANE:].astype(BF)

    @pl.when(jnp.logical_not(jnp.logical_or(full, part)))
    def _():
        o_ref[...] = acc.astype(BF)


def _proj(layer, x2d, sc, sh, w_in, cos_t, sin_t, *, rope, tm):
    n = x2d.shape[0]
    kern = functools.partial(_proj_kernel, rope=rope)
    n_gl_tiles = GL_W // PROJ_TN
    out_tile = lambda j: jnp.where(j < N_SEG_TILES, j + n_gl_tiles, j - N_SEG_TILES)
    return pl.pallas_call(
        kern,
        grid=(n // tm, P_W // PROJ_TN),
        in_specs=[pl.BlockSpec((tm, D_MODEL), lambda i, j: (i, 0), pipeline_mode=pl.Buffered(1)),
                  pl.BlockSpec((1, D_MODEL), lambda i, j: (0, 0)),
                  pl.BlockSpec((1, D_MODEL), lambda i, j: (0, 0)),
                  pl.BlockSpec((1, D_MODEL, PROJ_TN), lambda i, j: (layer, 0, j)),
                  pl.BlockSpec((tm, LANE), lambda i, j: (i, 0)),
                  pl.BlockSpec((tm, LANE), lambda i, j: (i, 0))],
        out_specs=pl.BlockSpec((tm, PROJ_TN), lambda i, j: (i, out_tile(j))),
        out_shape=jax.ShapeDtypeStruct((n, P_W), BF),
        scratch_shapes=[pltpu.VMEM((tm, D_MODEL), BF)],
        compiler_params=_cparams(("arbitrary", "arbitrary"), 56),
        name="proj_rope" if rope else "proj_ctx",
    )(x2d, sc, sh, w_in, cos_t, sin_t)


_ACC_ROWS = 32
_ONES_ROWS = 16
LOG2E = math.log2(math.e)


def _diff_attn_kernel(lam_ref, qt_ref, k_ref, vt_ref, g_ref, o_ref, qbd_scr, m_scr, acc_scr, s_scr, mx_scr, *,
                      tq, tk, n_chunks, post_scale):
    w = 2 * tq
    qt = qt_ref[...].astype(F32) * (C_HD ** -0.5 * LOG2E)
    row = lax.broadcasted_iota(jnp.int32, qt.shape, 0)
    qbd_scr[...] = jnp.concatenate([jnp.where(row < C_HD, qt, 0.0), jnp.where(row >= C_HD, qt, 0.0)],
                                   axis=1).astype(BF)
    m_scr[...] = jnp.full(m_scr.shape, -jnp.inf, F32)
    acc_scr[...] = jnp.zeros(acc_scr.shape, F32)

    def scores(c, slot):
        k = k_ref[pl.ds(pl.multiple_of(c * tk, tk), tk), :]
        s = jnp.dot(k, qbd_scr[...], preferred_element_type=F32)
        s_scr[slot] = s
        mx_scr[slot] = jnp.max(s.reshape(tk // _ACC_ROWS, _ACC_ROWS, w), axis=0)

    def softmax_pv(c, slot):
        m_old = m_scr[...]
        m_new = jnp.maximum(m_old, jnp.max(mx_scr[slot], axis=0, keepdims=True))
        alpha = jnp.exp2(m_old - m_new)
        p = jnp.exp2(s_scr[slot] - m_new)
        acc_scr[...] = acc_scr[...] * alpha + jnp.dot(vt_ref[0, c], p.astype(BF), preferred_element_type=F32)
        m_scr[...] = m_new

    scores(0, 0)
    n_pairs = (n_chunks - 1) // 2

    def pair(i, carry):
        c = 2 * i
        softmax_pv(c, 0)
        scores(c + 1, 1)
        softmax_pv(c + 1, 1)
        scores(c + 2, 0)
        return carry

    if n_pairs > 0:
        lax.fori_loop(0, n_pairs, pair, 0)
    for c in range(2 * n_pairs, n_chunks):
        softmax_pv(c, c % 2)
        if c + 1 < n_chunks:
            scores(c + 1, (c + 1) % 2)

    o = acc_scr[:C_VD, :] / acc_scr[C_VD:C_VD + 1, :]
    od = o[:, :tq] - lam_ref[0] * o[:, tq:]
    ms = jnp.mean(od * od, axis=0, keepdims=True)
    on = od * lax.rsqrt(ms + LN_EPS) * g_ref[...] * post_scale
    o_ref[...] = on.T.astype(BF)


def _pick_tk(n_k):
    for tk in (1280, 1024, 768, 512, 256, 128):
        if n_k % tk == 0:
            return tk
    raise ValueError(f"unsupported key count {n_k}")


def _diff_attn(lam, q, k_all, v_all, subln_g, lam_init):
    n = q.shape[0]
    n_k = k_all.shape[0]
    tq = 256
    tk = _pick_tk(n_k)
    n_chunks = n_k // tk
    qt = q.T
    vt = v_all.reshape(n_chunks, tk, C_HEADS, C_VD).transpose(2, 0, 3, 1)
    vt = jnp.concatenate([vt, jnp.ones((C_HEADS, n_chunks, _ONES_ROWS, tk), BF)], axis=2)
    kern = functools.partial(_diff_attn_kernel, tq=tq, tk=tk, n_chunks=n_chunks, post_scale=1.0 - lam_init)
    return pl.pallas_call(
        kern,
        grid=(C_HEADS, n // tq),
        in_specs=[pl.BlockSpec(memory_space=pltpu.SMEM),
                  pl.BlockSpec((C_VD, tq), lambda h, i: (h, i)),
                  pl.BlockSpec((n_k, C_VD), lambda h, i: (0, h)),
                  pl.BlockSpec((1, n_chunks, C_VD + _ONES_ROWS, tk), lambda h, i: (h, 0, 0, 0)),
                  pl.BlockSpec((C_VD, 1), lambda h, i: (0, 0))],
        out_specs=pl.BlockSpec((tq, C_VD), lambda h, i: (i, h)),
        out_shape=jax.ShapeDtypeStruct((n, C_HEADS * C_VD), BF),
        scratch_shapes=[pltpu.VMEM((C_VD, 2 * tq), BF), pltpu.VMEM((1, 2 * tq), F32),
                        pltpu.VMEM((C_VD + _ONES_ROWS, 2 * tq), F32),
                        pltpu.VMEM((2, tk, 2 * tq), F32), pltpu.VMEM((2, _ACC_ROWS, 2 * tq), F32)],
        compiler_params=_cparams(("arbitrary", "arbitrary"), 48),
        name="diff_attn",
    )(lam, qt, k_all, vt, subln_g.reshape(C_VD, 1))


def _win_heads(q, kb, vb, valid, sink_ref):
    outs = []
    grp = D_HEADS // D_KV_HEADS
    for h in range(D_HEADS):
        kh = h // grp
        qh = q[:, h * D_HD:(h + 1) * D_HD]
        k_h = kb[:, kh * D_HD:(kh + 1) * D_HD]
        v_h = vb[:, kh * D_HD:(kh + 1) * D_HD]
        s = lax.dot_general(qh, k_h, (((1,), (1,)), ((), ())), preferred_element_type=F32) * (D_HD ** -0.5)
        if valid is not None:
            s = jnp.where(valid, s, NEG_BIG)
        sk = sink_ref[h]
        m = jnp.maximum(jnp.max(s, axis=-1, keepdims=True), sk)
        e = jnp.exp(s - m)
        l = jnp.sum(e, axis=-1, keepdims=True) + jnp.exp(sk - m)
        p = (e / l).astype(BF)
        outs.append(jnp.dot(p, v_h, preferred_element_type=F32))
    return jnp.concatenate(outs, axis=1).astype(BF)


def _win_attn_kernel(sink_ref, q_ref, kp_ref, kc_ref, kn_ref, vp_ref, vc_ref, vn_ref, kx_ref, vx_ref, o_ref, *,
                     n_tok, n_ctx):
    i = pl.program_id(0)
    kb = jnp.concatenate([kp_ref[...], kc_ref[...], kn_ref[...], kx_ref[...]], axis=0)
    vb = jnp.concatenate([vp_ref[...], vc_ref[...], vn_ref[...], vx_ref[...]], axis=0)
    n_keys = 3 * BLOCK + n_ctx
    r = lax.broadcasted_iota(jnp.int32, (BLOCK, n_keys), 0)
    j = lax.broadcasted_iota(jnp.int32, (BLOCK, n_keys), 1)
    kpos = (i - 1) * BLOCK + j
    dist = j - BLOCK - r
    in_band = (jnp.abs(dist) <= WINDOW) & (kpos >= 0) & (kpos < n_tok)
    valid = (j >= 3 * BLOCK) | in_band
    o_ref[...] = _win_heads(q_ref[...], kb, vb, valid, sink_ref)


def _win_attn_ctx_kernel(sink_ref, q_ref, kx_ref, vx_ref, o_ref):
    o_ref[...] = _win_heads(q_ref[...], kx_ref[...], vx_ref[...], None, sink_ref)


def _win_attn(sink, p_lat, p_ctx):
    n = p_lat.shape[0]
    m = p_ctx.shape[0]
    nb = n // BLOCK
    ck, cv, cq = COL_DK // LANE, COL_DV // LANE, COL_DQ // 512
    kern = functools.partial(_win_attn_kernel, n_tok=n, n_ctx=m)
    prev = lambda i: jnp.maximum(i - 1, 0)
    nxt = lambda i: jnp.minimum(i + 1, nb - 1)
    return pl.pallas_call(
        kern,
        grid=(nb,),
        in_specs=[pl.BlockSpec(memory_space=pltpu.SMEM),
                  pl.BlockSpec((BLOCK, 512), lambda i: (i, cq)),
                  pl.BlockSpec((BLOCK, LANE), lambda i: (prev(i), ck)),
                  pl.BlockSpec((BLOCK, LANE), lambda i: (i, ck)),
                  pl.BlockSpec((BLOCK, LANE), lambda i: (nxt(i), ck)),
                  pl.BlockSpec((BLOCK, LANE), lambda i: (prev(i), cv)),
                  pl.BlockSpec((BLOCK, LANE), lambda i: (i, cv)),
                  pl.BlockSpec((BLOCK, LANE), lambda i: (nxt(i), cv)),
                  pl.BlockSpec((m, LANE), lambda i: (0, ck)),
                  pl.BlockSpec((m, LANE), lambda i: (0, cv))],
        out_specs=pl.BlockSpec((BLOCK, 512), lambda i: (i, 0)),
        out_shape=jax.ShapeDtypeStruct((n, 512), BF),
        compiler_params=_cparams(("arbitrary",), 32),
        name="win_attn",
    )(sink, p_lat, p_lat, p_lat, p_lat, p_lat, p_lat, p_lat, p_ctx, p_ctx)


def _win_attn_ctx(sink, p_ctx):
    m = p_ctx.shape[0]
    ck, cv, cq = COL_DK // LANE, COL_DV // LANE, COL_DQ // 512
    return pl.pallas_call(
        _win_attn_ctx_kernel,
        grid=(m // BLOCK,),
        in_specs=[pl.BlockSpec(memory_space=pltpu.SMEM),
                  pl.BlockSpec((BLOCK, 512), lambda i: (i, cq)),
                  pl.BlockSpec((m, LANE), lambda i: (0, ck)),
                  pl.BlockSpec((m, LANE), lambda i: (0, cv))],
        out_specs=pl.BlockSpec((BLOCK, 512), lambda i: (i, 0)),
        out_shape=jax.ShapeDtypeStruct((m, 512), BF),
        compiler_params=_cparams(("arbitrary",), 32),
        name="win_attn_ctx",
    )(sink, p_ctx, p_ctx, p_ctx)


def _merge_kernel(gl_ref, au_ref, av_ref, bx_ref, bb_ref, bc_ref, bxp_ref, bcp_ref, bxn_ref, bcn_ref,
                  brc_ref, brd_ref, x_ref, mod_ref, bgate_ref, alng_ref, alnb_ref, ws_ref, bs_ref, cw_ref, cb_ref,
                  wbr_ref, wout_ref, ln1g_ref, ln1b_ref, wr_ref, brt_ref,
                  x1_ref, h2_ref, tidx_ref, tw_ref, *, tm):
    i = pl.program_id(0)
    last = pl.num_programs(0) - 1

    u = jax.nn.gelu(au_ref[...].astype(F32), approximate=True)
    v = jax.nn.gelu(av_ref[...].astype(F32), approximate=True)
    vn = _layer_norm_rows(v, alng_ref[...], alnb_ref[...]).astype(BF)
    blocks = []
    for b in range(tm // BLOCK):
        cols = []
        for g in range(A_GROUPS):
            vbg = vn[b * BLOCK:(b + 1) * BLOCK, g * LANE:(g + 1) * LANE]
            cols.append(jnp.dot(ws_ref[g], vbg, preferred_element_type=F32))
        blocks.append(jnp.concatenate(cols, axis=1) + bs_ref[...])
    mixed = jnp.concatenate(blocks, axis=0) if len(blocks) > 1 else blocks[0]
    br_a = u * mixed

    z = bc_ref[...].astype(F32) * bx_ref[...].astype(F32)
    z_prev = bcp_ref[7:8, :].astype(F32) * bxp_ref[7:8, :].astype(F32) * (i > 0).astype(F32)
    z_next = bcn_ref[0:1, :].astype(F32) * bxn_ref[0:1, :].astype(F32) * (i < last).astype(F32)
    row = lax.broadcasted_iota(jnp.int32, z.shape, 0)
    z_up = jnp.where(row == 0, z_prev, pltpu.roll(z, 1, 0))
    z_dn = jnp.where(row == tm - 1, z_next, pltpu.roll(z, tm - 1, 0))
    y_conv = cw_ref[0:1, :] * z_up + cw_ref[1:2, :] * z + cw_ref[2:3, :] * z_dn + cb_ref[...]
    br_b = bb_ref[...].astype(F32) * y_conv

    branches = (br_a.astype(BF), br_b.astype(BF), brc_ref[...], brd_ref[...])
    merged = None
    for g in range(N_BRANCH):
        pr = jnp.dot(branches[g], wbr_ref[g], preferred_element_type=F32)
        gate = jax.nn.sigmoid(gl_ref[:, g * D_MODEL:(g + 1) * D_MODEL].astype(F32) + bgate_ref[g:g + 1, :])
        merged = gate * pr if merged is None else merged + gate * pr
    y = jnp.dot(merged.astype(BF), wout_ref[...], preferred_element_type=F32)

    r = DN_ALPHA * x_ref[...] + mod_ref[0:1, :] * y
    x1 = _layer_norm_rows(r, ln1g_ref[...], ln1b_ref[...])
    x1_ref[...] = x1
    h2f = x1 * (1.0 + mod_ref[1:2, :]) + mod_ref[2:3, :]
    for j in range(ROW_SLAB):
        h2_ref[pl.ds(j, tm, stride=ROW_SLAB), :] = h2f[:, j * LANE:(j + 1) * LANE]
    h2 = h2f.astype(BF)
    logits = jnp.dot(h2, wr_ref[...], preferred_element_type=F32) + brt_ref[...]

    lane = lax.broadcasted_iota(jnp.int32, logits.shape, 1)
    vals, idxs = [], []
    cur = logits
    for _ in range(TOP_K):
        mx = jnp.max(cur, axis=-1, keepdims=True)
        ix = jnp.min(jnp.where(cur == mx, lane, LANE), axis=-1, keepdims=True)
        vals.append(mx)
        idxs.append(ix)
        cur = jnp.where(lane == ix, -jnp.inf, cur)
    es = [jnp.exp(vk - vals[0]) for vk in vals]
    den = es[0] + es[1] + es[2] + es[3]
    tidx = jnp.zeros(logits.shape, jnp.int32)
    tw = jnp.zeros(logits.shape, F32)
    for k in range(TOP_K):
        tidx = jnp.where(lane == k, idxs[k], tidx)
        tw = jnp.where(lane == k, es[k] / den, tw)
    tidx_ref[...] = tidx
    tw_ref[...] = tw


def _merge(p, br_c, br_d, x2d, mod3, lw, *, tm):
    n = p.shape[0]
    nt = n // tm
    r8 = tm // 8
    c512 = lambda col: col // 512
    kern = functools.partial(_merge_kernel, tm=tm)
    prev8 = lambda i: (jnp.maximum(i * r8 - 1, 0))
    next8 = lambda i: (jnp.minimum((i + 1) * r8, n // 8 - 1))
    seg = lambda col: pl.BlockSpec((tm, 512), lambda i: (i, c512(col)))
    in_specs = [
        pl.BlockSpec((tm, GL_W), lambda i: (i, 0)),
        seg(COL_AU), seg(COL_AV), seg(COL_BX), seg(COL_BB), seg(COL_BC),
        pl.BlockSpec((8, 512), lambda i: (prev8(i), c512(COL_BX))),
        pl.BlockSpec((8, 512), lambda i: (prev8(i), c512(COL_BC))),
        pl.BlockSpec((8, 512), lambda i: (next8(i), c512(COL_BX))),
        pl.BlockSpec((8, 512), lambda i: (next8(i), c512(COL_BC))),
        pl.BlockSpec((tm, 512), lambda i: (i, 0)),
        pl.BlockSpec((tm, 512), lambda i: (i, 0)),
        pl.BlockSpec((tm, D_MODEL), lambda i: (i, 0)),
        _const_spec((8, D_MODEL)),
        _const_spec((N_BRANCH, D_MODEL)),
        _const_spec((1, 512)), _const_spec((1, 512)),
        _const_spec((A_GROUPS, BLOCK, BLOCK)),
        _const_spec((BLOCK, 512)),
        _const_spec((3, 512)), _const_spec((1, 512)),
        _const_spec((N_BRANCH, BRANCH_W, D_MODEL)),
        _const_spec((D_MODEL, D_MODEL)),
        _const_spec((1, D_MODEL)), _const_spec((1, D_MODEL)),
        _const_spec((D_MODEL, LANE)), _const_spec((1, LANE)),
    ]
    out_specs = [pl.BlockSpec((tm, D_MODEL), lambda i: (i, 0)),
                 pl.BlockSpec((tm * ROW_SLAB, LANE), lambda i: (i, 0)),
                 pl.BlockSpec((tm, LANE), lambda i: (i, 0)),
                 pl.BlockSpec((tm, LANE), lambda i: (i, 0))]
    out_shape = [jax.ShapeDtypeStruct((n, D_MODEL), F32), jax.ShapeDtypeStruct((n * ROW_SLAB, LANE), F32),
                 jax.ShapeDtypeStruct((n, LANE), jnp.int32), jax.ShapeDtypeStruct((n, LANE), F32)]
    return pl.pallas_call(
        kern, grid=(nt,), in_specs=in_specs, out_specs=out_specs, out_shape=out_shape,
        compiler_params=_cparams(("arbitrary",), 56),
        name="merge",
    )(p, p, p, p, p, p, p, p, p, p, br_c, br_d, x2d, mod3, lw['b_gate'], lw['a_ln_g'], lw['a_ln_b'],
      lw['a_ws'], lw['a_bs_full'], lw['b_conv_w'], lw['b_conv_b'], lw['w_br'], lw['w_out'],
      lw['ln1_g'], lw['ln1_b'], lw['w_router'], lw['b_router'])


def _moe_kernel(be_ref, bc_ref, tokc_ref, tokn_ref, pair_ref, rw_ref, wgu_ref, bgu_ref, wd_ref, bd_ref, h2_hbm,
                out_hbm, wgu_bf, wd_bf, xbuf, obuf, gsem, ssem):
    i = pl.program_id(0)
    slot = lax.rem(i, 2)
    cnt = bc_ref[i]
    prev_cnt = jnp.where(i > 0, bc_ref[jnp.maximum(i - 1, 0)], 0)
    first = jnp.logical_and(i == 0, cnt > 0)
    changed = jnp.logical_or(i == 0, be_ref[i] != be_ref[jnp.maximum(i - 1, 0)])

    def gather_copy(tok, r, sl):
        return pltpu.make_async_copy(
            h2_hbm.at[pl.ds(pl.multiple_of(tok * ROW_SLAB, ROW_SLAB), ROW_SLAB), :],
            xbuf.at[sl, pl.ds(r * BUF_PITCH, ROW_SLAB), :], gsem.at[sl])

    def scatter_copy(pair, r):
        tok = lax.shift_right_logical(pair, 2)
        return pltpu.make_async_copy(
            obuf.at[pl.ds(pl.multiple_of(r * BUF_PITCH, 8), ROW_SLAB), :],
            out_hbm.at[jnp.bitwise_and(pair, TOP_K - 1), pl.ds(pl.multiple_of(tok * ROW_SLAB, ROW_SLAB), ROW_SLAB), :],
            ssem.at[0])

    def wait_prev_scatter():
        @pl.when(prev_cnt == MOE_BLK)
        def _():
            for r in range(MOE_BLK):
                scatter_copy(0, r).wait()

        @pl.when(jnp.logical_and(prev_cnt > 0, prev_cnt < MOE_BLK))
        def _():
            def one(r, carry):
                scatter_copy(0, r).wait()
                return carry
            lax.fori_loop(0, prev_cnt, one, 0)

    def block(full):
        for r in range(MOE_BLK):
            gather_copy(tokn_ref[0, 0, r], r, 1 - slot).start()
        x = jnp.concatenate([xbuf[slot, pl.ds(j, MOE_BLK, stride=BUF_PITCH), :] for j in range(ROW_SLAB)],
                            axis=1).astype(BF)
        gu = jnp.dot(x, wgu_bf[...], preferred_element_type=F32) + bgu_ref[0, 0]
        g = jnp.minimum(gu[:, :D_FF], SWIGLU_LIMIT)
        u = jnp.clip(gu[:, D_FF:], -SWIGLU_LIMIT, SWIGLU_LIMIT)
        act = (u + 1.0) * (g * jax.nn.sigmoid(SWIGLU_ALPHA * g))
        out = (jnp.dot(act.astype(BF), wd_bf[...], preferred_element_type=F32) + bd_ref[0, 0]) * rw_ref[...]
        wait_prev_scatter()
        for j in range(ROW_SLAB):
            obuf[pl.ds(j, MOE_BLK, stride=BUF_PITCH), :] = out[:, j * LANE:(j + 1) * LANE]
        if full:
            for r in range(MOE_BLK):
                scatter_copy(pair_ref[0, 0, r], r).start()
        else:
            def one(r, carry):
                scatter_copy(pair_ref[0, 0, r], r).start()
                return carry
            lax.fori_loop(0, cnt, one, 0)

    @pl.when(first)
    def _():
        for r in range(MOE_BLK):
            gather_copy(tokc_ref[0, 0, r], r, 0).start()

    @pl.when(jnp.logical_or(first, prev_cnt > 0))
    def _():
        for r in range(MOE_BLK):
            gather_copy(0, r, slot).wait()

    @pl.when(changed)
    def _():
        wgu_bf[...] = wgu_ref[0, 0].astype(BF)
        wd_bf[...] = wd_ref[0, 0].astype(BF)

    @pl.when(cnt == MOE_BLK)
    def _():
        block(True)

    @pl.when(jnp.logical_and(cnt > 0, cnt < MOE_BLK))
    def _():
        block(False)

    @pl.when(cnt == 0)
    def _():
        wait_prev_scatter()


def _moe_rows(layer, n_all, blk_expert, blk_cnt, row_tok, row_pair, row_w, h2_rows, e_w_gu, e_b_gu, e_w_down,
              e_b_down):
    n_blk = blk_expert.shape[0]
    once = pl.Buffered(1)
    idx_blk = lambda f: pl.BlockSpec((1, 1, MOE_BLK), f, memory_space=pltpu.SMEM)
    grid_spec = pltpu.PrefetchScalarGridSpec(
        num_scalar_prefetch=2,
        grid=(n_blk,),
        in_specs=[idx_blk(lambda i, be, bc: (i, 0, 0)),
                  idx_blk(lambda i, be, bc: (jnp.minimum(i + 1, n_blk - 1), 0, 0)),
                  idx_blk(lambda i, be, bc: (i, 0, 0)),
                  pl.BlockSpec((MOE_BLK, 1), lambda i, be, bc: (i, 0)),
                  pl.BlockSpec((1, 1, D_MODEL, 2 * D_FF), lambda i, be, bc: (layer, be[i], 0, 0), pipeline_mode=once),
                  pl.BlockSpec((1, 1, 1, 2 * D_FF), lambda i, be, bc: (layer, be[i], 0, 0)),
                  pl.BlockSpec((1, 1, D_FF, D_MODEL), lambda i, be, bc: (layer, be[i], 0, 0), pipeline_mode=once),
                  pl.BlockSpec((1, 1, 1, D_MODEL), lambda i, be, bc: (layer, be[i], 0, 0)),
                  pl.BlockSpec(memory_space=pl.ANY)],
        out_specs=pl.BlockSpec(memory_space=pl.ANY),
        scratch_shapes=[pltpu.VMEM((D_MODEL, 2 * D_FF), BF), pltpu.VMEM((D_FF, D_MODEL), BF),
                        pltpu.VMEM((2, MOE_BLK * BUF_PITCH, LANE), F32), pltpu.VMEM((MOE_BLK * BUF_PITCH, LANE), F32),
                        pltpu.SemaphoreType.DMA((2,)), pltpu.SemaphoreType.DMA((1,))],
    )
    tok3 = row_tok.reshape(n_blk, 1, MOE_BLK)
    return pl.pallas_call(
        _moe_kernel, grid_spec=grid_spec,
        out_shape=jax.ShapeDtypeStruct((TOP_K, n_all * ROW_SLAB, LANE), F32),
        compiler_params=_cparams(("arbitrary",), 58),
        name="moe_experts",
    )(blk_expert, blk_cnt, tok3, tok3, row_pair.reshape(n_blk, 1, MOE_BLK), row_w, e_w_gu, e_b_gu, e_w_down,
      e_b_down, h2_rows)


def _route(top_idx, top_w):
    n_tok = top_idx.shape[0]
    n_pairs = n_tok * TOP_K
    e_flat = top_idx.reshape(-1)
    onehot = e_flat[:, None] == jnp.arange(N_EXPERTS, dtype=jnp.int32)[None, :]
    counts = jnp.sum(onehot, axis=0, dtype=jnp.int32)
    padded = (counts + MOE_BLK - 1) // MOE_BLK * MOE_BLK
    starts = jnp.cumsum(counts) - counts
    pends = jnp.cumsum(padded)
    pstarts = pends - padded
    n_rows = (n_pairs + MOE_BLK - 1) // MOE_BLK * MOE_BLK + N_EXPERTS * MOE_BLK
    n_blk = n_rows // MOE_BLK
    blk_start = jnp.arange(n_blk, dtype=jnp.int32) * MOE_BLK
    blk_expert = jnp.minimum(jnp.sum(blk_start[:, None] >= pends[None, :], axis=1, dtype=jnp.int32), N_EXPERTS - 1)
    blk_valid = (blk_start < pends[-1]).astype(jnp.int32)
    order = jnp.argsort(e_flat).astype(jnp.int32)
    j_in_blk = jnp.arange(MOE_BLK, dtype=jnp.int32)[None, :]
    j_in_e = (blk_start - pstarts[blk_expert])[:, None] + j_in_blk
    row_valid = (j_in_e < counts[blk_expert][:, None]) & (blk_valid[:, None] > 0)
    src = jnp.clip(starts[blk_expert][:, None] + j_in_e, 0, n_pairs - 1).reshape(-1)
    row_pair = order[src]
    row_valid = row_valid.reshape(-1)
    row_tok = jnp.where(row_valid, row_pair // TOP_K, 0)
    row_w = jnp.where(row_valid, top_w.reshape(-1)[row_pair], 0.0)
    row_pair = jnp.where(row_valid, row_pair, 0)
    blk_cnt = jnp.clip(counts[blk_expert] - (blk_start - pstarts[blk_expert]), 0, MOE_BLK) * blk_valid
    return row_tok, row_w, row_pair, blk_expert, blk_cnt


def _combine_kernel(x1_ref, r_ref, g2_ref, lng_ref, lnb_ref, o_ref, *, tm):
    f = None
    for k in range(TOP_K):
        fk = jnp.concatenate([r_ref[k, pl.ds(j, tm, stride=ROW_SLAB), :] for j in range(ROW_SLAB)], axis=1)
        f = fk if f is None else f + fk
    r = DN_ALPHA * x1_ref[...] + g2_ref[...] * f
    o_ref[...] = _layer_norm_rows(r, lng_ref[...], lnb_ref[...])


def _combine(x1, rows, g2, ln_g, ln_b, *, tm, row_off=0):
    n = x1.shape[0]
    assert row_off % tm == 0
    blk_off = row_off // tm
    return pl.pallas_call(
        functools.partial(_combine_kernel, tm=tm), grid=(n // tm,),
        in_specs=[pl.BlockSpec((tm, D_MODEL), lambda i: (i, 0)),
                  pl.BlockSpec((TOP_K, tm * ROW_SLAB, LANE), lambda i: (0, i + blk_off, 0)),
                  pl.BlockSpec((1, D_MODEL), lambda i: (0, 0)),
                  pl.BlockSpec((1, D_MODEL), lambda i: (0, 0)),
                  pl.BlockSpec((1, D_MODEL), lambda i: (0, 0))],
        out_specs=pl.BlockSpec((tm, D_MODEL), lambda i: (i, 0)),
        out_shape=jax.ShapeDtypeStruct((n, D_MODEL), F32),
        compiler_params=_cparams(("arbitrary",), 48),
        name="combine_ln2",
    )(x1, rows, g2, ln_g, ln_b)


def _rope_tables(n_tok):
    n_freq = ROPE_DIM // 4
    inv = ROPE_BASE ** (-jnp.arange(n_freq, dtype=F32) / n_freq)
    t = jnp.arange(n_tok)
    ang_r = (t // GRID_W).astype(F32)[:, None] * inv
    ang_c = (t % GRID_W).astype(F32)[:, None] * inv
    cos64 = jnp.concatenate([jnp.cos(ang_r), jnp.cos(ang_r), jnp.cos(ang_c), jnp.cos(ang_c)], axis=1)
    sin64 = jnp.concatenate([-jnp.sin(ang_r), jnp.sin(ang_r), -jnp.sin(ang_c), jnp.sin(ang_c)], axis=1)
    return jnp.concatenate([cos64, cos64], axis=1), jnp.concatenate([sin64, sin64], axis=1)


def _row_tile(n, pref):
    return pref if n % pref == 0 else n


def kernel(x, c, ctx, c_ctx, w_ada, b_ada, w_in, b_gate, a_ln_g, a_ln_b, a_ws, a_bs, b_conv_w, b_conv_b, c_lq1, c_lk1, c_lq2, c_lk2, c_subln_g, d_sink, w_br, w_out, ln1_g, ln1_b, w_router, b_router, e_w_gu, e_b_gu, e_w_down, e_b_down, ln2_g, ln2_b):
    assert x.shape[0] == 1 and ctx.shape[0] == 1
    n, m = x.shape[1], ctx.shape[1]
    assert n % 256 == 0 and m % BLOCK == 0
    xl = x[0]
    xc = ctx[0]

    c_rows = jnp.zeros((8, D_MODEL), F32).at[0].set(c[0]).at[1].set(c_ctx)
    mods = _ada(c_rows, w_ada, b_ada)
    cos_t, sin_t = _rope_tables(n)
    cos_c = jnp.ones((m, LANE), F32)
    sin_c = jnp.zeros((m, LANE), F32)
    row2 = lambda v: v.reshape(1, -1)

    for l in range(DEPTH):
        need_ctx = l < DEPTH - 1
        lam_init = 0.8 - 0.6 * math.exp(-0.3 * l)
        sh1, sc1, g1, sh2, sc2, g2 = [row2(t) for t in jnp.split(mods[l, 0], 6)]
        csh1, csc1, cg1, csh2, csc2, cg2 = [row2(t) for t in jnp.split(mods[l, 1], 6)]
        lam = (jnp.exp(jnp.sum(c_lq1[l] * c_lk1[l])) - jnp.exp(jnp.sum(c_lq2[l] * c_lk2[l]))
               + lam_init).astype(F32).reshape(1)
        lw = {
            'b_gate': b_gate[l], 'a_ln_g': row2(a_ln_g[l]), 'a_ln_b': row2(a_ln_b[l]),
            'a_ws': a_ws[l].astype(BF),
            'a_bs_full': jnp.repeat(a_bs[l].T, BLOCK, axis=1),
            'b_conv_w': b_conv_w[l], 'b_conv_b': row2(b_conv_b[l]),
            'w_br': w_br[l].astype(BF), 'w_out': w_out[l].astype(BF),
            'ln1_g': row2(ln1_g[l]), 'ln1_b': row2(ln1_b[l]),
            'w_router': jnp.pad(w_router[l], ((0, 0), (0, LANE - N_EXPERTS))).astype(BF),
            'b_router': jnp.pad(row2(b_router[l]), ((0, 0), (0, LANE - N_EXPERTS)), constant_values=NEG_BIG),
        }

        p_lat = _proj(l, xl, sc1, sh1, w_in, cos_t, sin_t, rope=True, tm=_row_tile(n, 2048))
        p_ctx = _proj(l, xc, csc1, csh1, w_in, cos_c, sin_c, rope=False, tm=m)

        sl = lambda arr, col, w: lax.slice_in_dim(arr, col, col + w, axis=1)
        k_all = jnp.concatenate([sl(p_lat, COL_CK, 512), sl(p_ctx, COL_CK, 512)], axis=0)
        v_all = jnp.concatenate([sl(p_lat, COL_CV, 512), sl(p_ctx, COL_CV, 512)], axis=0)
        br_c = _diff_attn(lam, sl(p_lat, COL_CQ, 512), k_all, v_all, c_subln_g[l], lam_init)
        br_d = _win_attn(d_sink[l], p_lat, p_ctx)

        mod3 = jnp.zeros((8, D_MODEL), F32).at[0].set(g1[0]).at[1].set(sc2[0]).at[2].set(sh2[0])
        x1, h2, tidx, tw = _merge(p_lat, br_c, br_d, xl, mod3, lw, tm=256)
        tidx, tw = tidx[:, :TOP_K], tw[:, :TOP_K]

        if need_ctx:
            cbr_c = _diff_attn(lam, sl(p_ctx, COL_CQ, 512), sl(p_ctx, COL_CK, 512), sl(p_ctx, COL_CV, 512),
                               c_subln_g[l], lam_init)
            cbr_d = _win_attn_ctx(d_sink[l], p_ctx)
            cmod3 = jnp.zeros((8, D_MODEL), F32).at[0].set(cg1[0]).at[1].set(csc2[0]).at[2].set(csh2[0])
            xc1, hc2, ctidx, ctw = _merge(p_ctx, cbr_c, cbr_d, xc, cmod3, lw, tm=_row_tile(m, 256))
            h2 = jnp.concatenate([h2, hc2], axis=0)
            tidx = jnp.concatenate([tidx, ctidx[:, :TOP_K]], axis=0)
            tw = jnp.concatenate([tw, ctw[:, :TOP_K]], axis=0)

        n_all = h2.shape[0] // ROW_SLAB
        row_tok, row_w, row_pair, blk_expert, blk_cnt = _route(tidx, tw)
        rows_out = _moe_rows(l, n_all, blk_expert, blk_cnt, row_tok, row_pair, row_w.reshape(-1, 1), h2,
                             e_w_gu, e_b_gu.reshape(DEPTH, N_EXPERTS, 1, -1),
                             e_w_down, e_b_down.reshape(DEPTH, N_EXPERTS, 1, -1))

        xl = _combine(x1, rows_out, g2, row2(ln2_g[l]), row2(ln2_b[l]), tm=256)
        if need_ctx:
            xc = _combine(xc1, rows_out, cg2, row2(ln2_g[l]), row2(ln2_b[l]), tm=_row_tile(m, 256), row_off=n)

    return xl[None]
```

```python
import functools
import math

import jax
import jax.numpy as jnp
from jax import lax
from jax.experimental import pallas as pl
from jax.experimental.pallas import tpu as pltpu

BF = jnp.bfloat16
F32 = jnp.float32

D_MODEL = 2048
DEPTH = 2
GRID_W = 64
BLOCK = 128
A_GROUPS = 4
C_HEADS = 4
C_HD = 64
C_VD = 128
D_HEADS = 8
D_KV_HEADS = 2
D_HD = 64
WINDOW = 128
N_BRANCH = 4
BRANCH_W = 512
PROJ_SIZES = (512, 512, 512, 512, 512, 512, 512, 512, 512, 128, 128, N_BRANCH * D_MODEL)
N_EXPERTS = 32
TOP_K = 4
D_FF = 1024
SWIGLU_LIMIT = 7.0
SWIGLU_ALPHA = 1.702
ROPE_DIM = 64
ROPE_BASE = 10000.0
LN_EPS = 1e-5
DN_ALPHA = (2 * DEPTH) ** 0.25

LANE = 128
NEG_BIG = -1e30

GL_W = N_BRANCH * D_MODEL
PROJ_TN = 256
SEG_W = 9 * 512 + 256
P_W = GL_W + SEG_W
N_SEG_TILES = SEG_W // PROJ_TN
COL_AU, COL_AV, COL_BX, COL_BB, COL_BC, COL_CQ, COL_CK, COL_CV, COL_DQ = [GL_W + 512 * s for s in range(9)]
COL_DK = GL_W + 9 * 512
COL_DV = COL_DK + 128
ROPE_FULL_TILES = (10, 11, 12, 13, 16, 17)
ROPE_PART_TILE = 18

MOE_BLK = 256
ROW_SLAB = D_MODEL // LANE
BUF_PITCH = 24


def _cparams(dims, vmem_mib):
    return pltpu.CompilerParams(dimension_semantics=dims, vmem_limit_bytes=vmem_mib * 1024 * 1024)


def _const_spec(shape):
    nd = len(shape)
    return pl.BlockSpec(shape, lambda *_: (0,) * nd, pipeline_mode=pl.Buffered(1))


def _layer_norm_rows(r, g, b):
    mu = jnp.mean(r, axis=-1, keepdims=True)
    d = r - mu
    var = jnp.mean(d * d, axis=-1, keepdims=True)
    return d * lax.rsqrt(var + LN_EPS) * g + b


def _ada_kernel(c_ref, w_ref, b_ref, o_ref):
    cs = c_ref[...]
    s = cs * jax.nn.sigmoid(cs)
    o_ref[0] = jnp.dot(s.astype(BF), w_ref[0].astype(BF), preferred_element_type=F32) + b_ref[0]


def _ada(c_rows, w_ada, b_ada):
    n_l, _, n_out = w_ada.shape
    tn = 1536
    return pl.pallas_call(
        _ada_kernel,
        grid=(n_l, n_out // tn),
        in_specs=[pl.BlockSpec((8, D_MODEL), lambda l, j: (0, 0)),
                  pl.BlockSpec((1, D_MODEL, tn), lambda l, j: (l, 0, j)),
                  pl.BlockSpec((1, 1, tn), lambda l, j: (l, 0, j))],
        out_specs=pl.BlockSpec((1, 8, tn), lambda l, j: (l, 0, j)),
        out_shape=jax.ShapeDtypeStruct((n_l, 8, n_out), F32),
        compiler_params=_cparams(("arbitrary", "arbitrary"), 40),
        name="ada",
    )(c_rows, w_ada, b_ada.reshape(n_l, 1, n_out))


def _rope_rotate(a, cos_ref, sin_ref):
    w = a.shape[1]
    lane = lax.broadcasted_iota(jnp.int32, a.shape, 1)
    first = jnp.bitwise_and(lane, 16) == 0
    swapped = jnp.where(first, pltpu.roll(a, w - 16, 1), pltpu.roll(a, 16, 1))
    reps = w // LANE
    cos = cos_ref[...]
    sin = sin_ref[...]
    if reps > 1:
        cos = jnp.concatenate([cos] * reps, axis=1)
        sin = jnp.concatenate([sin] * reps, axis=1)
    return a * cos + swapped * sin


def _proj_kernel(x_ref, sc_ref, sh_ref, w_ref, cos_ref, sin_ref, o_ref, h_scr, *, rope):
    j = pl.program_id(1)

    @pl.when(j == 0)
    def _():
        h_scr[...] = (x_ref[...] * (1.0 + sc_ref[...]) + sh_ref[...]).astype(BF)

    acc = jnp.dot(h_scr[...], w_ref[0].astype(BF), preferred_element_type=F32)
    if not rope:
        o_ref[...] = acc.astype(BF)
        return

    full = functools.reduce(jnp.logical_or, [j == t for t in ROPE_FULL_TILES])
    part = j == ROPE_PART_TILE

    @pl.when(full)
    def _():
        o_ref[...] = _rope_rotate(acc, cos_ref, sin_ref).astype(BF)

    @pl.when(part)
    def _():
        o_ref[:, :LANE] = _rope_rotate(acc[:, :LANE], cos_ref, sin_ref).astype(BF)
        o_ref[:, LANE:] = acc[:, LANE:].astype(BF)

    @pl.when(jnp.logical_not(jnp.logical_or(full, part)))
    def _():
        o_ref[...] = acc.astype(BF)


def _proj(layer, x2d, sc, sh, w_in, cos_t, sin_t, *, rope, tm):
    n = x2d.shape[0]
    kern = functools.partial(_proj_kernel, rope=rope)
    n_gl_tiles = GL_W // PROJ_TN
    out_tile = lambda j: jnp.where(j < N_SEG_TILES, j + n_gl_tiles, j - N_SEG_TILES)
    return pl.pallas_call(
        kern,
        grid=(n // tm, P_W // PROJ_TN),
        in_specs=[pl.BlockSpec((tm, D_MODEL), lambda i, j: (i, 0), pipeline_mode=pl.Buffered(1)),
                  pl.BlockSpec((1, D_MODEL), lambda i, j: (0, 0)),
                  pl.BlockSpec((1, D_MODEL), lambda i, j: (0, 0)),
                  pl.BlockSpec((1, D_MODEL, PROJ_TN), lambda i, j: (layer, 0, j)),
                  pl.BlockSpec((tm, LANE), lambda i, j: (i, 0)),
                  pl.BlockSpec((tm, LANE), lambda i, j: (i, 0))],
        out_specs=pl.BlockSpec((tm, PROJ_TN), lambda i, j: (i, out_tile(j))),
        out_shape=jax.ShapeDtypeStruct((n, P_W), BF),
        scratch_shapes=[pltpu.VMEM((tm, D_MODEL), BF)],
        compiler_params=_cparams(("arbitrary", "arbitrary"), 56),
        name="proj_rope" if rope else "proj_ctx",
    )(x2d, sc, sh, w_in, cos_t, sin_t)


_ACC_ROWS = 32
_ONES_ROWS = 16
LOG2E = math.log2(math.e)


def _diff_attn_kernel(lam_ref, qt_ref, k_ref, vt_ref, g_ref, o_ref, qbd_scr, m_scr, acc_scr, s_scr, mx_scr, *,
                      tq, tk, sub, n_chunks, post_scale):
    w = 2 * tq
    qt = qt_ref[...].astype(F32) * (C_HD ** -0.5 * LOG2E)
    row = lax.broadcasted_iota(jnp.int32, qt.shape, 0)
    qbd_scr[...] = jnp.concatenate([jnp.where(row < C_HD, qt, 0.0), jnp.where(row >= C_HD, qt, 0.0)],
                                   axis=1).astype(BF)
    m_scr[...] = jnp.full(m_scr.shape, -jnp.inf, F32)
    acc_scr[...] = jnp.zeros(acc_scr.shape, F32)

    n_sub = tk // sub

    def scores_sub(c, slot, j):
        k = k_ref[pl.ds(pl.multiple_of(c * tk + j * sub, sub), sub), :]
        s = jnp.dot(k, qbd_scr[...], preferred_element_type=F32)
        s_scr[slot, j * sub:(j + 1) * sub, :] = s
        return jnp.max(s.reshape(sub // _ACC_ROWS, _ACC_ROWS, w), axis=0)

    def step(c, slot, nxt):
        m_old = m_scr[...]
        m_new = jnp.maximum(m_old, jnp.max(mx_scr[slot], axis=0, keepdims=True))
        alpha = jnp.exp2(m_old - m_new)
        pv = None
        mx = None
        for j in range(n_sub):
            if nxt is not None:
                mj = scores_sub(nxt, 1 - slot, j)
                mx = mj if mx is None else jnp.maximum(mx, mj)
            p = jnp.exp2(s_scr[slot, j * sub:(j + 1) * sub, :] - m_new).astype(BF)
            d = jnp.dot(vt_ref[0, c, :, j * sub:(j + 1) * sub], p, preferred_element_type=F32)
            pv = d if pv is None else pv + d
        acc_scr[...] = acc_scr[...] * alpha + pv
        m_scr[...] = m_new
        if nxt is not None:
            mx_scr[1 - slot] = mx

    mx0 = None
    for j in range(n_sub):
        mj = scores_sub(0, 0, j)
        mx0 = mj if mx0 is None else jnp.maximum(mx0, mj)
    mx_scr[0] = mx0
    n_pairs = (n_chunks - 1) // 2

    def pair(i, carry):
        c = 2 * i
        step(c, 0, c + 1)
        step(c + 1, 1, c + 2)
        return carry

    if n_pairs > 0:
        lax.fori_loop(0, n_pairs, pair, 0)
    for c in range(2 * n_pairs, n_chunks):
        step(c, c % 2, c + 1 if c + 1 < n_chunks else None)

    o = acc_scr[:C_VD, :] / acc_scr[C_VD:C_VD + 1, :]
    od = o[:, :tq] - lam_ref[0] * o[:, tq:]
    ms = jnp.mean(od * od, axis=0, keepdims=True)
    on = od * lax.rsqrt(ms + LN_EPS) * g_ref[...] * post_scale
    o_ref[...] = on.T.astype(BF)


def _pick_tk(n_k):
    for tk in (1280, 1024, 768, 512, 256, 128):
        if n_k % tk == 0:
            return tk
    raise ValueError(f"unsupported key count {n_k}")


def _diff_attn(lam, q, k_all, v_all, subln_g, lam_init):
    n = q.shape[0]
    n_k = k_all.shape[0]
    tq = 256
    tk = _pick_tk(n_k)
    n_chunks = n_k // tk
    qt = q.T
    vt = v_all.reshape(n_chunks, tk, C_HEADS, C_VD).transpose(2, 0, 3, 1)
    vt = jnp.concatenate([vt, jnp.ones((C_HEADS, n_chunks, _ONES_ROWS, tk), BF)], axis=2)
    sub = 256 if tk % 256 == 0 else tk
    kern = functools.partial(_diff_attn_kernel, tq=tq, tk=tk, sub=sub, n_chunks=n_chunks,
                             post_scale=1.0 - lam_init)
    return pl.pallas_call(
        kern,
        grid=(C_HEADS, n // tq),
        in_specs=[pl.BlockSpec(memory_space=pltpu.SMEM),
                  pl.BlockSpec((C_VD, tq), lambda h, i: (h, i)),
                  pl.BlockSpec((n_k, C_VD), lambda h, i: (0, h)),
                  pl.BlockSpec((1, n_chunks, C_VD + _ONES_ROWS, tk), lambda h, i: (h, 0, 0, 0)),
                  pl.BlockSpec((C_VD, 1), lambda h, i: (0, 0))],
        out_specs=pl.BlockSpec((tq, C_VD), lambda h, i: (i, h)),
        out_shape=jax.ShapeDtypeStruct((n, C_HEADS * C_VD), BF),
        scratch_shapes=[pltpu.VMEM((C_VD, 2 * tq), BF), pltpu.VMEM((1, 2 * tq), F32),
                        pltpu.VMEM((C_VD + _ONES_ROWS, 2 * tq), F32),
                        pltpu.VMEM((2, tk, 2 * tq), F32), pltpu.VMEM((2, _ACC_ROWS, 2 * tq), F32)],
        compiler_params=_cparams(("arbitrary", "arbitrary"), 48),
        name="diff_attn",
    )(lam, qt, k_all, vt, subln_g.reshape(C_VD, 1))


def _win_heads(q, kb, vb, valid, sink_ref):
    outs = []
    grp = D_HEADS // D_KV_HEADS
    for h in range(D_HEADS):
        kh = h // grp
        qh = q[:, h * D_HD:(h + 1) * D_HD]
        k_h = kb[:, kh * D_HD:(kh + 1) * D_HD]
        v_h = vb[:, kh * D_HD:(kh + 1) * D_HD]
        s = lax.dot_general(qh, k_h, (((1,), (1,)), ((), ())), preferred_element_type=F32) * (D_HD ** -0.5)
        if valid is not None:
            s = jnp.where(valid, s, NEG_BIG)
        sk = sink_ref[h]
        m = jnp.maximum(jnp.max(s, axis=-1, keepdims=True), sk)
        e = jnp.exp(s - m)
        l = jnp.sum(e, axis=-1, keepdims=True) + jnp.exp(sk - m)
        p = (e / l).astype(BF)
        outs.append(jnp.dot(p, v_h, preferred_element_type=F32))
    return jnp.concatenate(outs, axis=1).astype(BF)


def _win_attn_ctx_kernel(sink_ref, q_ref, kx_ref, vx_ref, o_ref):
    o_ref[...] = _win_heads(q_ref[...], kx_ref[...], vx_ref[...], None, sink_ref)


def _win_attn_t_kernel(sink_ref, qt_ref, kp_ref, kc_ref, kn_ref, kx_ref, vtp_ref, vtc_ref, vtn_ref, vtx_ref, o_ref, *,
                       n_tok, n_ctx):
    i = pl.program_id(0)
    grp = D_HEADS // D_KV_HEADS
    kfull = jnp.concatenate([kp_ref[...], kc_ref[...], kn_ref[...], kx_ref[...]], axis=0)
    vtfull = jnp.concatenate([vtp_ref[...], vtc_ref[...], vtn_ref[...], vtx_ref[...]], axis=1)
    n_keys = 3 * BLOCK + n_ctx
    kk = lax.broadcasted_iota(jnp.int32, (n_keys, BLOCK), 0)
    qq = lax.broadcasted_iota(jnp.int32, (n_keys, BLOCK), 1)
    kpos = (i - 1) * BLOCK + kk
    in_band = (jnp.abs(kk - BLOCK - qq) <= WINDOW) & (kpos >= 0) & (kpos < n_tok)
    bias = jnp.where((kk >= 3 * BLOCK) | in_band, 0.0, NEG_BIG)
    bias = jnp.concatenate([bias] * grp, axis=1)
    qt = (qt_ref[...].astype(F32) * (D_HD ** -0.5)).astype(BF)
    zeros = jnp.zeros((D_HD, grp * BLOCK), BF)
    for g in range(D_KV_HEADS):
        q64 = jnp.concatenate([qt[(g * grp + hh) * D_HD:(g * grp + hh + 1) * D_HD, :] for hh in range(grp)], axis=1)
        qbd = jnp.concatenate([q64, zeros] if g == 0 else [zeros, q64], axis=0)
        s = jnp.dot(kfull, qbd, preferred_element_type=F32) + bias
        sk = sink_ref[g]
        m = jnp.maximum(jnp.max(s, axis=0, keepdims=True), sk)
        e = jnp.exp(s - m)
        l = jnp.sum(e, axis=0, keepdims=True) + jnp.exp(sk - m)
        p = (e * (1.0 / l)).astype(BF)
        ot = jnp.dot(vtfull, p, preferred_element_type=F32)
        og = ot[g * D_HD:(g + 1) * D_HD, :]
        for hh in range(grp):
            h = g * grp + hh
            o_ref[h * D_HD:(h + 1) * D_HD, :] = og[:, hh * BLOCK:(hh + 1) * BLOCK].astype(BF)


def _win_attn(sink, p_lat, p_ctx):
    n = p_lat.shape[0]
    m = p_ctx.shape[0]
    nb = n // BLOCK
    grp = D_HEADS // D_KV_HEADS
    ck = COL_DK // LANE
    qt = lax.slice_in_dim(p_lat, COL_DQ, COL_DQ + 512, axis=1).T
    vt = lax.slice_in_dim(p_lat, COL_DV, COL_DV + LANE, axis=1).T
    vtx = lax.slice_in_dim(p_ctx, COL_DV, COL_DV + LANE, axis=1).T
    sink_rows = jnp.repeat(sink.astype(F32).reshape(D_KV_HEADS, grp), BLOCK, axis=1).reshape(D_KV_HEADS, 1, grp * BLOCK)
    kern = functools.partial(_win_attn_t_kernel, n_tok=n, n_ctx=m)
    prev = lambda i: jnp.maximum(i - 1, 0)
    nxt = lambda i: jnp.minimum(i + 1, nb - 1)
    o_t = pl.pallas_call(
        kern,
        grid=(nb,),
        in_specs=[pl.BlockSpec((D_KV_HEADS, 1, grp * BLOCK), lambda i: (0, 0, 0)),
                  pl.BlockSpec((512, BLOCK), lambda i: (0, i)),
                  pl.BlockSpec((BLOCK, LANE), lambda i: (prev(i), ck)),
                  pl.BlockSpec((BLOCK, LANE), lambda i: (i, ck)),
                  pl.BlockSpec((BLOCK, LANE), lambda i: (nxt(i), ck)),
                  pl.BlockSpec((m, LANE), lambda i: (0, ck)),
                  pl.BlockSpec((LANE, BLOCK), lambda i: (0, prev(i))),
                  pl.BlockSpec((LANE, BLOCK), lambda i: (0, i)),
                  pl.BlockSpec((LANE, BLOCK), lambda i: (0, nxt(i))),
                  pl.BlockSpec((LANE, m), lambda i: (0, 0))],
        out_specs=pl.BlockSpec((512, BLOCK), lambda i: (0, i)),
        out_shape=jax.ShapeDtypeStruct((512, n), BF),
        compiler_params=_cparams(("arbitrary",), 32),
        name="win_attn",
    )(sink_rows, qt, p_lat, p_lat, p_lat, p_ctx, vt, vt, vt, vtx)
    return o_t.T


def _win_attn_ctx(sink, p_ctx):
    m = p_ctx.shape[0]
    ck, cv, cq = COL_DK // LANE, COL_DV // LANE, COL_DQ // 512
    return pl.pallas_call(
        _win_attn_ctx_kernel,
        grid=(m // BLOCK,),
        in_specs=[pl.BlockSpec(memory_space=pltpu.SMEM),
                  pl.BlockSpec((BLOCK, 512), lambda i: (i, cq)),
                  pl.BlockSpec((m, LANE), lambda i: (0, ck)),
                  pl.BlockSpec((m, LANE), lambda i: (0, cv))],
        out_specs=pl.BlockSpec((BLOCK, 512), lambda i: (i, 0)),
        out_shape=jax.ShapeDtypeStruct((m, 512), BF),
        compiler_params=_cparams(("arbitrary",), 32),
        name="win_attn_ctx",
    )(sink, p_ctx, p_ctx, p_ctx)


def _merge_kernel(gl_ref, au_ref, av_ref, bx_ref, bb_ref, bc_ref, bxp_ref, bcp_ref, bxn_ref, bcn_ref,
                  brc_ref, brd_ref, x_ref, mod_ref, bgate_ref, alng_ref, alnb_ref, ws_ref, bs_ref, cw_ref, cb_ref,
                  wbr_ref, wout_ref, ln1g_ref, ln1b_ref, wr_ref, brt_ref,
                  x1_ref, h2_ref, tidx_ref, tw_ref, *, tm):
    i = pl.program_id(0)
    last = pl.num_programs(0) - 1

    u = jax.nn.gelu(au_ref[...].astype(F32), approximate=True)
    v = jax.nn.gelu(av_ref[...].astype(F32), approximate=True)
    vn = _layer_norm_rows(v, alng_ref[...], alnb_ref[...]).astype(BF)
    blocks = []
    for b in range(tm // BLOCK):
        cols = []
        for g in range(A_GROUPS):
            vbg = vn[b * BLOCK:(b + 1) * BLOCK, g * LANE:(g + 1) * LANE]
            cols.append(jnp.dot(ws_ref[g], vbg, preferred_element_type=F32))
        blocks.append(jnp.concatenate(cols, axis=1) + bs_ref[...])
    mixed = jnp.concatenate(blocks, axis=0) if len(blocks) > 1 else blocks[0]
    br_a = u * mixed

    z = bc_ref[...].astype(F32) * bx_ref[...].astype(F32)
    z_prev = bcp_ref[7:8, :].astype(F32) * bxp_ref[7:8, :].astype(F32) * (i > 0).astype(F32)
    z_next = bcn_ref[0:1, :].astype(F32) * bxn_ref[0:1, :].astype(F32) * (i < last).astype(F32)
    row = lax.broadcasted_iota(jnp.int32, z.shape, 0)
    z_up = jnp.where(row == 0, z_prev, pltpu.roll(z, 1, 0))
    z_dn = jnp.where(row == tm - 1, z_next, pltpu.roll(z, tm - 1, 0))
    y_conv = cw_ref[0:1, :] * z_up + cw_ref[1:2, :] * z + cw_ref[2:3, :] * z_dn + cb_ref[...]
    br_b = bb_ref[...].astype(F32) * y_conv

    branches = (br_a.astype(BF), br_b.astype(BF), brc_ref[...], brd_ref[...])
    merged = None
    for g in range(N_BRANCH):
        pr = jnp.dot(branches[g], wbr_ref[g], preferred_element_type=F32)
        gate = jax.nn.sigmoid(gl_ref[:, g * D_MODEL:(g + 1) * D_MODEL].astype(F32) + bgate_ref[g:g + 1, :])
        merged = gate * pr if merged is None else merged + gate * pr
    y = jnp.dot(merged.astype(BF), wout_ref[...], preferred_element_type=F32)

    r = DN_ALPHA * x_ref[...] + mod_ref[0:1, :] * y
    x1 = _layer_norm_rows(r, ln1g_ref[...], ln1b_ref[...])
    x1_ref[...] = x1
    h2f = x1 * (1.0 + mod_ref[1:2, :]) + mod_ref[2:3, :]
    for j in range(ROW_SLAB):
        h2_ref[pl.ds(j, tm, stride=ROW_SLAB), :] = h2f[:, j * LANE:(j + 1) * LANE]
    h2 = h2f.astype(BF)
    logits = jnp.dot(h2, wr_ref[...], preferred_element_type=F32) + brt_ref[...]

    lane = lax.broadcasted_iota(jnp.int32, logits.shape, 1)
    vals, idxs = [], []
    cur = logits
    for _ in range(TOP_K):
        mx = jnp.max(cur, axis=-1, keepdims=True)
        ix = jnp.min(jnp.where(cur == mx, lane, LANE), axis=-1, keepdims=True)
        vals.append(mx)
        idxs.append(ix)
        cur = jnp.where(lane == ix, -jnp.inf, cur)
    es = [jnp.exp(vk - vals[0]) for vk in vals]
    den = es[0] + es[1] + es[2] + es[3]
    tidx = jnp.zeros(logits.shape, jnp.int32)
    tw = jnp.zeros(logits.shape, F32)
    for k in range(TOP_K):
        tidx = jnp.where(lane == k, idxs[k], tidx)
        tw = jnp.where(lane == k, es[k] / den, tw)
    tidx_ref[...] = tidx
    tw_ref[...] = tw


def _merge(p, br_c, br_d, x2d, mod3, lw, *, tm):
    n = p.shape[0]
    nt = n // tm
    r8 = tm // 8
    c512 = lambda col: col // 512
    kern = functools.partial(_merge_kernel, tm=tm)
    prev8 = lambda i: (jnp.maximum(i * r8 - 1, 0))
    next8 = lambda i: (jnp.minimum((i + 1) * r8, n // 8 - 1))
    seg = lambda col: pl.BlockSpec((tm, 512), lambda i: (i, c512(col)))
    in_specs = [
        pl.BlockSpec((tm, GL_W), lambda i: (i, 0)),
        seg(COL_AU), seg(COL_AV), seg(COL_BX), seg(COL_BB), seg(COL_BC),
        pl.BlockSpec((8, 512), lambda i: (prev8(i), c512(COL_BX))),
        pl.BlockSpec((8, 512), lambda i: (prev8(i), c512(COL_BC))),
        pl.BlockSpec((8, 512), lambda i: (next8(i), c512(COL_BX))),
        pl.BlockSpec((8, 512), lambda i: (next8(i), c512(COL_BC))),
        pl.BlockSpec((tm, 512), lambda i: (i, 0)),
        pl.BlockSpec((tm, 512), lambda i: (i, 0)),
        pl.BlockSpec((tm, D_MODEL), lambda i: (i, 0)),
        _const_spec((8, D_MODEL)),
        _const_spec((N_BRANCH, D_MODEL)),
        _const_spec((1, 512)), _const_spec((1, 512)),
        _const_spec((A_GROUPS, BLOCK, BLOCK)),
        _const_spec((BLOCK, 512)),
        _const_spec((3, 512)), _const_spec((1, 512)),
        _const_spec((N_BRANCH, BRANCH_W, D_MODEL)),
        _const_spec((D_MODEL, D_MODEL)),
        _const_spec((1, D_MODEL)), _const_spec((1, D_MODEL)),
        _const_spec((D_MODEL, LANE)), _const_spec((1, LANE)),
    ]
    out_specs = [pl.BlockSpec((tm, D_MODEL), lambda i: (i, 0)),
                 pl.BlockSpec((tm * ROW_SLAB, LANE), lambda i: (i, 0)),
                 pl.BlockSpec((tm, LANE), lambda i: (i, 0)),
                 pl.BlockSpec((tm, LANE), lambda i: (i, 0))]
    out_shape = [jax.ShapeDtypeStruct((n, D_MODEL), F32), jax.ShapeDtypeStruct((n * ROW_SLAB, LANE), F32),
                 jax.ShapeDtypeStruct((n, LANE), jnp.int32), jax.ShapeDtypeStruct((n, LANE), F32)]
    return pl.pallas_call(
        kern, grid=(nt,), in_specs=in_specs, out_specs=out_specs, out_shape=out_shape,
        compiler_params=_cparams(("arbitrary",), 56),
        name="merge",
    )(p, p, p, p, p, p, p, p, p, p, br_c, br_d, x2d, mod3, lw['b_gate'], lw['a_ln_g'], lw['a_ln_b'],
      lw['a_ws'], lw['a_bs_full'], lw['b_conv_w'], lw['b_conv_b'], lw['w_br'], lw['w_out'],
      lw['ln1_g'], lw['ln1_b'], lw['w_router'], lw['b_router'])


def _moe_kernel(be_ref, bc_ref, tokc_ref, tokn_ref, pair_ref, rw_ref, wgu_ref, bgu_ref, wd_ref, bd_ref, h2_hbm,
                out_hbm, wgu_bf, wd_bf, xbuf, obuf, gsem, ssem):
    i = pl.program_id(0)
    slot = lax.rem(i, 2)
    cnt = bc_ref[i]
    prev_cnt = jnp.where(i > 0, bc_ref[jnp.maximum(i - 1, 0)], 0)
    first = jnp.logical_and(i == 0, cnt > 0)
    changed = jnp.logical_or(i == 0, be_ref[i] != be_ref[jnp.maximum(i - 1, 0)])

    def gather_copy(tok, r, sl):
        return pltpu.make_async_copy(
            h2_hbm.at[pl.ds(pl.multiple_of(tok * ROW_SLAB, ROW_SLAB), ROW_SLAB), :],
            xbuf.at[sl, pl.ds(r * BUF_PITCH, ROW_SLAB), :], gsem.at[sl])

    def scatter_copy(pair, r):
        tok = lax.shift_right_logical(pair, 2)
        return pltpu.make_async_copy(
            obuf.at[pl.ds(pl.multiple_of(r * BUF_PITCH, 8), ROW_SLAB), :],
            out_hbm.at[jnp.bitwise_and(pair, TOP_K - 1), pl.ds(pl.multiple_of(tok * ROW_SLAB, ROW_SLAB), ROW_SLAB), :],
            ssem.at[0])

    def wait_prev_scatter():
        @pl.when(prev_cnt == MOE_BLK)
        def _():
            for r in range(MOE_BLK):
                scatter_copy(0, r).wait()

        @pl.when(jnp.logical_and(prev_cnt > 0, prev_cnt < MOE_BLK))
        def _():
            def one(r, carry):
                scatter_copy(0, r).wait()
                return carry
            lax.fori_loop(0, prev_cnt, one, 0)

    def block(full):
        for r in range(MOE_BLK):
            gather_copy(tokn_ref[0, 0, r], r, 1 - slot).start(priority=r % 2)
        x = jnp.concatenate([xbuf[slot, pl.ds(j, MOE_BLK, stride=BUF_PITCH), :] for j in range(ROW_SLAB)],
                            axis=1).astype(BF)
        gu = jnp.dot(x, wgu_bf[...], preferred_element_type=F32) + bgu_ref[0, 0]
        g = jnp.minimum(gu[:, :D_FF], SWIGLU_LIMIT)
        u = jnp.clip(gu[:, D_FF:], -SWIGLU_LIMIT, SWIGLU_LIMIT)
        act = (u + 1.0) * (g * jax.nn.sigmoid(SWIGLU_ALPHA * g))
        out = (jnp.dot(act.astype(BF), wd_bf[...], preferred_element_type=F32) + bd_ref[0, 0]) * rw_ref[...]
        wait_prev_scatter()
        for j in range(ROW_SLAB):
            obuf[pl.ds(j, MOE_BLK, stride=BUF_PITCH), :] = out[:, j * LANE:(j + 1) * LANE]
        if full:
            for r in range(MOE_BLK):
                scatter_copy(pair_ref[0, 0, r], r).start(priority=r % 2)
        else:
            def one(r, carry):
                scatter_copy(pair_ref[0, 0, r], r).start()
                return carry
            lax.fori_loop(0, cnt, one, 0)

    @pl.when(first)
    def _():
        for r in range(MOE_BLK):
            gather_copy(tokc_ref[0, 0, r], r, 0).start()

    @pl.when(jnp.logical_or(first, prev_cnt > 0))
    def _():
        for r in range(MOE_BLK):
            gather_copy(0, r, slot).wait()

    @pl.when(changed)
    def _():
        wgu_bf[...] = wgu_ref[0, 0].astype(BF)
        wd_bf[...] = wd_ref[0, 0].astype(BF)

    @pl.when(cnt == MOE_BLK)
    def _():
        block(True)

    @pl.when(jnp.logical_and(cnt > 0, cnt < MOE_BLK))
    def _():
        block(False)

    @pl.when(cnt == 0)
    def _():
        wait_prev_scatter()


def _moe_rows(layer, n_all, blk_expert, blk_cnt, row_tok, row_pair, row_w, h2_rows, e_w_gu, e_b_gu, e_w_down,
              e_b_down):
    n_blk = blk_expert.shape[0]
    once = pl.Buffered(1)
    idx_blk = lambda f: pl.BlockSpec((1, 1, MOE_BLK), f, memory_space=pltpu.SMEM)
    grid_spec = pltpu.PrefetchScalarGridSpec(
        num_scalar_prefetch=2,
        grid=(n_blk,),
        in_specs=[idx_blk(lambda i, be, bc: (i, 0, 0)),
                  idx_blk(lambda i, be, bc: (jnp.minimum(i + 1, n_blk - 1), 0, 0)),
                  idx_blk(lambda i, be, bc: (i, 0, 0)),
                  pl.BlockSpec((MOE_BLK, 1), lambda i, be, bc: (i, 0)),
                  pl.BlockSpec((1, 1, D_MODEL, 2 * D_FF), lambda i, be, bc: (layer, be[i], 0, 0), pipeline_mode=once),
                  pl.BlockSpec((1, 1, 1, 2 * D_FF), lambda i, be, bc: (layer, be[i], 0, 0)),
                  pl.BlockSpec((1, 1, D_FF, D_MODEL), lambda i, be, bc: (layer, be[i], 0, 0), pipeline_mode=once),
                  pl.BlockSpec((1, 1, 1, D_MODEL), lambda i, be, bc: (layer, be[i], 0, 0)),
                  pl.BlockSpec(memory_space=pl.ANY)],
        out_specs=pl.BlockSpec(memory_space=pl.ANY),
        scratch_shapes=[pltpu.VMEM((D_MODEL, 2 * D_FF), BF), pltpu.VMEM((D_FF, D_MODEL), BF),
                        pltpu.VMEM((2, MOE_BLK * BUF_PITCH, LANE), F32), pltpu.VMEM((MOE_BLK * BUF_PITCH, LANE), F32),
                        pltpu.SemaphoreType.DMA((2,)), pltpu.SemaphoreType.DMA((1,))],
    )
    tok3 = row_tok.reshape(n_blk, 1, MOE_BLK)
    return pl.pallas_call(
        _moe_kernel, grid_spec=grid_spec,
        out_shape=jax.ShapeDtypeStruct((TOP_K, n_all * ROW_SLAB, LANE), F32),
        compiler_params=_cparams(("arbitrary",), 58),
        name="moe_experts",
    )(blk_expert, blk_cnt, tok3, tok3, row_pair.reshape(n_blk, 1, MOE_BLK), row_w, e_w_gu, e_b_gu, e_w_down,
      e_b_down, h2_rows)


def _route(top_idx, top_w):
    n_tok = top_idx.shape[0]
    n_pairs = n_tok * TOP_K
    e_flat = top_idx.reshape(-1)
    onehot = e_flat[:, None] == jnp.arange(N_EXPERTS, dtype=jnp.int32)[None, :]
    counts = jnp.sum(onehot, axis=0, dtype=jnp.int32)
    padded = (counts + MOE_BLK - 1) // MOE_BLK * MOE_BLK
    starts = jnp.cumsum(counts) - counts
    pends = jnp.cumsum(padded)
    pstarts = pends - padded
    n_rows = (n_pairs + MOE_BLK - 1) // MOE_BLK * MOE_BLK + N_EXPERTS * MOE_BLK
    n_blk = n_rows // MOE_BLK
    blk_start = jnp.arange(n_blk, dtype=jnp.int32) * MOE_BLK
    blk_expert = jnp.minimum(jnp.sum(blk_start[:, None] >= pends[None, :], axis=1, dtype=jnp.int32), N_EXPERTS - 1)
    blk_valid = (blk_start < pends[-1]).astype(jnp.int32)
    order = jnp.argsort(e_flat).astype(jnp.int32)
    j_in_blk = jnp.arange(MOE_BLK, dtype=jnp.int32)[None, :]
    j_in_e = (blk_start - pstarts[blk_expert])[:, None] + j_in_blk
    row_valid = (j_in_e < counts[blk_expert][:, None]) & (blk_valid[:, None] > 0)
    src = jnp.clip(starts[blk_expert][:, None] + j_in_e, 0, n_pairs - 1).reshape(-1)
    row_pair = order[src]
    row_valid = row_valid.reshape(-1)
    row_tok = jnp.where(row_valid, row_pair // TOP_K, 0)
    row_w = jnp.where(row_valid, top_w.reshape(-1)[row_pair], 0.0)
    row_pair = jnp.where(row_valid, row_pair, 0)
    blk_cnt = jnp.clip(counts[blk_expert] - (blk_start - pstarts[blk_expert]), 0, MOE_BLK) * blk_valid
    return row_tok, row_w, row_pair, blk_expert, blk_cnt


def _combine_kernel(x1_ref, r_ref, g2_ref, lng_ref, lnb_ref, o_ref, *, tm):
    f = None
    for k in range(TOP_K):
        fk = jnp.concatenate([r_ref[k, pl.ds(j, tm, stride=ROW_SLAB), :] for j in range(ROW_SLAB)], axis=1)
        f = fk if f is None else f + fk
    r = DN_ALPHA * x1_ref[...] + g2_ref[...] * f
    o_ref[...] = _layer_norm_rows(r, lng_ref[...], lnb_ref[...])


def _combine(x1, rows, g2, ln_g, ln_b, *, tm, row_off=0):
    n = x1.shape[0]
    assert row_off % tm == 0
    blk_off = row_off // tm
    return pl.pallas_call(
        functools.partial(_combine_kernel, tm=tm), grid=(n // tm,),
        in_specs=[pl.BlockSpec((tm, D_MODEL), lambda i: (i, 0)),
                  pl.BlockSpec((TOP_K, tm * ROW_SLAB, LANE), lambda i: (0, i + blk_off, 0)),
                  pl.BlockSpec((1, D_MODEL), lambda i: (0, 0)),
                  pl.BlockSpec((1, D_MODEL), lambda i: (0, 0)),
                  pl.BlockSpec((1, D_MODEL), lambda i: (0, 0))],
        out_specs=pl.BlockSpec((tm, D_MODEL), lambda i: (i, 0)),
        out_shape=jax.ShapeDtypeStruct((n, D_MODEL), F32),
        compiler_params=_cparams(("arbitrary",), 48),
        name="combine_ln2",
    )(x1, rows, g2, ln_g, ln_b)


def _rope_tables(n_tok):
    n_freq = ROPE_DIM // 4
    inv = ROPE_BASE ** (-jnp.arange(n_freq, dtype=F32) / n_freq)
    t = jnp.arange(n_tok)
    ang_r = (t // GRID_W).astype(F32)[:, None] * inv
    ang_c = (t % GRID_W).astype(F32)[:, None] * inv
    cos64 = jnp.concatenate([jnp.cos(ang_r), jnp.cos(ang_r), jnp.cos(ang_c), jnp.cos(ang_c)], axis=1)
    sin64 = jnp.concatenate([-jnp.sin(ang_r), jnp.sin(ang_r), -jnp.sin(ang_c), jnp.sin(ang_c)], axis=1)
    return jnp.concatenate([cos64, cos64], axis=1), jnp.concatenate([sin64, sin64], axis=1)


def _row_tile(n, pref):
    return pref if n % pref == 0 else n


def kernel(x, c, ctx, c_ctx, w_ada, b_ada, w_in, b_gate, a_ln_g, a_ln_b, a_ws, a_bs, b_conv_w, b_conv_b, c_lq1, c_lk1, c_lq2, c_lk2, c_subln_g, d_sink, w_br, w_out, ln1_g, ln1_b, w_router, b_router, e_w_gu, e_b_gu, e_w_down, e_b_down, ln2_g, ln2_b):
    assert x.shape[0] == 1 and ctx.shape[0] == 1
    n, m = x.shape[1], ctx.shape[1]
    assert n % 256 == 0 and m % BLOCK == 0
    xl = x[0]
    xc = ctx[0]

    c_rows = jnp.zeros((8, D_MODEL), F32).at[0].set(c[0]).at[1].set(c_ctx)
    mods = _ada(c_rows, w_ada, b_ada)
    cos_t, sin_t = _rope_tables(n)
    cos_c = jnp.ones((m, LANE), F32)
    sin_c = jnp.zeros((m, LANE), F32)
    row2 = lambda v: v.reshape(1, -1)

    for l in range(DEPTH):
        need_ctx = l < DEPTH - 1
        lam_init = 0.8 - 0.6 * math.exp(-0.3 * l)
        sh1, sc1, g1, sh2, sc2, g2 = [row2(t) for t in jnp.split(mods[l, 0], 6)]
        csh1, csc1, cg1, csh2, csc2, cg2 = [row2(t) for t in jnp.split(mods[l, 1], 6)]
        lam = (jnp.exp(jnp.sum(c_lq1[l] * c_lk1[l])) - jnp.exp(jnp.sum(c_lq2[l] * c_lk2[l]))
               + lam_init).astype(F32).reshape(1)
        lw = {
            'b_gate': b_gate[l], 'a_ln_g': row2(a_ln_g[l]), 'a_ln_b': row2(a_ln_b[l]),
            'a_ws': a_ws[l].astype(BF),
            'a_bs_full': jnp.repeat(a_bs[l].T, BLOCK, axis=1),
            'b_conv_w': b_conv_w[l], 'b_conv_b': row2(b_conv_b[l]),
            'w_br': w_br[l].astype(BF), 'w_out': w_out[l].astype(BF),
            'ln1_g': row2(ln1_g[l]), 'ln1_b': row2(ln1_b[l]),
            'w_router': jnp.pad(w_router[l], ((0, 0), (0, LANE - N_EXPERTS))).astype(BF),
            'b_router': jnp.pad(row2(b_router[l]), ((0, 0), (0, LANE - N_EXPERTS)), constant_values=NEG_BIG),
        }

        p_lat = _proj(l, xl, sc1, sh1, w_in, cos_t, sin_t, rope=True, tm=_row_tile(n, 2048))
        p_ctx = _proj(l, xc, csc1, csh1, w_in, cos_c, sin_c, rope=False, tm=m)

        sl = lambda arr, col, w: lax.slice_in_dim(arr, col, col + w, axis=1)
        k_all = jnp.concatenate([sl(p_lat, COL_CK, 512), sl(p_ctx, COL_CK, 512)], axis=0)
        v_all = jnp.concatenate([sl(p_lat, COL_CV, 512), sl(p_ctx, COL_CV, 512)], axis=0)
        br_c = _diff_attn(lam, sl(p_lat, COL_CQ, 512), k_all, v_all, c_subln_g[l], lam_init)
        br_d = _win_attn(d_sink[l], p_lat, p_ctx)

        mod3 = jnp.zeros((8, D_MODEL), F32).at[0].set(g1[0]).at[1].set(sc2[0]).at[2].set(sh2[0])
        x1, h2, tidx, tw = _merge(p_lat, br_c, br_d, xl, mod3, lw, tm=256)
        tidx, tw = tidx[:, :TOP_K], tw[:, :TOP_K]

        if need_ctx:
            cbr_c = _diff_attn(lam, sl(p_ctx, COL_CQ, 512), sl(p_ctx, COL_CK, 512), sl(p_ctx, COL_CV, 512),
                               c_subln_g[l], lam_init)
            cbr_d = _win_attn_ctx(d_sink[l], p_ctx)
            cmod3 = jnp.zeros((8, D_MODEL), F32).at[0].set(cg1[0]).at[1].set(csc2[0]).at[2].set(csh2[0])
            xc1, hc2, ctidx, ctw = _merge(p_ctx, cbr_c, cbr_d, xc, cmod3, lw, tm=_row_tile(m, 256))
            h2 = jnp.concatenate([h2, hc2], axis=0)
            tidx = jnp.concatenate([tidx, ctidx[:, :TOP_K]], axis=0)
            tw = jnp.concatenate([tw, ctw[:, :TOP_K]], axis=0)

        n_all = h2.shape[0] // ROW_SLAB
        row_tok, row_w, row_pair, blk_expert, blk_cnt = _route(tidx, tw)
        rows_out = _moe_rows(l, n_all, blk_expert, blk_cnt, row_tok, row_pair, row_w.reshape(-1, 1), h2,
                             e_w_gu, e_b_gu.reshape(DEPTH, N_EXPERTS, 1, -1),
                             e_w_down, e_b_down.reshape(DEPTH, N_EXPERTS, 1, -1))

        xl = _combine(x1, rows_out, g2, row2(ln2_g[l]), row2(ln2_b[l]), tm=256)
        if need_ctx:
            xc = _combine(xc1, rows_out, cg2, row2(ln2_g[l]), row2(ln2_b[l]), tm=_row_tile(m, 256), row_off=n)

    return xl[None]
```

```python
import functools
import math

import jax
import jax.numpy as jnp
import numpy as np
from jax import lax
from jax.experimental import pallas as pl
from jax.experimental.pallas import tpu as pltpu

BF = jnp.bfloat16
F32 = jnp.float32

D_MODEL = 2048
DEPTH = 2
GRID_W = 64
BLOCK = 128
A_GROUPS = 4
C_HEADS = 4
C_HD = 64
C_VD = 128
D_HEADS = 8
D_KV_HEADS = 2
D_HD = 64
WINDOW = 128
N_BRANCH = 4
BRANCH_W = 512
PROJ_SIZES = (512, 512, 512, 512, 512, 512, 512, 512, 512, 128, 128, N_BRANCH * D_MODEL)
N_EXPERTS = 32
TOP_K = 4
D_FF = 1024
SWIGLU_LIMIT = 7.0
SWIGLU_ALPHA = 1.702
ROPE_DIM = 64
ROPE_BASE = 10000.0
LN_EPS = 1e-5
DN_ALPHA = (2 * DEPTH) ** 0.25

LANE = 128
NEG_BIG = -1e30

GL_W = N_BRANCH * D_MODEL
PROJ_TN = 256
SEG_W = 9 * 512 + 256
P_W = GL_W + SEG_W
N_SEG_TILES = SEG_W // PROJ_TN
COL_AU, COL_AV, COL_BX, COL_BB, COL_BC, COL_CQ, COL_CK, COL_CV, COL_DQ = [GL_W + 512 * s for s in range(9)]
COL_DK = GL_W + 9 * 512
COL_DV = COL_DK + 128
ROPE_FULL_TILES = (10, 11, 12, 13, 16, 17)
ROPE_PART_TILE = 18

MOE_BLK = 256
ROW_SLAB = D_MODEL // LANE
BUF_PITCH = 24


def _cparams(dims, vmem_mib):
    return pltpu.CompilerParams(dimension_semantics=dims, vmem_limit_bytes=vmem_mib * 1024 * 1024)


def _const_spec(shape):
    nd = len(shape)
    return pl.BlockSpec(shape, lambda *_: (0,) * nd, pipeline_mode=pl.Buffered(1))


def _layer_norm_rows(r, g, b):
    mu = jnp.mean(r, axis=-1, keepdims=True)
    d = r - mu
    var = jnp.mean(d * d, axis=-1, keepdims=True)
    return d * lax.rsqrt(var + LN_EPS) * g + b


def _ada_kernel(c_ref, w_ref, b_ref, o_ref):
    cs = c_ref[...]
    s = cs * jax.nn.sigmoid(cs)
    o_ref[0] = jnp.dot(s.astype(BF), w_ref[0].astype(BF), preferred_element_type=F32) + b_ref[0]


def _ada(c_rows, w_ada, b_ada):
    n_l, _, n_out = w_ada.shape
    tn = 1536
    return pl.pallas_call(
        _ada_kernel,
        grid=(n_l, n_out // tn),
        in_specs=[pl.BlockSpec((8, D_MODEL), lambda l, j: (0, 0)),
                  pl.BlockSpec((1, D_MODEL, tn), lambda l, j: (l, 0, j)),
                  pl.BlockSpec((1, 1, tn), lambda l, j: (l, 0, j))],
        out_specs=pl.BlockSpec((1, 8, tn), lambda l, j: (l, 0, j)),
        out_shape=jax.ShapeDtypeStruct((n_l, 8, n_out), F32),
        compiler_params=_cparams(("arbitrary", "arbitrary"), 40),
        name="ada",
    )(c_rows, w_ada, b_ada.reshape(n_l, 1, n_out))


def _rope_rotate(a, cos_ref, sin_ref):
    w = a.shape[1]
    lane = lax.broadcasted_iota(jnp.int32, a.shape, 1)
    first = jnp.bitwise_and(lane, 16) == 0
    swapped = jnp.where(first, pltpu.roll(a, w - 16, 1), pltpu.roll(a, 16, 1))
    reps = w // LANE
    cos = cos_ref[...]
    sin = sin_ref[...]
    if reps > 1:
        cos = jnp.concatenate([cos] * reps, axis=1)
        sin = jnp.concatenate([sin] * reps, axis=1)
    return a * cos + swapped * sin


def _proj_kernel(x_ref, sc_ref, sh_ref, w_ref, cos_ref, sin_ref, o_ref, h_scr, *, rope):
    j = pl.program_id(1)

    @pl.when(j == 0)
    def _():
        h_scr[...] = (x_ref[...] * (1.0 + sc_ref[...]) + sh_ref[...]).astype(BF)

    acc = jnp.dot(h_scr[...], w_ref[0].astype(BF), preferred_element_type=F32)
    if not rope:
        o_ref[...] = acc.astype(BF)
        return

    full = functools.reduce(jnp.logical_or, [j == t for t in ROPE_FULL_TILES])
    part = j == ROPE_PART_TILE

    @pl.when(full)
    def _():
        o_ref[...] = _rope_rotate(acc, cos_ref, sin_ref).astype(BF)

    @pl.when(part)
    def _():
        o_ref[:, :LANE] = _rope_rotate(acc[:, :LANE], cos_ref, sin_ref).astype(BF)
        o_ref[:, LANE:] = acc[:, LANE:].astype(BF)

    @pl.when(jnp.logical_not(jnp.logical_or(full, part)))
    def _():
        o_ref[...] = acc.astype(BF)


def _proj(layer, x2d, sc, sh, w_in, cos_t, sin_t, *, rope, tm):
    n = x2d.shape[0]
    kern = functools.partial(_proj_kernel, rope=rope)
    n_gl_tiles = GL_W // PROJ_TN
    out_tile = lambda j: jnp.where(j < N_SEG_TILES, j + n_gl_tiles, j - N_SEG_TILES)
    return pl.pallas_call(
        kern,
        grid=(n // tm, P_W // PROJ_TN),
        in_specs=[pl.BlockSpec((tm, D_MODEL), lambda i, j: (i, 0), pipeline_mode=pl.Buffered(1)),
                  pl.BlockSpec((1, D_MODEL), lambda i, j: (0, 0)),
                  pl.BlockSpec((1, D_MODEL), lambda i, j: (0, 0)),
                  pl.BlockSpec((1, D_MODEL, PROJ_TN), lambda i, j: (layer, 0, j)),
                  pl.BlockSpec((tm, LANE), lambda i, j: (i, 0)),
                  pl.BlockSpec((tm, LANE), lambda i, j: (i, 0))],
        out_specs=pl.BlockSpec((tm, PROJ_TN), lambda i, j: (i, out_tile(j))),
        out_shape=jax.ShapeDtypeStruct((n, P_W), BF),
        scratch_shapes=[pltpu.VMEM((tm, D_MODEL), BF)],
        compiler_params=_cparams(("arbitrary", "arbitrary"), 56),
        name="proj_rope" if rope else "proj_ctx",
    )(x2d, sc, sh, w_in, cos_t, sin_t)


_ACC_ROWS = 32
_ONES_ROWS = 16
LOG2E = math.log2(math.e)


def _diff_attn_kernel(lam_ref, qt_ref, k_ref, vt_ref, g_ref, o_ref, qbd_scr, m_scr, acc_scr, s_scr, mx_scr, *,
                      tq, tk, sub, n_chunks, post_scale):
    w = 2 * tq
    qt = qt_ref[...].astype(F32) * (C_HD ** -0.5 * LOG2E)
    row = lax.broadcasted_iota(jnp.int32, qt.shape, 0)
    qbd_scr[...] = jnp.concatenate([jnp.where(row < C_HD, qt, 0.0), jnp.where(row >= C_HD, qt, 0.0)],
                                   axis=1).astype(BF)
    m_scr[...] = jnp.full(m_scr.shape, -jnp.inf, F32)
    acc_scr[...] = jnp.zeros(acc_scr.shape, F32)

    n_sub = tk // sub

    def scores_sub(c, slot, j):
        k = k_ref[0, pl.ds(pl.multiple_of(c * tk + j * sub, sub), sub), :]
        s = jnp.dot(k, qbd_scr[...], preferred_element_type=F32)
        s_scr[slot, j * sub:(j + 1) * sub, :] = s
        return jnp.max(s.reshape(sub // _ACC_ROWS, _ACC_ROWS, w), axis=0)

    def step(c, slot, nxt):
        m_old = m_scr[...]
        m_new = jnp.maximum(m_old, jnp.max(mx_scr[slot], axis=0, keepdims=True))
        alpha = jnp.exp2(m_old - m_new)
        pv = None
        mx = None
        for j in range(n_sub):
            if nxt is not None:
                mj = scores_sub(nxt, 1 - slot, j)
                mx = mj if mx is None else jnp.maximum(mx, mj)
            p = jnp.exp2(s_scr[slot, j * sub:(j + 1) * sub, :] - m_new).astype(BF)
            d = jnp.dot(vt_ref[0, c, :, j * sub:(j + 1) * sub], p, preferred_element_type=F32)
            pv = d if pv is None else pv + d
        acc_scr[...] = acc_scr[...] * alpha + pv
        m_scr[...] = m_new
        if nxt is not None:
            mx_scr[1 - slot] = mx

    mx0 = None
    for j in range(n_sub):
        mj = scores_sub(0, 0, j)
        mx0 = mj if mx0 is None else jnp.maximum(mx0, mj)
    mx_scr[0] = mx0
    n_pairs = (n_chunks - 1) // 2

    def pair(i, carry):
        c = 2 * i
        step(c, 0, c + 1)
        step(c + 1, 1, c + 2)
        return carry

    if n_pairs > 0:
        lax.fori_loop(0, n_pairs, pair, 0)
    for c in range(2 * n_pairs, n_chunks):
        step(c, c % 2, c + 1 if c + 1 < n_chunks else None)

    o = acc_scr[:C_VD, :] / acc_scr[C_VD:C_VD + 1, :]
    od = o[:, :tq] - lam_ref[0] * o[:, tq:]
    ms = jnp.mean(od * od, axis=0, keepdims=True)
    on = od * lax.rsqrt(ms + LN_EPS) * g_ref[...] * post_scale
    o_ref[...] = on.T.astype(BF)


def _pick_tk(n_k):
    for tk in (1280, 1024, 768, 512, 256, 128):
        if n_k % tk == 0:
            return tk
    raise ValueError(f"unsupported key count {n_k}")


def _diff_attn(lam, q, k_all, v_all, subln_g, lam_init):
    n = q.shape[0]
    n_k = k_all.shape[0]
    tq = 256
    tk = _pick_tk(n_k)
    n_chunks = n_k // tk
    qt = q.T
    kh = k_all.reshape(n_k, C_HEADS, C_VD).transpose(1, 0, 2)
    vt = v_all.reshape(n_chunks, tk, C_HEADS, C_VD).transpose(2, 0, 3, 1)
    vt = jnp.concatenate([vt, jnp.ones((C_HEADS, n_chunks, _ONES_ROWS, tk), BF)], axis=2)
    sub = 256 if tk % 256 == 0 else tk
    kern = functools.partial(_diff_attn_kernel, tq=tq, tk=tk, sub=sub, n_chunks=n_chunks,
                             post_scale=1.0 - lam_init)
    return pl.pallas_call(
        kern,
        grid=(C_HEADS, n // tq),
        in_specs=[pl.BlockSpec(memory_space=pltpu.SMEM),
                  pl.BlockSpec((C_VD, tq), lambda h, i: (h, i)),
                  pl.BlockSpec((1, n_k, C_VD), lambda h, i: (h, 0, 0)),
                  pl.BlockSpec((1, n_chunks, C_VD + _ONES_ROWS, tk), lambda h, i: (h, 0, 0, 0)),
                  pl.BlockSpec((C_VD, 1), lambda h, i: (0, 0))],
        out_specs=pl.BlockSpec((tq, C_VD), lambda h, i: (i, h)),
        out_shape=jax.ShapeDtypeStruct((n, C_HEADS * C_VD), BF),
        scratch_shapes=[pltpu.VMEM((C_VD, 2 * tq), BF), pltpu.VMEM((1, 2 * tq), F32),
                        pltpu.VMEM((C_VD + _ONES_ROWS, 2 * tq), F32),
                        pltpu.VMEM((2, tk, 2 * tq), F32), pltpu.VMEM((2, _ACC_ROWS, 2 * tq), F32)],
        compiler_params=_cparams(("arbitrary", "arbitrary"), 48),
        name="diff_attn",
    )(lam, qt, kh, vt, subln_g.reshape(C_VD, 1))


def _win_heads(q, kb, vb, valid, sink_ref):
    outs = []
    grp = D_HEADS // D_KV_HEADS
    for h in range(D_HEADS):
        kh = h // grp
        qh = q[:, h * D_HD:(h + 1) * D_HD]
        k_h = kb[:, kh * D_HD:(kh + 1) * D_HD]
        v_h = vb[:, kh * D_HD:(kh + 1) * D_HD]
        s = lax.dot_general(qh, k_h, (((1,), (1,)), ((), ())), preferred_element_type=F32) * (D_HD ** -0.5)
        if valid is not None:
            s = jnp.where(valid, s, NEG_BIG)
        sk = sink_ref[h]
        m = jnp.maximum(jnp.max(s, axis=-1, keepdims=True), sk)
        e = jnp.exp(s - m)
        l = jnp.sum(e, axis=-1, keepdims=True) + jnp.exp(sk - m)
        p = (e / l).astype(BF)
        outs.append(jnp.dot(p, v_h, preferred_element_type=F32))
    return jnp.concatenate(outs, axis=1).astype(BF)


def _win_attn_ctx_kernel(sink_ref, q_ref, kx_ref, vx_ref, o_ref):
    o_ref[...] = _win_heads(q_ref[...], kx_ref[...], vx_ref[...], None, sink_ref)


def _win_attn_t_kernel(sink_ref, qt_ref, kp_ref, kc_ref, kn_ref, kx_ref, vtp_ref, vtc_ref, vtn_ref, vtx_ref, o_ref, *,
                       n_tok, n_ctx):
    i = pl.program_id(0)
    grp = D_HEADS // D_KV_HEADS
    kfull = jnp.concatenate([kp_ref[...], kc_ref[...], kn_ref[...], kx_ref[...]], axis=0)
    vtfull = jnp.concatenate([vtp_ref[...], vtc_ref[...], vtn_ref[...], vtx_ref[...]], axis=1)
    n_keys = 3 * BLOCK + n_ctx
    kk = lax.broadcasted_iota(jnp.int32, (n_keys, BLOCK), 0)
    qq = lax.broadcasted_iota(jnp.int32, (n_keys, BLOCK), 1)
    kpos = (i - 1) * BLOCK + kk
    in_band = (jnp.abs(kk - BLOCK - qq) <= WINDOW) & (kpos >= 0) & (kpos < n_tok)
    bias = jnp.where((kk >= 3 * BLOCK) | in_band, 0.0, NEG_BIG)
    bias = jnp.concatenate([bias] * grp, axis=1)
    qt = (qt_ref[...].astype(F32) * (D_HD ** -0.5)).astype(BF)
    zeros = jnp.zeros((D_HD, grp * BLOCK), BF)
    for g in range(D_KV_HEADS):
        q64 = jnp.concatenate([qt[(g * grp + hh) * D_HD:(g * grp + hh + 1) * D_HD, :] for hh in range(grp)], axis=1)
        qbd = jnp.concatenate([q64, zeros] if g == 0 else [zeros, q64], axis=0)
        s = jnp.dot(kfull, qbd, preferred_element_type=F32) + bias
        sk = sink_ref[g]
        m = jnp.maximum(jnp.max(s, axis=0, keepdims=True), sk)
        e = jnp.exp(s - m)
        l = jnp.sum(e, axis=0, keepdims=True) + jnp.exp(sk - m)
        p = (e * (1.0 / l)).astype(BF)
        ot = jnp.dot(vtfull, p, preferred_element_type=F32)
        og = ot[g * D_HD:(g + 1) * D_HD, :]
        for hh in range(grp):
            h = g * grp + hh
            o_ref[h * D_HD:(h + 1) * D_HD, :] = og[:, hh * BLOCK:(hh + 1) * BLOCK].astype(BF)


def _win_attn(sink, p_lat, p_ctx):
    n = p_lat.shape[0]
    m = p_ctx.shape[0]
    nb = n // BLOCK
    grp = D_HEADS // D_KV_HEADS
    ck = COL_DK // LANE
    qt = lax.slice_in_dim(p_lat, COL_DQ, COL_DQ + 512, axis=1).T
    vt = lax.slice_in_dim(p_lat, COL_DV, COL_DV + LANE, axis=1).T
    vtx = lax.slice_in_dim(p_ctx, COL_DV, COL_DV + LANE, axis=1).T
    sink_rows = jnp.repeat(sink.astype(F32).reshape(D_KV_HEADS, grp), BLOCK, axis=1).reshape(D_KV_HEADS, 1, grp * BLOCK)
    kern = functools.partial(_win_attn_t_kernel, n_tok=n, n_ctx=m)
    prev = lambda i: jnp.maximum(i - 1, 0)
    nxt = lambda i: jnp.minimum(i + 1, nb - 1)
    o_t = pl.pallas_call(
        kern,
        grid=(nb,),
        in_specs=[pl.BlockSpec((D_KV_HEADS, 1, grp * BLOCK), lambda i: (0, 0, 0)),
                  pl.BlockSpec((512, BLOCK), lambda i: (0, i)),
                  pl.BlockSpec((BLOCK, LANE), lambda i: (prev(i), ck)),
                  pl.BlockSpec((BLOCK, LANE), lambda i: (i, ck)),
                  pl.BlockSpec((BLOCK, LANE), lambda i: (nxt(i), ck)),
                  pl.BlockSpec((m, LANE), lambda i: (0, ck)),
                  pl.BlockSpec((LANE, BLOCK), lambda i: (0, prev(i))),
                  pl.BlockSpec((LANE, BLOCK), lambda i: (0, i)),
                  pl.BlockSpec((LANE, BLOCK), lambda i: (0, nxt(i))),
                  pl.BlockSpec((LANE, m), lambda i: (0, 0))],
        out_specs=pl.BlockSpec((512, BLOCK), lambda i: (0, i)),
        out_shape=jax.ShapeDtypeStruct((512, n), BF),
        compiler_params=_cparams(("arbitrary",), 32),
        name="win_attn",
    )(sink_rows, qt, p_lat, p_lat, p_lat, p_ctx, vt, vt, vt, vtx)
    return o_t.T


def _win_attn_ctx(sink, p_ctx):
    m = p_ctx.shape[0]
    ck, cv, cq = COL_DK // LANE, COL_DV // LANE, COL_DQ // 512
    return pl.pallas_call(
        _win_attn_ctx_kernel,
        grid=(m // BLOCK,),
        in_specs=[pl.BlockSpec(memory_space=pltpu.SMEM),
                  pl.BlockSpec((BLOCK, 512), lambda i: (i, cq)),
                  pl.BlockSpec((m, LANE), lambda i: (0, ck)),
                  pl.BlockSpec((m, LANE), lambda i: (0, cv))],
        out_specs=pl.BlockSpec((BLOCK, 512), lambda i: (i, 0)),
        out_shape=jax.ShapeDtypeStruct((m, 512), BF),
        compiler_params=_cparams(("arbitrary",), 32),
        name="win_attn_ctx",
    )(sink, p_ctx, p_ctx, p_ctx)


def _merge_kernel(gl_ref, au_ref, av_ref, bx_ref, bb_ref, bc_ref, bxp_ref, bcp_ref, bxn_ref, bcn_ref,
                  brc_ref, brd_ref, x_ref, mod_ref, bgate_ref, alng_ref, alnb_ref, ws_ref, bs_ref, cw_ref, cb_ref,
                  wbr_ref, wout_ref, ln1g_ref, ln1b_ref, wr_ref, brt_ref,
                  x1_ref, h2_ref, tidx_ref, tw_ref, *, tm):
    i = pl.program_id(0)
    last = pl.num_programs(0) - 1

    u = jax.nn.gelu(au_ref[...].astype(F32), approximate=True)
    v = jax.nn.gelu(av_ref[...].astype(F32), approximate=True)
    vn = _layer_norm_rows(v, alng_ref[...], alnb_ref[...]).astype(BF)
    blocks = []
    for b in range(tm // BLOCK):
        cols = []
        for g in range(A_GROUPS):
            vbg = vn[b * BLOCK:(b + 1) * BLOCK, g * LANE:(g + 1) * LANE]
            cols.append(jnp.dot(ws_ref[g], vbg, preferred_element_type=F32))
        blocks.append(jnp.concatenate(cols, axis=1) + bs_ref[...])
    mixed = jnp.concatenate(blocks, axis=0) if len(blocks) > 1 else blocks[0]
    br_a = u * mixed

    z = bc_ref[...].astype(F32) * bx_ref[...].astype(F32)
    z_prev = bcp_ref[7:8, :].astype(F32) * bxp_ref[7:8, :].astype(F32) * (i > 0).astype(F32)
    z_next = bcn_ref[0:1, :].astype(F32) * bxn_ref[0:1, :].astype(F32) * (i < last).astype(F32)
    row = lax.broadcasted_iota(jnp.int32, z.shape, 0)
    z_up = jnp.where(row == 0, z_prev, pltpu.roll(z, 1, 0))
    z_dn = jnp.where(row == tm - 1, z_next, pltpu.roll(z, tm - 1, 0))
    y_conv = cw_ref[0:1, :] * z_up + cw_ref[1:2, :] * z + cw_ref[2:3, :] * z_dn + cb_ref[...]
    br_b = bb_ref[...].astype(F32) * y_conv

    branches = (br_a.astype(BF), br_b.astype(BF), brc_ref[...], brd_ref[...])
    merged = None
    for g in range(N_BRANCH):
        pr = jnp.dot(branches[g], wbr_ref[g], preferred_element_type=F32)
        gate = jax.nn.sigmoid(gl_ref[:, g * D_MODEL:(g + 1) * D_MODEL].astype(F32) + bgate_ref[g:g + 1, :])
        merged = gate * pr if merged is None else merged + gate * pr
    y = jnp.dot(merged.astype(BF), wout_ref[...], preferred_element_type=F32)

    r = DN_ALPHA * x_ref[...] + mod_ref[0:1, :] * y
    x1 = _layer_norm_rows(r, ln1g_ref[...], ln1b_ref[...])
    x1_ref[...] = x1
    h2f = x1 * (1.0 + mod_ref[1:2, :]) + mod_ref[2:3, :]
    for j in range(ROW_SLAB):
        h2_ref[pl.ds(j, tm, stride=ROW_SLAB), :] = h2f[:, j * LANE:(j + 1) * LANE]
    h2 = h2f.astype(BF)
    logits = jnp.dot(h2, wr_ref[...], preferred_element_type=F32) + brt_ref[...]

    lane = lax.broadcasted_iota(jnp.int32, logits.shape, 1)
    vals, idxs = [], []
    cur = logits
    for _ in range(TOP_K):
        mx = jnp.max(cur, axis=-1, keepdims=True)
        ix = jnp.min(jnp.where(cur == mx, lane, LANE), axis=-1, keepdims=True)
        vals.append(mx)
        idxs.append(ix)
        cur = jnp.where(lane == ix, -jnp.inf, cur)
    es = [jnp.exp(vk - vals[0]) for vk in vals]
    den = es[0] + es[1] + es[2] + es[3]
    tidx = jnp.zeros(logits.shape, jnp.int32)
    tw = jnp.zeros(logits.shape, F32)
    for k in range(TOP_K):
        tidx = jnp.where(lane == k, idxs[k], tidx)
        tw = jnp.where(lane == k, es[k] / den, tw)
    tidx_ref[...] = tidx
    tw_ref[...] = tw


def _merge(p, br_c, br_d, x2d, mod3, lw, *, tm):
    n = p.shape[0]
    nt = n // tm
    r8 = tm // 8
    c512 = lambda col: col // 512
    kern = functools.partial(_merge_kernel, tm=tm)
    prev8 = lambda i: (jnp.maximum(i * r8 - 1, 0))
    next8 = lambda i: (jnp.minimum((i + 1) * r8, n // 8 - 1))
    seg = lambda col: pl.BlockSpec((tm, 512), lambda i: (i, c512(col)))
    in_specs = [
        pl.BlockSpec((tm, GL_W), lambda i: (i, 0)),
        seg(COL_AU), seg(COL_AV), seg(COL_BX), seg(COL_BB), seg(COL_BC),
        pl.BlockSpec((8, 512), lambda i: (prev8(i), c512(COL_BX))),
        pl.BlockSpec((8, 512), lambda i: (prev8(i), c512(COL_BC))),
        pl.BlockSpec((8, 512), lambda i: (next8(i), c512(COL_BX))),
        pl.BlockSpec((8, 512), lambda i: (next8(i), c512(COL_BC))),
        pl.BlockSpec((tm, 512), lambda i: (i, 0)),
        pl.BlockSpec((tm, 512), lambda i: (i, 0)),
        pl.BlockSpec((tm, D_MODEL), lambda i: (i, 0)),
        _const_spec((8, D_MODEL)),
        _const_spec((N_BRANCH, D_MODEL)),
        _const_spec((1, 512)), _const_spec((1, 512)),
        _const_spec((A_GROUPS, BLOCK, BLOCK)),
        _const_spec((BLOCK, 512)),
        _const_spec((3, 512)), _const_spec((1, 512)),
        _const_spec((N_BRANCH, BRANCH_W, D_MODEL)),
        _const_spec((D_MODEL, D_MODEL)),
        _const_spec((1, D_MODEL)), _const_spec((1, D_MODEL)),
        _const_spec((D_MODEL, LANE)), _const_spec((1, LANE)),
    ]
    out_specs = [pl.BlockSpec((tm, D_MODEL), lambda i: (i, 0)),
                 pl.BlockSpec((tm * ROW_SLAB, LANE), lambda i: (i, 0)),
                 pl.BlockSpec((tm, LANE), lambda i: (i, 0)),
                 pl.BlockSpec((tm, LANE), lambda i: (i, 0))]
    out_shape = [jax.ShapeDtypeStruct((n, D_MODEL), F32), jax.ShapeDtypeStruct((n * ROW_SLAB, LANE), F32),
                 jax.ShapeDtypeStruct((n, LANE), jnp.int32), jax.ShapeDtypeStruct((n, LANE), F32)]
    return pl.pallas_call(
        kern, grid=(nt,), in_specs=in_specs, out_specs=out_specs, out_shape=out_shape,
        compiler_params=_cparams(("arbitrary",), 56),
        name="merge",
    )(p, p, p, p, p, p, p, p, p, p, br_c, br_d, x2d, mod3, lw['b_gate'], lw['a_ln_g'], lw['a_ln_b'],
      lw['a_ws'], lw['a_bs_full'], lw['b_conv_w'], lw['b_conv_b'], lw['w_br'], lw['w_out'],
      lw['ln1_g'], lw['ln1_b'], lw['w_router'], lw['b_router'])


GU_CHUNKS = 8


def _moe_kernel(be_ref, bc_ref, nx_ref, tokc_ref, tokn_ref, pair_ref, rw_ref, bgu_ref, bd_ref, wgu_hbm, wd_hbm, h2_hbm,
                out_hbm, wgu_f32, wd_f32, wgu_bf, wd_bf, xbuf, obuf, gsem, ssem, wsem, *, layer):
    i = pl.program_id(0)
    slot = lax.rem(i, 2)
    cnt = bc_ref[i]
    prev_cnt = jnp.where(i > 0, bc_ref[jnp.maximum(i - 1, 0)], 0)
    first = jnp.logical_and(i == 0, cnt > 0)
    changed = jnp.logical_or(i == 0, be_ref[i] != be_ref[jnp.maximum(i - 1, 0)])

    def weight_copies(e):
        return (pltpu.make_async_copy(wgu_hbm.at[layer, e], wgu_f32, wsem.at[0]),
                pltpu.make_async_copy(wd_hbm.at[layer, e], wd_f32, wsem.at[1]))

    def gather_copy(tok, r, sl):
        return pltpu.make_async_copy(
            h2_hbm.at[pl.ds(pl.multiple_of(tok * ROW_SLAB, ROW_SLAB), ROW_SLAB), :],
            xbuf.at[sl, pl.ds(r * BUF_PITCH, ROW_SLAB), :], gsem.at[sl])

    def scatter_copy(pair, r):
        tok = lax.shift_right_logical(pair, 2)
        return pltpu.make_async_copy(
            obuf.at[pl.ds(pl.multiple_of(r * BUF_PITCH, 8), ROW_SLAB), :],
            out_hbm.at[jnp.bitwise_and(pair, TOP_K - 1), pl.ds(pl.multiple_of(tok * ROW_SLAB, ROW_SLAB), ROW_SLAB), :],
            ssem.at[0])

    def wait_prev_scatter():
        @pl.when(prev_cnt == MOE_BLK)
        def _():
            for r in range(MOE_BLK):
                scatter_copy(0, r).wait()

        @pl.when(jnp.logical_and(prev_cnt > 0, prev_cnt < MOE_BLK))
        def _():
            def one(r, carry):
                scatter_copy(0, r).wait()
                return carry
            lax.fori_loop(0, prev_cnt, one, 0)

    def block(full):
        x = jnp.concatenate([xbuf[slot, pl.ds(j, MOE_BLK, stride=BUF_PITCH), :] for j in range(ROW_SLAB)],
                            axis=1).astype(BF)
        cw = 2 * D_FF // GU_CHUNKS
        per = MOE_BLK // GU_CHUNKS
        gus = []
        for c in range(GU_CHUNKS):
            gus.append(jnp.dot(x, wgu_bf[:, c * cw:(c + 1) * cw], preferred_element_type=F32)
                       + bgu_ref[0, 0, :, c * cw:(c + 1) * cw])
            for r in range(c * per, (c + 1) * per):
                gather_copy(tokn_ref[0, 0, r], r, 1 - slot).start(priority=r % 2)
        gu = jnp.concatenate(gus, axis=1)
        g = jnp.minimum(gu[:, :D_FF], SWIGLU_LIMIT)
        u = jnp.clip(gu[:, D_FF:], -SWIGLU_LIMIT, SWIGLU_LIMIT)
        act = (u + 1.0) * (g * jax.nn.sigmoid(SWIGLU_ALPHA * g))
        out = (jnp.dot(act.astype(BF), wd_bf[...], preferred_element_type=F32) + bd_ref[0, 0]) * rw_ref[...]
        wait_prev_scatter()
        for j in range(ROW_SLAB):
            obuf[pl.ds(j, MOE_BLK, stride=BUF_PITCH), :] = out[:, j * LANE:(j + 1) * LANE]
        if full:
            for r in range(MOE_BLK):
                scatter_copy(pair_ref[0, 0, r], r).start(priority=r % 2)
        else:
            def one(r, carry):
                scatter_copy(pair_ref[0, 0, r], r).start()
                return carry
            lax.fori_loop(0, cnt, one, 0)

    @pl.when(first)
    def _():
        for r in range(MOE_BLK):
            gather_copy(tokc_ref[0, 0, r], r, 0).start()

    @pl.when(i == 0)
    def _():
        for cp in weight_copies(be_ref[0]):
            cp.start()

    @pl.when(changed)
    def _():
        for cp in weight_copies(be_ref[i]):
            cp.wait()
        wgu_bf[...] = wgu_f32[...].astype(BF)
        wd_bf[...] = wd_f32[...].astype(BF)

        @pl.when(nx_ref[i] >= 0)
        def _():
            for cp in weight_copies(nx_ref[i]):
                cp.start()

    @pl.when(jnp.logical_or(first, prev_cnt > 0))
    def _():
        for r in range(MOE_BLK):
            gather_copy(0, r, slot).wait()

    @pl.when(cnt == MOE_BLK)
    def _():
        block(True)

    @pl.when(jnp.logical_and(cnt > 0, cnt < MOE_BLK))
    def _():
        block(False)

    @pl.when(cnt == 0)
    def _():
        wait_prev_scatter()


def _moe_rows(layer, n_all, blk_expert, blk_cnt, blk_next, row_tok, row_pair, row_w, h2_rows, e_w_gu, e_b_gu, e_w_down,
              e_b_down):
    n_blk = blk_expert.shape[0]
    idx_blk = lambda f: pl.BlockSpec((1, 1, MOE_BLK), f, memory_space=pltpu.SMEM)
    grid_spec = pltpu.PrefetchScalarGridSpec(
        num_scalar_prefetch=3,
        grid=(n_blk,),
        in_specs=[idx_blk(lambda i, be, bc, nx: (i, 0, 0)),
                  idx_blk(lambda i, be, bc, nx: (jnp.minimum(i + 1, n_blk - 1), 0, 0)),
                  idx_blk(lambda i, be, bc, nx: (i, 0, 0)),
                  pl.BlockSpec((MOE_BLK, 1), lambda i, be, bc, nx: (i, 0)),
                  pl.BlockSpec((1, 1, 1, 2 * D_FF), lambda i, be, bc, nx: (layer, be[i], 0, 0)),
                  pl.BlockSpec((1, 1, 1, D_MODEL), lambda i, be, bc, nx: (layer, be[i], 0, 0)),
                  pl.BlockSpec(memory_space=pl.ANY),
                  pl.BlockSpec(memory_space=pl.ANY),
                  pl.BlockSpec(memory_space=pl.ANY)],
        out_specs=pl.BlockSpec(memory_space=pl.ANY),
        scratch_shapes=[pltpu.VMEM((D_MODEL, 2 * D_FF), F32), pltpu.VMEM((D_FF, D_MODEL), F32),
                        pltpu.VMEM((D_MODEL, 2 * D_FF), BF), pltpu.VMEM((D_FF, D_MODEL), BF),
                        pltpu.VMEM((2, MOE_BLK * BUF_PITCH, LANE), F32), pltpu.VMEM((MOE_BLK * BUF_PITCH, LANE), F32),
                        pltpu.SemaphoreType.DMA((2,)), pltpu.SemaphoreType.DMA((1,)), pltpu.SemaphoreType.DMA((2,))],
    )
    tok3 = row_tok.reshape(n_blk, 1, MOE_BLK)
    return pl.pallas_call(
        functools.partial(_moe_kernel, layer=layer), grid_spec=grid_spec,
        out_shape=jax.ShapeDtypeStruct((TOP_K, n_all * ROW_SLAB, LANE), F32),
        compiler_params=_cparams(("arbitrary",), 58),
        name="moe_experts",
    )(blk_expert, blk_cnt, blk_next, tok3, tok3, row_pair.reshape(n_blk, 1, MOE_BLK), row_w, e_b_gu, e_b_down,
      e_w_gu, e_w_down, h2_rows)


def _route(top_idx, top_w):
    n_tok = top_idx.shape[0]
    n_pairs = n_tok * TOP_K
    e_flat = top_idx.reshape(-1)
    onehot = e_flat[:, None] == jnp.arange(N_EXPERTS, dtype=jnp.int32)[None, :]
    counts = jnp.sum(onehot, axis=0, dtype=jnp.int32)
    padded = (counts + MOE_BLK - 1) // MOE_BLK * MOE_BLK
    starts = jnp.cumsum(counts) - counts
    pends = jnp.cumsum(padded)
    pstarts = pends - padded
    n_rows = (n_pairs + MOE_BLK - 1) // MOE_BLK * MOE_BLK + N_EXPERTS * MOE_BLK
    n_blk = n_rows // MOE_BLK
    blk_start = jnp.arange(n_blk, dtype=jnp.int32) * MOE_BLK
    blk_expert = jnp.minimum(jnp.sum(blk_start[:, None] >= pends[None, :], axis=1, dtype=jnp.int32), N_EXPERTS - 1)
    blk_valid = (blk_start < pends[-1]).astype(jnp.int32)
    order = jnp.argsort(e_flat).astype(jnp.int32)
    j_in_blk = jnp.arange(MOE_BLK, dtype=jnp.int32)[None, :]
    j_in_e = (blk_start - pstarts[blk_expert])[:, None] + j_in_blk
    row_valid = (j_in_e < counts[blk_expert][:, None]) & (blk_valid[:, None] > 0)
    src = jnp.clip(starts[blk_expert][:, None] + j_in_e, 0, n_pairs - 1).reshape(-1)
    row_pair = order[src]
    row_valid = row_valid.reshape(-1)
    row_tok = jnp.where(row_valid, row_pair // TOP_K, 0)
    row_w = jnp.where(row_valid, top_w.reshape(-1)[row_pair], 0.0)
    row_pair = jnp.where(row_valid, row_pair, 0)
    blk_cnt = jnp.clip(counts[blk_expert] - (blk_start - pstarts[blk_expert]), 0, MOE_BLK) * blk_valid
    later = jnp.arange(n_blk)[None, :] > jnp.arange(n_blk)[:, None]
    differs = later & (blk_expert[None, :] != blk_expert[:, None])
    blk_next = jnp.where(jnp.any(differs, axis=1), blk_expert[jnp.argmax(differs, axis=1)], -1).astype(jnp.int32)
    return row_tok, row_w, row_pair, blk_expert, blk_cnt, blk_next


def _combine_kernel(x1_ref, r_ref, g2_ref, lng_ref, lnb_ref, o_ref, *, tm):
    f = None
    for k in range(TOP_K):
        fk = jnp.concatenate([r_ref[k, pl.ds(j, tm, stride=ROW_SLAB), :] for j in range(ROW_SLAB)], axis=1)
        f = fk if f is None else f + fk
    r = DN_ALPHA * x1_ref[...] + g2_ref[...] * f
    o_ref[...] = _layer_norm_rows(r, lng_ref[...], lnb_ref[...])


def _combine(x1, rows, g2, ln_g, ln_b, *, tm, row_off=0):
    n = x1.shape[0]
    assert row_off % tm == 0
    blk_off = row_off // tm
    return pl.pallas_call(
        functools.partial(_combine_kernel, tm=tm), grid=(n // tm,),
        in_specs=[pl.BlockSpec((tm, D_MODEL), lambda i: (i, 0)),
                  pl.BlockSpec((TOP_K, tm * ROW_SLAB, LANE), lambda i: (0, i + blk_off, 0)),
                  pl.BlockSpec((1, D_MODEL), lambda i: (0, 0)),
                  pl.BlockSpec((1, D_MODEL), lambda i: (0, 0)),
                  pl.BlockSpec((1, D_MODEL), lambda i: (0, 0))],
        out_specs=pl.BlockSpec((tm, D_MODEL), lambda i: (i, 0)),
        out_shape=jax.ShapeDtypeStruct((n, D_MODEL), F32),
        compiler_params=_cparams(("arbitrary",), 48),
        name="combine_ln2",
    )(x1, rows, g2, ln_g, ln_b)


def _rope_tables(n_tok):
    n_freq = ROPE_DIM // 4
    inv = (np.float32(ROPE_BASE) ** (-np.arange(n_freq, dtype=np.float32) / np.float32(n_freq))).astype(np.float32)
    t = np.arange(n_tok)
    ang_r = (t // GRID_W).astype(np.float32)[:, None] * inv
    ang_c = (t % GRID_W).astype(np.float32)[:, None] * inv
    cos64 = np.concatenate([np.cos(ang_r), np.cos(ang_r), np.cos(ang_c), np.cos(ang_c)], axis=1)
    sin64 = np.concatenate([-np.sin(ang_r), np.sin(ang_r), -np.sin(ang_c), np.sin(ang_c)], axis=1)
    cos_t = np.concatenate([cos64, cos64], axis=1).astype(np.float32)
    sin_t = np.concatenate([sin64, sin64], axis=1).astype(np.float32)
    return jnp.asarray(cos_t), jnp.asarray(sin_t)


def _row_tile(n, pref):
    return pref if n % pref == 0 else n


def kernel(x, c, ctx, c_ctx, w_ada, b_ada, w_in, b_gate, a_ln_g, a_ln_b, a_ws, a_bs, b_conv_w, b_conv_b, c_lq1, c_lk1, c_lq2, c_lk2, c_subln_g, d_sink, w_br, w_out, ln1_g, ln1_b, w_router, b_router, e_w_gu, e_b_gu, e_w_down, e_b_down, ln2_g, ln2_b):
    assert x.shape[0] == 1 and ctx.shape[0] == 1
    n, m = x.shape[1], ctx.shape[1]
    assert n % 256 == 0 and m % BLOCK == 0
    xl = x[0]
    xc = ctx[0]

    c_rows = jnp.zeros((8, D_MODEL), F32).at[0].set(c[0]).at[1].set(c_ctx)
    mods = _ada(c_rows, w_ada, b_ada)
    cos_t, sin_t = _rope_tables(n)
    cos_c = jnp.ones((m, LANE), F32)
    sin_c = jnp.zeros((m, LANE), F32)
    row2 = lambda v: v.reshape(1, -1)

    for l in range(DEPTH):
        need_ctx = l < DEPTH - 1
        lam_init = 0.8 - 0.6 * math.exp(-0.3 * l)
        sh1, sc1, g1, sh2, sc2, g2 = [row2(t) for t in jnp.split(mods[l, 0], 6)]
        csh1, csc1, cg1, csh2, csc2, cg2 = [row2(t) for t in jnp.split(mods[l, 1], 6)]
        lam = (jnp.exp(jnp.sum(c_lq1[l] * c_lk1[l])) - jnp.exp(jnp.sum(c_lq2[l] * c_lk2[l]))
               + lam_init).astype(F32).reshape(1)
        lw = {
            'b_gate': b_gate[l], 'a_ln_g': row2(a_ln_g[l]), 'a_ln_b': row2(a_ln_b[l]),
            'a_ws': a_ws[l].astype(BF),
            'a_bs_full': jnp.repeat(a_bs[l].T, BLOCK, axis=1),
            'b_conv_w': b_conv_w[l], 'b_conv_b': row2(b_conv_b[l]),
            'w_br': w_br[l].astype(BF), 'w_out': w_out[l].astype(BF),
            'ln1_g': row2(ln1_g[l]), 'ln1_b': row2(ln1_b[l]),
            'w_router': jnp.pad(w_router[l], ((0, 0), (0, LANE - N_EXPERTS))).astype(BF),
            'b_router': jnp.pad(row2(b_router[l]), ((0, 0), (0, LANE - N_EXPERTS)), constant_values=NEG_BIG),
        }

        p_lat = _proj(l, xl, sc1, sh1, w_in, cos_t, sin_t, rope=True, tm=_row_tile(n, 2048))
        p_ctx = _proj(l, xc, csc1, csh1, w_in, cos_c, sin_c, rope=False, tm=m)

        sl = lambda arr, col, w: lax.slice_in_dim(arr, col, col + w, axis=1)
        k_all = jnp.concatenate([sl(p_lat, COL_CK, 512), sl(p_ctx, COL_CK, 512)], axis=0)
        v_all = jnp.concatenate([sl(p_lat, COL_CV, 512), sl(p_ctx, COL_CV, 512)], axis=0)
        br_c = _diff_attn(lam, sl(p_lat, COL_CQ, 512), k_all, v_all, c_subln_g[l], lam_init)
        br_d = _win_attn(d_sink[l], p_lat, p_ctx)

        mod3 = jnp.zeros((8, D_MODEL), F32).at[0].set(g1[0]).at[1].set(sc2[0]).at[2].set(sh2[0])
        x1, h2, tidx, tw = _merge(p_lat, br_c, br_d, xl, mod3, lw, tm=256)
        tidx, tw = tidx[:, :TOP_K], tw[:, :TOP_K]

        if need_ctx:
            cbr_c = _diff_attn(lam, sl(p_ctx, COL_CQ, 512), sl(p_ctx, COL_CK, 512), sl(p_ctx, COL_CV, 512),
                               c_subln_g[l], lam_init)
            cbr_d = _win_attn_ctx(d_sink[l], p_ctx)
            cmod3 = jnp.zeros((8, D_MODEL), F32).at[0].set(cg1[0]).at[1].set(csc2[0]).at[2].set(csh2[0])
            xc1, hc2, ctidx, ctw = _merge(p_ctx, cbr_c, cbr_d, xc, cmod3, lw, tm=_row_tile(m, 256))
            h2 = jnp.concatenate([h2, hc2], axis=0)
            tidx = jnp.concatenate([tidx, ctidx[:, :TOP_K]], axis=0)
            tw = jnp.concatenate([tw, ctw[:, :TOP_K]], axis=0)

        n_all = h2.shape[0] // ROW_SLAB
        row_tok, row_w, row_pair, blk_expert, blk_cnt, blk_next = _route(tidx, tw)
        rows_out = _moe_rows(l, n_all, blk_expert, blk_cnt, blk_next, row_tok, row_pair, row_w.reshape(-1, 1), h2,
                             e_w_gu, e_b_gu.reshape(DEPTH, N_EXPERTS, 1, -1),
                             e_w_down, e_b_down.reshape(DEPTH, N_EXPERTS, 1, -1))

        xl = _combine(x1, rows_out, g2, row2(ln2_g[l]), row2(ln2_b[l]), tm=256)
        if need_ctx:
            xc = _combine(xc1, rows_out, cg2, row2(ln2_g[l]), row2(ln2_b[l]), tm=_row_tile(m, 256), row_off=n)

    return xl[None]
```

```python
import functools
import math

import jax
import jax.numpy as jnp
import numpy as np
from jax import lax
from jax.experimental import pallas as pl
from jax.experimental.pallas import tpu as pltpu

BF = jnp.bfloat16
F32 = jnp.float32

D_MODEL = 2048
DEPTH = 2
GRID_W = 64
BLOCK = 128
A_GROUPS = 4
C_HEADS = 4
C_HD = 64
C_VD = 128
D_HEADS = 8
D_KV_HEADS = 2
D_HD = 64
WINDOW = 128
N_BRANCH = 4
BRANCH_W = 512
PROJ_SIZES = (512, 512, 512, 512, 512, 512, 512, 512, 512, 128, 128, N_BRANCH * D_MODEL)
N_EXPERTS = 32
TOP_K = 4
D_FF = 1024
SWIGLU_LIMIT = 7.0
SWIGLU_ALPHA = 1.702
ROPE_DIM = 64
ROPE_BASE = 10000.0
LN_EPS = 1e-5
DN_ALPHA = (2 * DEPTH) ** 0.25

LANE = 128
NEG_BIG = -1e30

GL_W = N_BRANCH * D_MODEL
PROJ_TN = 256
SEG_W = 9 * 512 + 256
P_W = GL_W + SEG_W
N_SEG_TILES = SEG_W // PROJ_TN
COL_AU, COL_AV, COL_BX, COL_BB, COL_BC, COL_CQ, COL_CK, COL_CV, COL_DQ = [GL_W + 512 * s for s in range(9)]
COL_DK = GL_W + 9 * 512
COL_DV = COL_DK + 128
ROPE_FULL_TILES = (10, 11, 12, 13, 16, 17)
ROPE_PART_TILE = 18

MOE_BLK = 256
ROW_SLAB = D_MODEL // LANE
BUF_PITCH = 24


def _cparams(dims, vmem_mib):
    return pltpu.CompilerParams(dimension_semantics=dims, vmem_limit_bytes=vmem_mib * 1024 * 1024)


def _const_spec(shape):
    nd = len(shape)
    return pl.BlockSpec(shape, lambda *_: (0,) * nd, pipeline_mode=pl.Buffered(1))


def _layer_norm_rows(r, g, b):
    mu = jnp.mean(r, axis=-1, keepdims=True)
    d = r - mu
    var = jnp.mean(d * d, axis=-1, keepdims=True)
    return d * lax.rsqrt(var + LN_EPS) * g + b


def _ada_kernel(c_ref, w_ref, b_ref, o_ref):
    cs = c_ref[...]
    s = cs * jax.nn.sigmoid(cs)
    o_ref[0] = jnp.dot(s.astype(BF), w_ref[0].astype(BF), preferred_element_type=F32) + b_ref[0]


def _ada(c_rows, w_ada, b_ada):
    n_l, _, n_out = w_ada.shape
    tn = 1536
    return pl.pallas_call(
        _ada_kernel,
        grid=(n_l, n_out // tn),
        in_specs=[pl.BlockSpec((8, D_MODEL), lambda l, j: (0, 0)),
                  pl.BlockSpec((1, D_MODEL, tn), lambda l, j: (l, 0, j)),
                  pl.BlockSpec((1, 1, tn), lambda l, j: (l, 0, j))],
        out_specs=pl.BlockSpec((1, 8, tn), lambda l, j: (l, 0, j)),
        out_shape=jax.ShapeDtypeStruct((n_l, 8, n_out), F32),
        compiler_params=_cparams(("arbitrary", "arbitrary"), 40),
        name="ada",
    )(c_rows, w_ada, b_ada.reshape(n_l, 1, n_out))


def _rope_rotate(a, cos_ref, sin_ref):
    w = a.shape[1]
    lane = lax.broadcasted_iota(jnp.int32, a.shape, 1)
    first = jnp.bitwise_and(lane, 16) == 0
    swapped = jnp.where(first, pltpu.roll(a, w - 16, 1), pltpu.roll(a, 16, 1))
    reps = w // LANE
    cos = cos_ref[...]
    sin = sin_ref[...]
    if reps > 1:
        cos = jnp.concatenate([cos] * reps, axis=1)
        sin = jnp.concatenate([sin] * reps, axis=1)
    return a * cos + swapped * sin


def _proj_kernel(x_ref, sc_ref, sh_ref, w_ref, cos_ref, sin_ref, o_ref, h_scr, *, rope):
    j = pl.program_id(1)

    @pl.when(j == 0)
    def _():
        h_scr[...] = (x_ref[...] * (1.0 + sc_ref[...]) + sh_ref[...]).astype(BF)

    acc = jnp.dot(h_scr[...], w_ref[0].astype(BF), preferred_element_type=F32)
    if not rope:
        o_ref[...] = acc.astype(BF)
        return

    full = functools.reduce(jnp.logical_or, [j == t for t in ROPE_FULL_TILES])
    part = j == ROPE_PART_TILE

    @pl.when(full)
    def _():
        o_ref[...] = _rope_rotate(acc, cos_ref, sin_ref).astype(BF)

    @pl.when(part)
    def _():
        o_ref[:, :LANE] = _rope_rotate(acc[:, :LANE], cos_ref, sin_ref).astype(BF)
        o_ref[:, LANE:] = acc[:, LANE:].astype(BF)

    @pl.when(jnp.logical_not(jnp.logical_or(full, part)))
    def _():
        o_ref[...] = acc.astype(BF)


def _proj(layer, x2d, sc, sh, w_in, cos_t, sin_t, *, rope, tm):
    n = x2d.shape[0]
    kern = functools.partial(_proj_kernel, rope=rope)
    n_gl_tiles = GL_W // PROJ_TN
    out_tile = lambda j: jnp.where(j < N_SEG_TILES, j + n_gl_tiles, j - N_SEG_TILES)
    return pl.pallas_call(
        kern,
        grid=(n // tm, P_W // PROJ_TN),
        in_specs=[pl.BlockSpec((tm, D_MODEL), lambda i, j: (i, 0), pipeline_mode=pl.Buffered(1)),
                  pl.BlockSpec((1, D_MODEL), lambda i, j: (0, 0)),
                  pl.BlockSpec((1, D_MODEL), lambda i, j: (0, 0)),
                  pl.BlockSpec((1, D_MODEL, PROJ_TN), lambda i, j: (layer, 0, j)),
                  pl.BlockSpec((tm, LANE), lambda i, j: (i, 0)),
                  pl.BlockSpec((tm, LANE), lambda i, j: (i, 0))],
        out_specs=pl.BlockSpec((tm, PROJ_TN), lambda i, j: (i, out_tile(j))),
        out_shape=jax.ShapeDtypeStruct((n, P_W), BF),
        scratch_shapes=[pltpu.VMEM((tm, D_MODEL), BF)],
        compiler_params=_cparams(("arbitrary", "arbitrary"), 56),
        name="proj_rope" if rope else "proj_ctx",
    )(x2d, sc, sh, w_in, cos_t, sin_t)


_ACC_ROWS = 32
_ONES_ROWS = 16
LOG2E = math.log2(math.e)


def _diff_attn_kernel(lam_ref, qt_ref, k_ref, vt_ref, g_ref, o_ref, qbd_scr, m_scr, acc_scr, s_scr, mx_scr, *,
                      tq, tk, sub, n_chunks, post_scale):
    w = 2 * tq
    qt = qt_ref[...].astype(F32) * (C_HD ** -0.5 * LOG2E)
    row = lax.broadcasted_iota(jnp.int32, qt.shape, 0)
    qbd_scr[...] = jnp.concatenate([jnp.where(row < C_HD, qt, 0.0), jnp.where(row >= C_HD, qt, 0.0)],
                                   axis=1).astype(BF)
    m_scr[...] = jnp.full(m_scr.shape, -jnp.inf, F32)
    acc_scr[...] = jnp.zeros(acc_scr.shape, F32)

    n_sub = tk // sub

    def scores_sub(c, slot, j):
        k = k_ref[0, pl.ds(pl.multiple_of(c * tk + j * sub, sub), sub), :]
        s = jnp.dot(k, qbd_scr[...], preferred_element_type=F32)
        s_scr[slot, j * sub:(j + 1) * sub, :] = s
        return jnp.max(s.reshape(sub // _ACC_ROWS, _ACC_ROWS, w), axis=0)

    def step(c, slot, nxt):
        m_old = m_scr[...]
        m_new = jnp.maximum(m_old, jnp.max(mx_scr[slot], axis=0, keepdims=True))
        alpha = jnp.exp2(m_old - m_new)
        pv = None
        mx = None
        for j in range(n_sub):
            if nxt is not None:
                mj = scores_sub(nxt, 1 - slot, j)
                mx = mj if mx is None else jnp.maximum(mx, mj)
            p = jnp.exp2(s_scr[slot, j * sub:(j + 1) * sub, :] - m_new).astype(BF)
            d = jnp.dot(vt_ref[0, c, :, j * sub:(j + 1) * sub], p, preferred_element_type=F32)
            pv = d if pv is None else pv + d
        acc_scr[...] = acc_scr[...] * alpha + pv
        m_scr[...] = m_new
        if nxt is not None:
            mx_scr[1 - slot] = mx

    mx0 = None
    for j in range(n_sub):
        mj = scores_sub(0, 0, j)
        mx0 = mj if mx0 is None else jnp.maximum(mx0, mj)
    mx_scr[0] = mx0
    n_pairs = (n_chunks - 1) // 2

    def pair(i, carry):
        c = 2 * i
        step(c, 0, c + 1)
        step(c + 1, 1, c + 2)
        return carry

    if n_pairs > 0:
        lax.fori_loop(0, n_pairs, pair, 0)
    for c in range(2 * n_pairs, n_chunks):
        step(c, c % 2, c + 1 if c + 1 < n_chunks else None)

    o = acc_scr[:C_VD, :] / acc_scr[C_VD:C_VD + 1, :]
    od = o[:, :tq] - lam_ref[0] * o[:, tq:]
    ms = jnp.mean(od * od, axis=0, keepdims=True)
    on = od * lax.rsqrt(ms + LN_EPS) * g_ref[...] * post_scale
    o_ref[...] = on.T.astype(BF)


def _pick_tk(n_k):
    for tk in (1280, 1024, 768, 512, 256, 128):
        if n_k % tk == 0:
            return tk
    raise ValueError(f"unsupported key count {n_k}")


def _diff_attn(lam, q, k_all, v_all, subln_g, lam_init):
    n = q.shape[0]
    n_k = k_all.shape[0]
    tq = 256
    tk = _pick_tk(n_k)
    n_chunks = n_k // tk
    qt = q.T
    kh = k_all.reshape(n_k, C_HEADS, C_VD).transpose(1, 0, 2)
    vt = v_all.reshape(n_chunks, tk, C_HEADS, C_VD).transpose(2, 0, 3, 1)
    vt = jnp.concatenate([vt, jnp.ones((C_HEADS, n_chunks, _ONES_ROWS, tk), BF)], axis=2)
    sub = 256 if tk % 256 == 0 else tk
    kern = functools.partial(_diff_attn_kernel, tq=tq, tk=tk, sub=sub, n_chunks=n_chunks,
                             post_scale=1.0 - lam_init)
    return pl.pallas_call(
        kern,
        grid=(C_HEADS, n // tq),
        in_specs=[pl.BlockSpec(memory_space=pltpu.SMEM),
                  pl.BlockSpec((C_VD, tq), lambda h, i: (h, i)),
                  pl.BlockSpec((1, n_k, C_VD), lambda h, i: (h, 0, 0)),
                  pl.BlockSpec((1, n_chunks, C_VD + _ONES_ROWS, tk), lambda h, i: (h, 0, 0, 0)),
                  pl.BlockSpec((C_VD, 1), lambda h, i: (0, 0))],
        out_specs=pl.BlockSpec((tq, C_VD), lambda h, i: (i, h)),
        out_shape=jax.ShapeDtypeStruct((n, C_HEADS * C_VD), BF),
        scratch_shapes=[pltpu.VMEM((C_VD, 2 * tq), BF), pltpu.VMEM((1, 2 * tq), F32),
                        pltpu.VMEM((C_VD + _ONES_ROWS, 2 * tq), F32),
                        pltpu.VMEM((2, tk, 2 * tq), F32), pltpu.VMEM((2, _ACC_ROWS, 2 * tq), F32)],
        compiler_params=_cparams(("arbitrary", "arbitrary"), 48),
        name="diff_attn",
    )(lam, qt, kh, vt, subln_g.reshape(C_VD, 1))


def _win_heads(q, kb, vb, valid, sink_ref):
    outs = []
    grp = D_HEADS // D_KV_HEADS
    for h in range(D_HEADS):
        kh = h // grp
        qh = q[:, h * D_HD:(h + 1) * D_HD]
        k_h = kb[:, kh * D_HD:(kh + 1) * D_HD]
        v_h = vb[:, kh * D_HD:(kh + 1) * D_HD]
        s = lax.dot_general(qh, k_h, (((1,), (1,)), ((), ())), preferred_element_type=F32) * (D_HD ** -0.5)
        if valid is not None:
            s = jnp.where(valid, s, NEG_BIG)
        sk = sink_ref[h]
        m = jnp.maximum(jnp.max(s, axis=-1, keepdims=True), sk)
        e = jnp.exp(s - m)
        l = jnp.sum(e, axis=-1, keepdims=True) + jnp.exp(sk - m)
        p = (e / l).astype(BF)
        outs.append(jnp.dot(p, v_h, preferred_element_type=F32))
    return jnp.concatenate(outs, axis=1).astype(BF)


def _win_attn_ctx_kernel(sink_ref, q_ref, kx_ref, vx_ref, o_ref):
    o_ref[...] = _win_heads(q_ref[...], kx_ref[...], vx_ref[...], None, sink_ref)


def _win_attn_t_kernel(sink_ref, qt_ref, kp_ref, kc_ref, kn_ref, kx_ref, vtp_ref, vtc_ref, vtn_ref, vtx_ref, o_ref, *,
                       n_tok, n_ctx):
    i = pl.program_id(0)
    grp = D_HEADS // D_KV_HEADS
    kfull = jnp.concatenate([kp_ref[...], kc_ref[...], kn_ref[...], kx_ref[...]], axis=0)
    vtfull = jnp.concatenate([vtp_ref[...], vtc_ref[...], vtn_ref[...], vtx_ref[...]], axis=1)
    n_keys = 3 * BLOCK + n_ctx
    kk = lax.broadcasted_iota(jnp.int32, (n_keys, BLOCK), 0)
    qq = lax.broadcasted_iota(jnp.int32, (n_keys, BLOCK), 1)
    kpos = (i - 1) * BLOCK + kk
    in_band = (jnp.abs(kk - BLOCK - qq) <= WINDOW) & (kpos >= 0) & (kpos < n_tok)
    bias = jnp.where((kk >= 3 * BLOCK) | in_band, 0.0, NEG_BIG)
    bias = jnp.concatenate([bias] * grp, axis=1)
    qt = (qt_ref[...].astype(F32) * (D_HD ** -0.5)).astype(BF)
    zeros = jnp.zeros((D_HD, grp * BLOCK), BF)
    for g in range(D_KV_HEADS):
        q64 = jnp.concatenate([qt[(g * grp + hh) * D_HD:(g * grp + hh + 1) * D_HD, :] for hh in range(grp)], axis=1)
        qbd = jnp.concatenate([q64, zeros] if g == 0 else [zeros, q64], axis=0)
        s = jnp.dot(kfull, qbd, preferred_element_type=F32) + bias
        sk = sink_ref[g]
        m = jnp.maximum(jnp.max(s, axis=0, keepdims=True), sk)
        e = jnp.exp(s - m)
        l = jnp.sum(e, axis=0, keepdims=True) + jnp.exp(sk - m)
        p = (e * (1.0 / l)).astype(BF)
        ot = jnp.dot(vtfull, p, preferred_element_type=F32)
        og = ot[g * D_HD:(g + 1) * D_HD, :]
        for hh in range(grp):
            h = g * grp + hh
            o_ref[h * D_HD:(h + 1) * D_HD, :] = og[:, hh * BLOCK:(hh + 1) * BLOCK].astype(BF)


def _win_attn(sink, p_lat, p_ctx):
    n = p_lat.shape[0]
    m = p_ctx.shape[0]
    nb = n // BLOCK
    grp = D_HEADS // D_KV_HEADS
    ck = COL_DK // LANE
    qt = lax.slice_in_dim(p_lat, COL_DQ, COL_DQ + 512, axis=1).T
    vt = lax.slice_in_dim(p_lat, COL_DV, COL_DV + LANE, axis=1).T
    vtx = lax.slice_in_dim(p_ctx, COL_DV, COL_DV + LANE, axis=1).T
    sink_rows = jnp.repeat(sink.astype(F32).reshape(D_KV_HEADS, grp), BLOCK, axis=1).reshape(D_KV_HEADS, 1, grp * BLOCK)
    kern = functools.partial(_win_attn_t_kernel, n_tok=n, n_ctx=m)
    prev = lambda i: jnp.maximum(i - 1, 0)
    nxt = lambda i: jnp.minimum(i + 1, nb - 1)
    o_t = pl.pallas_call(
        kern,
        grid=(nb,),
        in_specs=[pl.BlockSpec((D_KV_HEADS, 1, grp * BLOCK), lambda i: (0, 0, 0)),
                  pl.BlockSpec((512, BLOCK), lambda i: (0, i)),
                  pl.BlockSpec((BLOCK, LANE), lambda i: (prev(i), ck)),
                  pl.BlockSpec((BLOCK, LANE), lambda i: (i, ck)),
                  pl.BlockSpec((BLOCK, LANE), lambda i: (nxt(i), ck)),
                  pl.BlockSpec((m, LANE), lambda i: (0, ck)),
                  pl.BlockSpec((LANE, BLOCK), lambda i: (0, prev(i))),
                  pl.BlockSpec((LANE, BLOCK), lambda i: (0, i)),
                  pl.BlockSpec((LANE, BLOCK), lambda i: (0, nxt(i))),
                  pl.BlockSpec((LANE, m), lambda i: (0, 0))],
        out_specs=pl.BlockSpec((512, BLOCK), lambda i: (0, i)),
        out_shape=jax.ShapeDtypeStruct((512, n), BF),
        compiler_params=_cparams(("arbitrary",), 32),
        name="win_attn",
    )(sink_rows, qt, p_lat, p_lat, p_lat, p_ctx, vt, vt, vt, vtx)
    return o_t.T


def _win_attn_ctx(sink, p_ctx):
    m = p_ctx.shape[0]
    ck, cv, cq = COL_DK // LANE, COL_DV // LANE, COL_DQ // 512
    return pl.pallas_call(
        _win_attn_ctx_kernel,
        grid=(m // BLOCK,),
        in_specs=[pl.BlockSpec(memory_space=pltpu.SMEM),
                  pl.BlockSpec((BLOCK, 512), lambda i: (i, cq)),
                  pl.BlockSpec((m, LANE), lambda i: (0, ck)),
                  pl.BlockSpec((m, LANE), lambda i: (0, cv))],
        out_specs=pl.BlockSpec((BLOCK, 512), lambda i: (i, 0)),
        out_shape=jax.ShapeDtypeStruct((m, 512), BF),
        compiler_params=_cparams(("arbitrary",), 32),
        name="win_attn_ctx",
    )(sink, p_ctx, p_ctx, p_ctx)


def _merge_kernel(gl_ref, au_ref, av_ref, bx_ref, bb_ref, bc_ref, bxp_ref, bcp_ref, bxn_ref, bcn_ref,
                  brc_ref, brd_ref, x_ref, mod_ref, bgate_ref, alng_ref, alnb_ref, ws_ref, bs_ref, cw_ref, cb_ref,
                  wbr_ref, wout_ref, ln1g_ref, ln1b_ref, wr_ref, brt_ref,
                  x1_ref, h2_ref, tidx_ref, tw_ref, *, tm):
    i = pl.program_id(0)
    last = pl.num_programs(0) - 1

    u = jax.nn.gelu(au_ref[...].astype(F32), approximate=True)
    v = jax.nn.gelu(av_ref[...].astype(F32), approximate=True)
    vn = _layer_norm_rows(v, alng_ref[...], alnb_ref[...]).astype(BF)
    blocks = []
    for b in range(tm // BLOCK):
        cols = []
        for g in range(A_GROUPS):
            vbg = vn[b * BLOCK:(b + 1) * BLOCK, g * LANE:(g + 1) * LANE]
            cols.append(jnp.dot(ws_ref[g], vbg, preferred_element_type=F32))
        blocks.append(jnp.concatenate(cols, axis=1) + bs_ref[...])
    mixed = jnp.concatenate(blocks, axis=0) if len(blocks) > 1 else blocks[0]
    br_a = u * mixed

    z = bc_ref[...].astype(F32) * bx_ref[...].astype(F32)
    z_prev = bcp_ref[7:8, :].astype(F32) * bxp_ref[7:8, :].astype(F32) * (i > 0).astype(F32)
    z_next = bcn_ref[0:1, :].astype(F32) * bxn_ref[0:1, :].astype(F32) * (i < last).astype(F32)
    row = lax.broadcasted_iota(jnp.int32, z.shape, 0)
    z_up = jnp.where(row == 0, z_prev, pltpu.roll(z, 1, 0))
    z_dn = jnp.where(row == tm - 1, z_next, pltpu.roll(z, tm - 1, 0))
    y_conv = cw_ref[0:1, :] * z_up + cw_ref[1:2, :] * z + cw_ref[2:3, :] * z_dn + cb_ref[...]
    br_b = bb_ref[...].astype(F32) * y_conv

    branches = (br_a.astype(BF), br_b.astype(BF), brc_ref[...], brd_ref[...])
    merged = None
    for g in range(N_BRANCH):
        pr = jnp.dot(branches[g], wbr_ref[g], preferred_element_type=F32)
        gate = jax.nn.sigmoid(gl_ref[:, g * D_MODEL:(g + 1) * D_MODEL].astype(F32) + bgate_ref[g:g + 1, :])
        merged = gate * pr if merged is None else merged + gate * pr
    y = jnp.dot(merged.astype(BF), wout_ref[...], preferred_element_type=F32)

    r = DN_ALPHA * x_ref[...] + mod_ref[0:1, :] * y
    x1 = _layer_norm_rows(r, ln1g_ref[...], ln1b_ref[...])
    x1_ref[...] = x1
    h2f = x1 * (1.0 + mod_ref[1:2, :]) + mod_ref[2:3, :]
    for j in range(ROW_SLAB):
        h2_ref[pl.ds(j, tm, stride=ROW_SLAB), :] = h2f[:, j * LANE:(j + 1) * LANE]
    h2 = h2f.astype(BF)
    logits = jnp.dot(h2, wr_ref[...], preferred_element_type=F32) + brt_ref[...]

    lane = lax.broadcasted_iota(jnp.int32, logits.shape, 1)
    vals, idxs = [], []
    cur = logits
    for _ in range(TOP_K):
        mx = jnp.max(cur, axis=-1, keepdims=True)
        ix = jnp.min(jnp.where(cur == mx, lane, LANE), axis=-1, keepdims=True)
        vals.append(mx)
        idxs.append(ix)
        cur = jnp.where(lane == ix, -jnp.inf, cur)
    es = [jnp.exp(vk - vals[0]) for vk in vals]
    den = es[0] + es[1] + es[2] + es[3]
    tidx = jnp.zeros(logits.shape, jnp.int32)
    tw = jnp.zeros(logits.shape, F32)
    for k in range(TOP_K):
        tidx = jnp.where(lane == k, idxs[k], tidx)
        tw = jnp.where(lane == k, es[k] / den, tw)
    tidx_ref[...] = tidx
    tw_ref[...] = tw


def _merge(p, br_c, br_d, x2d, mod3, lw, *, tm):
    n = p.shape[0]
    nt = n // tm
    r8 = tm // 8
    c512 = lambda col: col // 512
    kern = functools.partial(_merge_kernel, tm=tm)
    prev8 = lambda i: (jnp.maximum(i * r8 - 1, 0))
    next8 = lambda i: (jnp.minimum((i + 1) * r8, n // 8 - 1))
    seg = lambda col: pl.BlockSpec((tm, 512), lambda i: (i, c512(col)))
    in_specs = [
        pl.BlockSpec((tm, GL_W), lambda i: (i, 0)),
        seg(COL_AU), seg(COL_AV), seg(COL_BX), seg(COL_BB), seg(COL_BC),
        pl.BlockSpec((8, 512), lambda i: (prev8(i), c512(COL_BX))),
        pl.BlockSpec((8, 512), lambda i: (prev8(i), c512(COL_BC))),
        pl.BlockSpec((8, 512), lambda i: (next8(i), c512(COL_BX))),
        pl.BlockSpec((8, 512), lambda i: (next8(i), c512(COL_BC))),
        pl.BlockSpec((tm, 512), lambda i: (i, 0)),
        pl.BlockSpec((tm, 512), lambda i: (i, 0)),
        pl.BlockSpec((tm, D_MODEL), lambda i: (i, 0)),
        _const_spec((8, D_MODEL)),
        _const_spec((N_BRANCH, D_MODEL)),
        _const_spec((1, 512)), _const_spec((1, 512)),
        _const_spec((A_GROUPS, BLOCK, BLOCK)),
        _const_spec((BLOCK, 512)),
        _const_spec((3, 512)), _const_spec((1, 512)),
        _const_spec((N_BRANCH, BRANCH_W, D_MODEL)),
        _const_spec((D_MODEL, D_MODEL)),
        _const_spec((1, D_MODEL)), _const_spec((1, D_MODEL)),
        _const_spec((D_MODEL, LANE)), _const_spec((1, LANE)),
    ]
    out_specs = [pl.BlockSpec((tm, D_MODEL), lambda i: (i, 0)),
                 pl.BlockSpec((tm * ROW_SLAB, LANE), lambda i: (i, 0)),
                 pl.BlockSpec((tm, LANE), lambda i: (i, 0)),
                 pl.BlockSpec((tm, LANE), lambda i: (i, 0))]
    out_shape = [jax.ShapeDtypeStruct((n, D_MODEL), F32), jax.ShapeDtypeStruct((n * ROW_SLAB, LANE), F32),
                 jax.ShapeDtypeStruct((n, LANE), jnp.int32), jax.ShapeDtypeStruct((n, LANE), F32)]
    return pl.pallas_call(
        kern, grid=(nt,), in_specs=in_specs, out_specs=out_specs, out_shape=out_shape,
        compiler_params=_cparams(("arbitrary",), 56),
        name="merge",
    )(p, p, p, p, p, p, p, p, p, p, br_c, br_d, x2d, mod3, lw['b_gate'], lw['a_ln_g'], lw['a_ln_b'],
      lw['a_ws'], lw['a_bs_full'], lw['b_conv_w'], lw['b_conv_b'], lw['w_br'], lw['w_out'],
      lw['ln1_g'], lw['ln1_b'], lw['w_router'], lw['b_router'])


GU_CHUNKS = 8


def _moe_kernel(be_ref, bc_ref, nx_ref, tokc_ref, tokn_ref, pair_ref, pairp_ref, rw_ref, bgu_ref, bd_ref, wgu_hbm,
                wd_hbm, h2_hbm, out_hbm, wgu_f32, wd_f32, wgu_bf, wd_bf, xbuf, obuf, gsem, ssem, wsem, *, layer):
    i = pl.program_id(0)
    slot = lax.rem(i, 2)
    cnt = bc_ref[i]
    prev_cnt = jnp.where(i > 0, bc_ref[jnp.maximum(i - 1, 0)], 0)
    prev2_cnt = jnp.where(i > 1, bc_ref[jnp.maximum(i - 2, 0)], 0)
    first = jnp.logical_and(i == 0, cnt > 0)
    changed = jnp.logical_or(i == 0, be_ref[i] != be_ref[jnp.maximum(i - 1, 0)])
    pending = prev_cnt == MOE_BLK

    def weight_copies(e):
        return (pltpu.make_async_copy(wgu_hbm.at[layer, e], wgu_f32, wsem.at[0]),
                pltpu.make_async_copy(wd_hbm.at[layer, e], wd_f32, wsem.at[1]))

    def gather_copy(tok, r, sl):
        return pltpu.make_async_copy(
            h2_hbm.at[pl.ds(pl.multiple_of(tok * ROW_SLAB, ROW_SLAB), ROW_SLAB), :],
            xbuf.at[sl, pl.ds(r * BUF_PITCH, ROW_SLAB), :], gsem.at[sl])

    def scatter_copy(pair, r, sl):
        tok = lax.shift_right_logical(pair, 2)
        return pltpu.make_async_copy(
            obuf.at[sl, pl.ds(pl.multiple_of(r * BUF_PITCH, 8), ROW_SLAB), :],
            out_hbm.at[jnp.bitwise_and(pair, TOP_K - 1), pl.ds(pl.multiple_of(tok * ROW_SLAB, ROW_SLAB), ROW_SLAB), :],
            ssem.at[sl])

    def wait_scatter_two_back():
        @pl.when(prev2_cnt == MOE_BLK)
        def _():
            for r in range(MOE_BLK):
                scatter_copy(0, r, slot).wait()

        @pl.when(jnp.logical_and(prev2_cnt > 0, prev2_cnt < MOE_BLK))
        def _():
            def one(r, carry):
                scatter_copy(0, r, slot).wait()
                return carry
            lax.fori_loop(0, prev2_cnt, one, 0)

    def issue_pending_scatter():
        @pl.when(pending)
        def _():
            for r in range(MOE_BLK):
                scatter_copy(pairp_ref[0, 0, r], r, 1 - slot).start(priority=r % 2)

    def block(full, with_pending):
        x = jnp.concatenate([xbuf[slot, pl.ds(j, MOE_BLK, stride=BUF_PITCH), :] for j in range(ROW_SLAB)],
                            axis=1).astype(BF)
        cw = 2 * D_FF // GU_CHUNKS
        per = MOE_BLK // GU_CHUNKS
        gus = []
        for c in range(GU_CHUNKS):
            gus.append(jnp.dot(x, wgu_bf[:, c * cw:(c + 1) * cw], preferred_element_type=F32)
                       + bgu_ref[0, 0, :, c * cw:(c + 1) * cw])
            for r in range(c * per, (c + 1) * per):
                gather_copy(tokn_ref[0, 0, r], r, 1 - slot).start(priority=r % 2)
            if with_pending:
                for r in range(c * per, (c + 1) * per):
                    scatter_copy(pairp_ref[0, 0, r], r, 1 - slot).start(priority=(r + 1) % 2)
        gu = jnp.concatenate(gus, axis=1)
        g = jnp.minimum(gu[:, :D_FF], SWIGLU_LIMIT)
        u = jnp.clip(gu[:, D_FF:], -SWIGLU_LIMIT, SWIGLU_LIMIT)
        act = (u + 1.0) * (g * jax.nn.sigmoid(SWIGLU_ALPHA * g))
        out = (jnp.dot(act.astype(BF), wd_bf[...], preferred_element_type=F32) + bd_ref[0, 0]) * rw_ref[...]
        wait_scatter_two_back()
        for j in range(ROW_SLAB):
            obuf[slot, pl.ds(j, MOE_BLK, stride=BUF_PITCH), :] = out[:, j * LANE:(j + 1) * LANE]
        if not full:
            def one(r, carry):
                scatter_copy(pair_ref[0, 0, r], r, slot).start()
                return carry
            lax.fori_loop(0, cnt, one, 0)

    @pl.when(first)
    def _():
        for r in range(MOE_BLK):
            gather_copy(tokc_ref[0, 0, r], r, 0).start()

    @pl.when(i == 0)
    def _():
        for cp in weight_copies(be_ref[0]):
            cp.start()

    @pl.when(changed)
    def _():
        for cp in weight_copies(be_ref[i]):
            cp.wait()
        wgu_bf[...] = wgu_f32[...].astype(BF)
        wd_bf[...] = wd_f32[...].astype(BF)

        @pl.when(nx_ref[i] >= 0)
        def _():
            for cp in weight_copies(nx_ref[i]):
                cp.start()

    @pl.when(jnp.logical_or(first, prev_cnt > 0))
    def _():
        for r in range(MOE_BLK):
            gather_copy(0, r, slot).wait()

    @pl.when(jnp.logical_and(cnt == MOE_BLK, pending))
    def _():
        block(True, True)

    @pl.when(jnp.logical_and(cnt == MOE_BLK, jnp.logical_not(pending)))
    def _():
        block(True, False)

    @pl.when(jnp.logical_and(cnt > 0, cnt < MOE_BLK))
    def _():
        issue_pending_scatter()
        block(False, False)

    @pl.when(cnt == 0)
    def _():
        issue_pending_scatter()
        wait_scatter_two_back()


def _moe_rows(layer, n_all, blk_expert, blk_cnt, blk_next, row_tok, row_pair, row_w, h2_rows, e_w_gu, e_b_gu, e_w_down,
              e_b_down):
    n_blk = blk_expert.shape[0]
    idx_blk = lambda f: pl.BlockSpec((1, 1, MOE_BLK), f, memory_space=pltpu.SMEM)
    grid_spec = pltpu.PrefetchScalarGridSpec(
        num_scalar_prefetch=3,
        grid=(n_blk,),
        in_specs=[idx_blk(lambda i, be, bc, nx: (i, 0, 0)),
                  idx_blk(lambda i, be, bc, nx: (jnp.minimum(i + 1, n_blk - 1), 0, 0)),
                  idx_blk(lambda i, be, bc, nx: (i, 0, 0)),
                  idx_blk(lambda i, be, bc, nx: (jnp.maximum(i - 1, 0), 0, 0)),
                  pl.BlockSpec((MOE_BLK, 1), lambda i, be, bc, nx: (i, 0)),
                  pl.BlockSpec((1, 1, 1, 2 * D_FF), lambda i, be, bc, nx: (layer, be[i], 0, 0)),
                  pl.BlockSpec((1, 1, 1, D_MODEL), lambda i, be, bc, nx: (layer, be[i], 0, 0)),
                  pl.BlockSpec(memory_space=pl.ANY),
                  pl.BlockSpec(memory_space=pl.ANY),
                  pl.BlockSpec(memory_space=pl.ANY)],
        out_specs=pl.BlockSpec(memory_space=pl.ANY),
        scratch_shapes=[pltpu.VMEM((D_MODEL, 2 * D_FF), F32), pltpu.VMEM((D_FF, D_MODEL), F32),
                        pltpu.VMEM((D_MODEL, 2 * D_FF), BF), pltpu.VMEM((D_FF, D_MODEL), BF),
                        pltpu.VMEM((2, MOE_BLK * BUF_PITCH, LANE), F32), pltpu.VMEM((2, MOE_BLK * BUF_PITCH, LANE), F32),
                        pltpu.SemaphoreType.DMA((2,)), pltpu.SemaphoreType.DMA((2,)), pltpu.SemaphoreType.DMA((2,))],
    )
    tok3 = row_tok.reshape(n_blk, 1, MOE_BLK)
    pair3 = row_pair.reshape(n_blk, 1, MOE_BLK)
    return pl.pallas_call(
        functools.partial(_moe_kernel, layer=layer), grid_spec=grid_spec,
        out_shape=jax.ShapeDtypeStruct((TOP_K, n_all * ROW_SLAB, LANE), F32),
        compiler_params=_cparams(("arbitrary",), 62),
        name="moe_experts",
    )(blk_expert, blk_cnt, blk_next, tok3, tok3, pair3, pair3, row_w, e_b_gu, e_b_down, e_w_gu, e_w_down, h2_rows)


def _route(top_idx, top_w):
    n_tok = top_idx.shape[0]
    n_pairs = n_tok * TOP_K
    e_flat = top_idx.reshape(-1)
    onehot = e_flat[:, None] == jnp.arange(N_EXPERTS, dtype=jnp.int32)[None, :]
    counts = jnp.sum(onehot, axis=0, dtype=jnp.int32)
    padded = (counts + MOE_BLK - 1) // MOE_BLK * MOE_BLK
    starts = jnp.cumsum(counts) - counts
    pends = jnp.cumsum(padded)
    pstarts = pends - padded
    n_rows = (n_pairs + MOE_BLK - 1) // MOE_BLK * MOE_BLK + (N_EXPERTS + 1) * MOE_BLK
    n_blk = n_rows // MOE_BLK
    blk_start = jnp.arange(n_blk, dtype=jnp.int32) * MOE_BLK
    blk_expert = jnp.minimum(jnp.sum(blk_start[:, None] >= pends[None, :], axis=1, dtype=jnp.int32), N_EXPERTS - 1)
    blk_valid = (blk_start < pends[-1]).astype(jnp.int32)
    order = jnp.argsort(e_flat).astype(jnp.int32)
    j_in_blk = jnp.arange(MOE_BLK, dtype=jnp.int32)[None, :]
    j_in_e = (blk_start - pstarts[blk_expert])[:, None] + j_in_blk
    row_valid = (j_in_e < counts[blk_expert][:, None]) & (blk_valid[:, None] > 0)
    src = jnp.clip(starts[blk_expert][:, None] + j_in_e, 0, n_pairs - 1).reshape(-1)
    row_pair = order[src]
    row_valid = row_valid.reshape(-1)
    row_tok = jnp.where(row_valid, row_pair // TOP_K, 0)
    row_w = jnp.where(row_valid, top_w.reshape(-1)[row_pair], 0.0)
    row_pair = jnp.where(row_valid, row_pair, 0)
    blk_cnt = jnp.clip(counts[blk_expert] - (blk_start - pstarts[blk_expert]), 0, MOE_BLK) * blk_valid
    later = jnp.arange(n_blk)[None, :] > jnp.arange(n_blk)[:, None]
    differs = later & (blk_expert[None, :] != blk_expert[:, None])
    blk_next = jnp.where(jnp.any(differs, axis=1), blk_expert[jnp.argmax(differs, axis=1)], -1).astype(jnp.int32)
    return row_tok, row_w, row_pair, blk_expert, blk_cnt, blk_next


def _combine_kernel(x1_ref, r_ref, g2_ref, lng_ref, lnb_ref, o_ref, *, tm):
    f = None
    for k in range(TOP_K):
        fk = jnp.concatenate([r_ref[k, pl.ds(j, tm, stride=ROW_SLAB), :] for j in range(ROW_SLAB)], axis=1)
        f = fk if f is None else f + fk
    r = DN_ALPHA * x1_ref[...] + g2_ref[...] * f
    o_ref[...] = _layer_norm_rows(r, lng_ref[...], lnb_ref[...])


def _combine(x1, rows, g2, ln_g, ln_b, *, tm, row_off=0):
    n = x1.shape[0]
    assert row_off % tm == 0
    blk_off = row_off // tm
    return pl.pallas_call(
        functools.partial(_combine_kernel, tm=tm), grid=(n // tm,),
        in_specs=[pl.BlockSpec((tm, D_MODEL), lambda i: (i, 0)),
                  pl.BlockSpec((TOP_K, tm * ROW_SLAB, LANE), lambda i: (0, i + blk_off, 0)),
                  pl.BlockSpec((1, D_MODEL), lambda i: (0, 0)),
                  pl.BlockSpec((1, D_MODEL), lambda i: (0, 0)),
                  pl.BlockSpec((1, D_MODEL), lambda i: (0, 0))],
        out_specs=pl.BlockSpec((tm, D_MODEL), lambda i: (i, 0)),
        out_shape=jax.ShapeDtypeStruct((n, D_MODEL), F32),
        compiler_params=_cparams(("arbitrary",), 48),
        name="combine_ln2",
    )(x1, rows, g2, ln_g, ln_b)


def _rope_tables(n_tok):
    n_freq = ROPE_DIM // 4
    inv = (np.float32(ROPE_BASE) ** (-np.arange(n_freq, dtype=np.float32) / np.float32(n_freq))).astype(np.float32)
    t = np.arange(n_tok)
    ang_r = (t // GRID_W).astype(np.float32)[:, None] * inv
    ang_c = (t % GRID_W).astype(np.float32)[:, None] * inv
    cos64 = np.concatenate([np.cos(ang_r), np.cos(ang_r), np.cos(ang_c), np.cos(ang_c)], axis=1)
    sin64 = np.concatenate([-np.sin(ang_r), np.sin(ang_r), -np.sin(ang_c), np.sin(ang_c)], axis=1)
    cos_t = np.concatenate([cos64, cos64], axis=1).astype(np.float32)
    sin_t = np.concatenate([sin64, sin64], axis=1).astype(np.float32)
    return jnp.asarray(cos_t), jnp.asarray(sin_t)


def _row_tile(n, pref):
    return pref if n % pref == 0 else n


def kernel(x, c, ctx, c_ctx, w_ada, b_ada, w_in, b_gate, a_ln_g, a_ln_b, a_ws, a_bs, b_conv_w, b_conv_b, c_lq1, c_lk1, c_lq2, c_lk2, c_subln_g, d_sink, w_br, w_out, ln1_g, ln1_b, w_router, b_router, e_w_gu, e_b_gu, e_w_down, e_b_down, ln2_g, ln2_b):
    assert x.shape[0] == 1 and ctx.shape[0] == 1
    n, m = x.shape[1], ctx.shape[1]
    assert n % 256 == 0 and m % BLOCK == 0
    xl = x[0]
    xc = ctx[0]

    c_rows = jnp.zeros((8, D_MODEL), F32).at[0].set(c[0]).at[1].set(c_ctx)
    mods = _ada(c_rows, w_ada, b_ada)
    cos_t, sin_t = _rope_tables(n)
    cos_c = jnp.ones((m, LANE), F32)
    sin_c = jnp.zeros((m, LANE), F32)
    row2 = lambda v: v.reshape(1, -1)

    for l in range(DEPTH):
        need_ctx = l < DEPTH - 1
        lam_init = 0.8 - 0.6 * math.exp(-0.3 * l)
        sh1, sc1, g1, sh2, sc2, g2 = [row2(t) for t in jnp.split(mods[l, 0], 6)]
        csh1, csc1, cg1, csh2, csc2, cg2 = [row2(t) for t in jnp.split(mods[l, 1], 6)]
        lam = (jnp.exp(jnp.sum(c_lq1[l] * c_lk1[l])) - jnp.exp(jnp.sum(c_lq2[l] * c_lk2[l]))
               + lam_init).astype(F32).reshape(1)
        lw = {
            'b_gate': b_gate[l], 'a_ln_g': row2(a_ln_g[l]), 'a_ln_b': row2(a_ln_b[l]),
            'a_ws': a_ws[l].astype(BF),
            'a_bs_full': jnp.repeat(a_bs[l].T, BLOCK, axis=1),
            'b_conv_w': b_conv_w[l], 'b_conv_b': row2(b_conv_b[l]),
            'w_br': w_br[l].astype(BF), 'w_out': w_out[l].astype(BF),
            'ln1_g': row2(ln1_g[l]), 'ln1_b': row2(ln1_b[l]),
            'w_router': jnp.pad(w_router[l], ((0, 0), (0, LANE - N_EXPERTS))).astype(BF),
            'b_router': jnp.pad(row2(b_router[l]), ((0, 0), (0, LANE - N_EXPERTS)), constant_values=NEG_BIG),
        }

        p_lat = _proj(l, xl, sc1, sh1, w_in, cos_t, sin_t, rope=True, tm=_row_tile(n, 2048))
        p_ctx = _proj(l, xc, csc1, csh1, w_in, cos_c, sin_c, rope=False, tm=m)

        sl = lambda arr, col, w: lax.slice_in_dim(arr, col, col + w, axis=1)
        k_all = jnp.concatenate([sl(p_lat, COL_CK, 512), sl(p_ctx, COL_CK, 512)], axis=0)
        v_all = jnp.concatenate([sl(p_lat, COL_CV, 512), sl(p_ctx, COL_CV, 512)], axis=0)
        br_c = _diff_attn(lam, sl(p_lat, COL_CQ, 512), k_all, v_all, c_subln_g[l], lam_init)
        br_d = _win_attn(d_sink[l], p_lat, p_ctx)

        mod3 = jnp.zeros((8, D_MODEL), F32).at[0].set(g1[0]).at[1].set(sc2[0]).at[2].set(sh2[0])
        x1, h2, tidx, tw = _merge(p_lat, br_c, br_d, xl, mod3, lw, tm=256)
        tidx, tw = tidx[:, :TOP_K], tw[:, :TOP_K]

        if need_ctx:
            cbr_c = _diff_attn(lam, sl(p_ctx, COL_CQ, 512), sl(p_ctx, COL_CK, 512), sl(p_ctx, COL_CV, 512),
                               c_subln_g[l], lam_init)
            cbr_d = _win_attn_ctx(d_sink[l], p_ctx)
            cmod3 = jnp.zeros((8, D_MODEL), F32).at[0].set(cg1[0]).at[1].set(csc2[0]).at[2].set(csh2[0])
            xc1, hc2, ctidx, ctw = _merge(p_ctx, cbr_c, cbr_d, xc, cmod3, lw, tm=_row_tile(m, 256))
            h2 = jnp.concatenate([h2, hc2], axis=0)
            tidx = jnp.concatenate([tidx, ctidx[:, :TOP_K]], axis=0)
            tw = jnp.concatenate([tw, ctw[:, :TOP_K]], axis=0)

        n_all = h2.shape[0] // ROW_SLAB
        row_tok, row_w, row_pair, blk_expert, blk_cnt, blk_next = _route(tidx, tw)
        rows_out = _moe_rows(l, n_all, blk_expert, blk_cnt, blk_next, row_tok, row_pair, row_w.reshape(-1, 1), h2,
                             e_w_gu, e_b_gu.reshape(DEPTH, N_EXPERTS, 1, -1),
                             e_w_down, e_b_down.reshape(DEPTH, N_EXPERTS, 1, -1))

        xl = _combine(x1, rows_out, g2, row2(ln2_g[l]), row2(ln2_b[l]), tm=256)
        if need_ctx:
            xc = _combine(xc1, rows_out, cg2, row2(ln2_g[l]), row2(ln2_b[l]), tm=_row_tile(m, 256), row_off=n)

    return xl[None]
```

```python
import functools
import math

import jax
import jax.numpy as jnp
import numpy as np
from jax import lax
from jax.experimental import pallas as pl
from jax.experimental.pallas import tpu as pltpu

BF = jnp.bfloat16
F32 = jnp.float32

D_MODEL = 2048
DEPTH = 2
GRID_W = 64
BLOCK = 128
A_GROUPS = 4
C_HEADS = 4
C_HD = 64
C_VD = 128
D_HEADS = 8
D_KV_HEADS = 2
D_HD = 64
WINDOW = 128
N_BRANCH = 4
BRANCH_W = 512
PROJ_SIZES = (512, 512, 512, 512, 512, 512, 512, 512, 512, 128, 128, N_BRANCH * D_MODEL)
N_EXPERTS = 32
TOP_K = 4
D_FF = 1024
SWIGLU_LIMIT = 7.0
SWIGLU_ALPHA = 1.702
ROPE_DIM = 64
ROPE_BASE = 10000.0
LN_EPS = 1e-5
DN_ALPHA = (2 * DEPTH) ** 0.25

LANE = 128
NEG_BIG = -1e30

GL_W = N_BRANCH * D_MODEL
PROJ_TN = 256
SEG_W = 9 * 512 + 256
P_W = GL_W + SEG_W
N_SEG_TILES = SEG_W // PROJ_TN
COL_AU, COL_AV, COL_BX, COL_BB, COL_BC, COL_CQ, COL_CK, COL_CV, COL_DQ = [GL_W + 512 * s for s in range(9)]
COL_DK = GL_W + 9 * 512
COL_DV = COL_DK + 128
ROPE_FULL_TILES = (10, 11, 12, 13, 16, 17)
ROPE_PART_TILE = 18

MOE_BLK = 256
ROW_SLAB = D_MODEL // LANE
BUF_PITCH = 24


def _cparams(dims, vmem_mib):
    return pltpu.CompilerParams(dimension_semantics=dims, vmem_limit_bytes=vmem_mib * 1024 * 1024)


def _const_spec(shape):
    nd = len(shape)
    return pl.BlockSpec(shape, lambda *_: (0,) * nd, pipeline_mode=pl.Buffered(1))


def _layer_norm_rows(r, g, b):
    mu = jnp.mean(r, axis=-1, keepdims=True)
    d = r - mu
    var = jnp.mean(d * d, axis=-1, keepdims=True)
    return d * lax.rsqrt(var + LN_EPS) * g + b


def _ada_kernel(c_ref, w_ref, b_ref, o_ref):
    cs = c_ref[...]
    s = cs * jax.nn.sigmoid(cs)
    o_ref[0] = jnp.dot(s.astype(BF), w_ref[0].astype(BF), preferred_element_type=F32) + b_ref[0]


def _ada(c_rows, w_ada, b_ada):
    n_l, _, n_out = w_ada.shape
    tn = 1536
    return pl.pallas_call(
        _ada_kernel,
        grid=(n_l, n_out // tn),
        in_specs=[pl.BlockSpec((8, D_MODEL), lambda l, j: (0, 0)),
                  pl.BlockSpec((1, D_MODEL, tn), lambda l, j: (l, 0, j)),
                  pl.BlockSpec((1, 1, tn), lambda l, j: (l, 0, j))],
        out_specs=pl.BlockSpec((1, 8, tn), lambda l, j: (l, 0, j)),
        out_shape=jax.ShapeDtypeStruct((n_l, 8, n_out), F32),
        compiler_params=_cparams(("arbitrary", "arbitrary"), 40),
        name="ada",
    )(c_rows, w_ada, b_ada.reshape(n_l, 1, n_out))


def _rope_rotate(a, cos_ref, sin_ref):
    w = a.shape[1]
    lane = lax.broadcasted_iota(jnp.int32, a.shape, 1)
    first = jnp.bitwise_and(lane, 16) == 0
    swapped = jnp.where(first, pltpu.roll(a, w - 16, 1), pltpu.roll(a, 16, 1))
    reps = w // LANE
    cos = cos_ref[...]
    sin = sin_ref[...]
    if reps > 1:
        cos = jnp.concatenate([cos] * reps, axis=1)
        sin = jnp.concatenate([sin] * reps, axis=1)
    return a * cos + swapped * sin


def _proj_kernel(x_ref, sc_ref, sh_ref, w_ref, cos_ref, sin_ref, o_ref, h_scr, *, rope):
    j = pl.program_id(1)

    @pl.when(j == 0)
    def _():
        h_scr[...] = (x_ref[...] * (1.0 + sc_ref[...]) + sh_ref[...]).astype(BF)

    acc = jnp.dot(h_scr[...], w_ref[0].astype(BF), preferred_element_type=F32)
    if not rope:
        o_ref[...] = acc.astype(BF)
        return

    full = functools.reduce(jnp.logical_or, [j == t for t in ROPE_FULL_TILES])
    part = j == ROPE_PART_TILE

    @pl.when(full)
    def _():
        o_ref[...] = _rope_rotate(acc, cos_ref, sin_ref).astype(BF)

    @pl.when(part)
    def _():
        o_ref[:, :LANE] = _rope_rotate(acc[:, :LANE], cos_ref, sin_ref).astype(BF)
        o_ref[:, LANE:] = acc[:, LANE:].astype(BF)

    @pl.when(jnp.logical_not(jnp.logical_or(full, part)))
    def _():
        o_ref[...] = acc.astype(BF)


def _proj(layer, x2d, sc, sh, w_in, cos_t, sin_t, *, rope, tm):
    n = x2d.shape[0]
    kern = functools.partial(_proj_kernel, rope=rope)
    n_gl_tiles = GL_W // PROJ_TN
    out_tile = lambda j: jnp.where(j < N_SEG_TILES, j + n_gl_tiles, j - N_SEG_TILES)
    return pl.pallas_call(
        kern,
        grid=(n // tm, P_W // PROJ_TN),
        in_specs=[pl.BlockSpec((tm, D_MODEL), lambda i, j: (i, 0), pipeline_mode=pl.Buffered(1)),
                  pl.BlockSpec((1, D_MODEL), lambda i, j: (0, 0)),
                  pl.BlockSpec((1, D_MODEL), lambda i, j: (0, 0)),
                  pl.BlockSpec((1, D_MODEL, PROJ_TN), lambda i, j: (layer, 0, j)),
                  pl.BlockSpec((tm, LANE), lambda i, j: (i, 0)),
                  pl.BlockSpec((tm, LANE), lambda i, j: (i, 0))],
        out_specs=pl.BlockSpec((tm, PROJ_TN), lambda i, j: (i, out_tile(j))),
        out_shape=jax.ShapeDtypeStruct((n, P_W), BF),
        scratch_shapes=[pltpu.VMEM((tm, D_MODEL), BF)],
        compiler_params=_cparams(("arbitrary", "arbitrary"), 56),
        name="proj_rope" if rope else "proj_ctx",
    )(x2d, sc, sh, w_in, cos_t, sin_t)


_ACC_ROWS = 32
_ONES_ROWS = 16
LOG2E = math.log2(math.e)


def _diff_attn_kernel(lam_ref, qt_ref, qtn_ref, k_ref, vt_ref, g_ref, o_ref, qbd_scr, qbdn_scr, m_scr, acc_scr,
                      s0_scr, s_scr, mx0_scr, mx_scr, *, tq, tk, sub, n_chunks, post_scale):
    w = 2 * tq
    n_sub = tk // sub
    i = pl.program_id(1)

    def block_diag(q_ref, dst):
        qt = q_ref[...].astype(F32) * (C_HD ** -0.5 * LOG2E)
        row = lax.broadcasted_iota(jnp.int32, qt.shape, 0)
        dst[...] = jnp.concatenate([jnp.where(row < C_HD, qt, 0.0), jnp.where(row >= C_HD, qt, 0.0)],
                                   axis=1).astype(BF)

    def scores_sub(c, qbd, s_dst, j):
        k = k_ref[0, pl.ds(pl.multiple_of(c * tk + j * sub, sub), sub), :]
        s = jnp.dot(k, qbd[...], preferred_element_type=F32)
        s_dst[j * sub:(j + 1) * sub, :] = s
        return jnp.max(s.reshape(sub // _ACC_ROWS, _ACC_ROWS, w), axis=0)

    def step(c, s_src, mx_src, nxt, qbd, s_dst, mx_dst):
        m_old = m_scr[...]
        m_new = jnp.maximum(m_old, jnp.max(mx_src[...], axis=0, keepdims=True))
        alpha = jnp.exp2(m_old - m_new)
        pv = None
        mx = None
        for j in range(n_sub):
            p = jnp.exp2(s_src[j * sub:(j + 1) * sub, :] - m_new).astype(BF)
            mj = scores_sub(nxt, qbd, s_dst, j)
            mx = mj if mx is None else jnp.maximum(mx, mj)
            d = jnp.dot(vt_ref[0, c, :, j * sub:(j + 1) * sub], p, preferred_element_type=F32)
            pv = d if pv is None else pv + d
        acc_scr[...] = acc_scr[...] * alpha + pv
        m_scr[...] = m_new
        mx_dst[...] = mx

    @pl.when(i == 0)
    def _():
        block_diag(qt_ref, qbd_scr)
        mx = None
        for j in range(n_sub):
            mj = scores_sub(0, qbd_scr, s0_scr, j)
            mx = mj if mx is None else jnp.maximum(mx, mj)
        mx0_scr[...] = mx

    block_diag(qtn_ref, qbdn_scr)
    m_scr[...] = jnp.full(m_scr.shape, -jnp.inf, F32)
    acc_scr[...] = jnp.zeros(acc_scr.shape, F32)

    slot_s = lambda c: s_scr.at[c % 2]
    slot_mx = lambda c: mx_scr.at[c % 2]
    last = n_chunks - 1

    def static_step(c):
        src = (s0_scr, mx0_scr) if c == 0 else (slot_s(c), slot_mx(c))
        if c == last:
            step(c, src[0], src[1], 0, qbdn_scr, s0_scr, mx0_scr)
        else:
            step(c, src[0], src[1], c + 1, qbd_scr, slot_s(c + 1), slot_mx(c + 1))

    static_step(0)
    n_pairs = max((n_chunks - 3) // 2, 0)

    def pair(jp, carry):
        c = 1 + 2 * jp
        step(c, slot_s(1), slot_mx(1), c + 1, qbd_scr, slot_s(0), slot_mx(0))
        step(c + 1, slot_s(0), slot_mx(0), c + 2, qbd_scr, slot_s(1), slot_mx(1))
        return carry

    if n_pairs > 0:
        lax.fori_loop(0, n_pairs, pair, 0)
    for c in range(1 + 2 * n_pairs, n_chunks):
        static_step(c)
    qbd_scr[...] = qbdn_scr[...]

    o = acc_scr[:C_VD, :] / acc_scr[C_VD:C_VD + 1, :]
    od = o[:, :tq] - lam_ref[0] * o[:, tq:]
    ms = jnp.mean(od * od, axis=0, keepdims=True)
    on = od * lax.rsqrt(ms + LN_EPS) * g_ref[...] * post_scale
    o_ref[...] = on.T.astype(BF)


def _pick_tk(n_k):
    for tk in (1280, 1024, 768, 512, 256, 128):
        if n_k % tk == 0:
            return tk
    raise ValueError(f"unsupported key count {n_k}")


def _diff_attn(lam, q, k_all, v_all, subln_g, lam_init):
    n = q.shape[0]
    n_k = k_all.shape[0]
    tq = 256
    tk = _pick_tk(n_k)
    n_chunks = n_k // tk
    qt = q.T
    kh = k_all.reshape(n_k, C_HEADS, C_VD).transpose(1, 0, 2)
    vt = v_all.reshape(n_chunks, tk, C_HEADS, C_VD).transpose(2, 0, 3, 1)
    vt = jnp.concatenate([vt, jnp.ones((C_HEADS, n_chunks, _ONES_ROWS, tk), BF)], axis=2)
    sub = 256 if tk % 256 == 0 else tk
    kern = functools.partial(_diff_attn_kernel, tq=tq, tk=tk, sub=sub, n_chunks=n_chunks,
                             post_scale=1.0 - lam_init)
    return pl.pallas_call(
        kern,
        grid=(C_HEADS, n // tq),
        in_specs=[pl.BlockSpec(memory_space=pltpu.SMEM),
                  pl.BlockSpec((C_VD, tq), lambda h, i: (h, i)),
                  pl.BlockSpec((C_VD, tq), lambda h, i: (h, jnp.minimum(i + 1, n // tq - 1))),
                  pl.BlockSpec((1, n_k, C_VD), lambda h, i: (h, 0, 0)),
                  pl.BlockSpec((1, n_chunks, C_VD + _ONES_ROWS, tk), lambda h, i: (h, 0, 0, 0)),
                  pl.BlockSpec((C_VD, 1), lambda h, i: (0, 0))],
        out_specs=pl.BlockSpec((tq, C_VD), lambda h, i: (i, h)),
        out_shape=jax.ShapeDtypeStruct((n, C_HEADS * C_VD), BF),
        scratch_shapes=[pltpu.VMEM((C_VD, 2 * tq), BF), pltpu.VMEM((C_VD, 2 * tq), BF),
                        pltpu.VMEM((1, 2 * tq), F32), pltpu.VMEM((C_VD + _ONES_ROWS, 2 * tq), F32),
                        pltpu.VMEM((tk, 2 * tq), F32), pltpu.VMEM((2, tk, 2 * tq), F32),
                        pltpu.VMEM((_ACC_ROWS, 2 * tq), F32), pltpu.VMEM((2, _ACC_ROWS, 2 * tq), F32)],
        compiler_params=_cparams(("arbitrary", "arbitrary"), 48),
        name="diff_attn",
    )(lam, qt, qt, kh, vt, subln_g.reshape(C_VD, 1))


def _win_heads(q, kb, vb, valid, sink_ref):
    outs = []
    grp = D_HEADS // D_KV_HEADS
    for h in range(D_HEADS):
        kh = h // grp
        qh = q[:, h * D_HD:(h + 1) * D_HD]
        k_h = kb[:, kh * D_HD:(kh + 1) * D_HD]
        v_h = vb[:, kh * D_HD:(kh + 1) * D_HD]
        s = lax.dot_general(qh, k_h, (((1,), (1,)), ((), ())), preferred_element_type=F32) * (D_HD ** -0.5)
        if valid is not None:
            s = jnp.where(valid, s, NEG_BIG)
        sk = sink_ref[h]
        m = jnp.maximum(jnp.max(s, axis=-1, keepdims=True), sk)
        e = jnp.exp(s - m)
        l = jnp.sum(e, axis=-1, keepdims=True) + jnp.exp(sk - m)
        p = (e / l).astype(BF)
        outs.append(jnp.dot(p, v_h, preferred_element_type=F32))
    return jnp.concatenate(outs, axis=1).astype(BF)


def _win_attn_ctx_kernel(sink_ref, q_ref, kx_ref, vx_ref, o_ref):
    o_ref[...] = _win_heads(q_ref[...], kx_ref[...], vx_ref[...], None, sink_ref)


def _win_attn_t_kernel(sink_ref, qt_ref, kp_ref, kc_ref, kn_ref, kx_ref, vtp_ref, vtc_ref, vtn_ref, vtx_ref, o_ref, *,
                       n_tok, n_ctx):
    i = pl.program_id(0)
    grp = D_HEADS // D_KV_HEADS
    kfull = jnp.concatenate([kp_ref[...], kc_ref[...], kn_ref[...], kx_ref[...]], axis=0)
    vtfull = jnp.concatenate([vtp_ref[...], vtc_ref[...], vtn_ref[...], vtx_ref[...]], axis=1)
    n_keys = 3 * BLOCK + n_ctx
    kk = lax.broadcasted_iota(jnp.int32, (n_keys, BLOCK), 0)
    qq = lax.broadcasted_iota(jnp.int32, (n_keys, BLOCK), 1)
    kpos = (i - 1) * BLOCK + kk
    in_band = (jnp.abs(kk - BLOCK - qq) <= WINDOW) & (kpos >= 0) & (kpos < n_tok)
    bias = jnp.where((kk >= 3 * BLOCK) | in_band, 0.0, NEG_BIG)
    bias = jnp.concatenate([bias] * grp, axis=1)
    qt = (qt_ref[...].astype(F32) * (D_HD ** -0.5)).astype(BF)
    zeros = jnp.zeros((D_HD, grp * BLOCK), BF)
    for g in range(D_KV_HEADS):
        q64 = jnp.concatenate([qt[(g * grp + hh) * D_HD:(g * grp + hh + 1) * D_HD, :] for hh in range(grp)], axis=1)
        qbd = jnp.concatenate([q64, zeros] if g == 0 else [zeros, q64], axis=0)
        s = jnp.dot(kfull, qbd, preferred_element_type=F32) + bias
        sk = sink_ref[g]
        m = jnp.maximum(jnp.max(s, axis=0, keepdims=True), sk)
        e = jnp.exp(s - m)
        l = jnp.sum(e, axis=0, keepdims=True) + jnp.exp(sk - m)
        p = (e * (1.0 / l)).astype(BF)
        ot = jnp.dot(vtfull, p, preferred_element_type=F32)
        og = ot[g * D_HD:(g + 1) * D_HD, :]
        for hh in range(grp):
            h = g * grp + hh
            o_ref[h * D_HD:(h + 1) * D_HD, :] = og[:, hh * BLOCK:(hh + 1) * BLOCK].astype(BF)


def _win_attn(sink, p_lat, p_ctx):
    n = p_lat.shape[0]
    m = p_ctx.shape[0]
    nb = n // BLOCK
    grp = D_HEADS // D_KV_HEADS
    ck = COL_DK // LANE
    qt = lax.slice_in_dim(p_lat, COL_DQ, COL_DQ + 512, axis=1).T
    vt = lax.slice_in_dim(p_lat, COL_DV, COL_DV + LANE, axis=1).T
    vtx = lax.slice_in_dim(p_ctx, COL_DV, COL_DV + LANE, axis=1).T
    sink_rows = jnp.repeat(sink.astype(F32).reshape(D_KV_HEADS, grp), BLOCK, axis=1).reshape(D_KV_HEADS, 1, grp * BLOCK)
    kern = functools.partial(_win_attn_t_kernel, n_tok=n, n_ctx=m)
    prev = lambda i: jnp.maximum(i - 1, 0)
    nxt = lambda i: jnp.minimum(i + 1, nb - 1)
    o_t = pl.pallas_call(
        kern,
        grid=(nb,),
        in_specs=[pl.BlockSpec((D_KV_HEADS, 1, grp * BLOCK), lambda i: (0, 0, 0)),
                  pl.BlockSpec((512, BLOCK), lambda i: (0, i)),
                  pl.BlockSpec((BLOCK, LANE), lambda i: (prev(i), ck)),
                  pl.BlockSpec((BLOCK, LANE), lambda i: (i, ck)),
                  pl.BlockSpec((BLOCK, LANE), lambda i: (nxt(i), ck)),
                  pl.BlockSpec((m, LANE), lambda i: (0, ck)),
                  pl.BlockSpec((LANE, BLOCK), lambda i: (0, prev(i))),
                  pl.BlockSpec((LANE, BLOCK), lambda i: (0, i)),
                  pl.BlockSpec((LANE, BLOCK), lambda i: (0, nxt(i))),
                  pl.BlockSpec((LANE, m), lambda i: (0, 0))],
        out_specs=pl.BlockSpec((512, BLOCK), lambda i: (0, i)),
        out_shape=jax.ShapeDtypeStruct((512, n), BF),
        compiler_params=_cparams(("arbitrary",), 32),
        name="win_attn",
    )(sink_rows, qt, p_lat, p_lat, p_lat, p_ctx, vt, vt, vt, vtx)
    return o_t.T


def _win_attn_ctx(sink, p_ctx):
    m = p_ctx.shape[0]
    ck, cv, cq = COL_DK // LANE, COL_DV // LANE, COL_DQ // 512
    return pl.pallas_call(
        _win_attn_ctx_kernel,
        grid=(m // BLOCK,),
        in_specs=[pl.BlockSpec(memory_space=pltpu.SMEM),
                  pl.BlockSpec((BLOCK, 512), lambda i: (i, cq)),
                  pl.BlockSpec((m, LANE), lambda i: (0, ck)),
                  pl.BlockSpec((m, LANE), lambda i: (0, cv))],
        out_specs=pl.BlockSpec((BLOCK, 512), lambda i: (i, 0)),
        out_shape=jax.ShapeDtypeStruct((m, 512), BF),
        compiler_params=_cparams(("arbitrary",), 32),
        name="win_attn_ctx",
    )(sink, p_ctx, p_ctx, p_ctx)


def _merge_kernel(gl_ref, au_ref, av_ref, bx_ref, bb_ref, bc_ref, bxp_ref, bcp_ref, bxn_ref, bcn_ref,
                  brc_ref, brd_ref, x_ref, mod_ref, bgate_ref, alng_ref, alnb_ref, ws_ref, bs_ref, cw_ref, cb_ref,
                  wbr_ref, wout_ref, ln1g_ref, ln1b_ref, wr_ref, brt_ref,
                  x1_ref, h2_ref, tidx_ref, tw_ref, *, tm):
    i = pl.program_id(0)
    last = pl.num_programs(0) - 1

    u = jax.nn.gelu(au_ref[...].astype(F32), approximate=True)
    v = jax.nn.gelu(av_ref[...].astype(F32), approximate=True)
    vn = _layer_norm_rows(v, alng_ref[...], alnb_ref[...]).astype(BF)
    blocks = []
    for b in range(tm // BLOCK):
        cols = []
        for g in range(A_GROUPS):
            vbg = vn[b * BLOCK:(b + 1) * BLOCK, g * LANE:(g + 1) * LANE]
            cols.append(jnp.dot(ws_ref[g], vbg, preferred_element_type=F32))
        blocks.append(jnp.concatenate(cols, axis=1) + bs_ref[...])
    mixed = jnp.concatenate(blocks, axis=0) if len(blocks) > 1 else blocks[0]
    br_a = u * mixed

    z = bc_ref[...].astype(F32) * bx_ref[...].astype(F32)
    z_prev = bcp_ref[7:8, :].astype(F32) * bxp_ref[7:8, :].astype(F32) * (i > 0).astype(F32)
    z_next = bcn_ref[0:1, :].astype(F32) * bxn_ref[0:1, :].astype(F32) * (i < last).astype(F32)
    row = lax.broadcasted_iota(jnp.int32, z.shape, 0)
    z_up = jnp.where(row == 0, z_prev, pltpu.roll(z, 1, 0))
    z_dn = jnp.where(row == tm - 1, z_next, pltpu.roll(z, tm - 1, 0))
    y_conv = cw_ref[0:1, :] * z_up + cw_ref[1:2, :] * z + cw_ref[2:3, :] * z_dn + cb_ref[...]
    br_b = bb_ref[...].astype(F32) * y_conv

    branches = (br_a.astype(BF), br_b.astype(BF), brc_ref[...], brd_ref[...])
    merged = None
    for g in range(N_BRANCH):
        pr = jnp.dot(branches[g], wbr_ref[g], preferred_element_type=F32)
        gate = jax.nn.sigmoid(gl_ref[:, g * D_MODEL:(g + 1) * D_MODEL].astype(F32) + bgate_ref[g:g + 1, :])
        merged = gate * pr if merged is None else merged + gate * pr
    y = jnp.dot(merged.astype(BF), wout_ref[...], preferred_element_type=F32)

    r = DN_ALPHA * x_ref[...] + mod_ref[0:1, :] * y
    x1 = _layer_norm_rows(r, ln1g_ref[...], ln1b_ref[...])
    x1_ref[...] = x1
    h2f = x1 * (1.0 + mod_ref[1:2, :]) + mod_ref[2:3, :]
    for j in range(ROW_SLAB):
        h2_ref[pl.ds(j, tm, stride=ROW_SLAB), :] = h2f[:, j * LANE:(j + 1) * LANE]
    h2 = h2f.astype(BF)
    logits = jnp.dot(h2, wr_ref[...], preferred_element_type=F32) + brt_ref[...]

    lane = lax.broadcasted_iota(jnp.int32, logits.shape, 1)
    vals, idxs = [], []
    cur = logits
    for _ in range(TOP_K):
        mx = jnp.max(cur, axis=-1, keepdims=True)
        ix = jnp.min(jnp.where(cur == mx, lane, LANE), axis=-1, keepdims=True)
        vals.append(mx)
        idxs.append(ix)
        cur = jnp.where(lane == ix, -jnp.inf, cur)
    es = [jnp.exp(vk - vals[0]) for vk in vals]
    den = es[0] + es[1] + es[2] + es[3]
    tidx = jnp.zeros(logits.shape, jnp.int32)
    tw = jnp.zeros(logits.shape, F32)
    for k in range(TOP_K):
        tidx = jnp.where(lane == k, idxs[k], tidx)
        tw = jnp.where(lane == k, es[k] / den, tw)
    tidx_ref[...] = tidx
    tw_ref[...] = tw


def _merge(p, br_c, br_d, x2d, mod3, lw, *, tm):
    n = p.shape[0]
    nt = n // tm
    r8 = tm // 8
    c512 = lambda col: col // 512
    kern = functools.partial(_merge_kernel, tm=tm)
    prev8 = lambda i: (jnp.maximum(i * r8 - 1, 0))
    next8 = lambda i: (jnp.minimum((i + 1) * r8, n // 8 - 1))
    seg = lambda col: pl.BlockSpec((tm, 512), lambda i: (i, c512(col)))
    in_specs = [
        pl.BlockSpec((tm, GL_W), lambda i: (i, 0)),
        seg(COL_AU), seg(COL_AV), seg(COL_BX), seg(COL_BB), seg(COL_BC),
        pl.BlockSpec((8, 512), lambda i: (prev8(i), c512(COL_BX))),
        pl.BlockSpec((8, 512), lambda i: (prev8(i), c512(COL_BC))),
        pl.BlockSpec((8, 512), lambda i: (next8(i), c512(COL_BX))),
        pl.BlockSpec((8, 512), lambda i: (next8(i), c512(COL_BC))),
        pl.BlockSpec((tm, 512), lambda i: (i, 0)),
        pl.BlockSpec((tm, 512), lambda i: (i, 0)),
        pl.BlockSpec((tm, D_MODEL), lambda i: (i, 0)),
        _const_spec((8, D_MODEL)),
        _const_spec((N_BRANCH, D_MODEL)),
        _const_spec((1, 512)), _const_spec((1, 512)),
        _const_spec((A_GROUPS, BLOCK, BLOCK)),
        _const_spec((BLOCK, 512)),
        _const_spec((3, 512)), _const_spec((1, 512)),
        _const_spec((N_BRANCH, BRANCH_W, D_MODEL)),
        _const_spec((D_MODEL, D_MODEL)),
        _const_spec((1, D_MODEL)), _const_spec((1, D_MODEL)),
        _const_spec((D_MODEL, LANE)), _const_spec((1, LANE)),
    ]
    out_specs = [pl.BlockSpec((tm, D_MODEL), lambda i: (i, 0)),
                 pl.BlockSpec((tm * ROW_SLAB, LANE), lambda i: (i, 0)),
                 pl.BlockSpec((tm, LANE), lambda i: (i, 0)),
                 pl.BlockSpec((tm, LANE), lambda i: (i, 0))]
    out_shape = [jax.ShapeDtypeStruct((n, D_MODEL), F32), jax.ShapeDtypeStruct((n * ROW_SLAB, LANE), F32),
                 jax.ShapeDtypeStruct((n, LANE), jnp.int32), jax.ShapeDtypeStruct((n, LANE), F32)]
    return pl.pallas_call(
        kern, grid=(nt,), in_specs=in_specs, out_specs=out_specs, out_shape=out_shape,
        compiler_params=_cparams(("arbitrary",), 56),
        name="merge",
    )(p, p, p, p, p, p, p, p, p, p, br_c, br_d, x2d, mod3, lw['b_gate'], lw['a_ln_g'], lw['a_ln_b'],
      lw['a_ws'], lw['a_bs_full'], lw['b_conv_w'], lw['b_conv_b'], lw['w_br'], lw['w_out'],
      lw['ln1_g'], lw['ln1_b'], lw['w_router'], lw['b_router'])


GU_CHUNKS = 8


def _moe_kernel(be_ref, bc_ref, nx_ref, tokc_ref, tokn_ref, pair_ref, rw_ref, bgu_ref, bd_ref, wgu_hbm, wd_hbm, h2_hbm,
                out_hbm, wgu_f32, wd_f32, wgu_bf, wd_bf, xbuf, obuf, gsem, ssem, wsem, *, layer):
    i = pl.program_id(0)
    slot = lax.rem(i, 2)
    cnt = bc_ref[i]
    prev_cnt = jnp.where(i > 0, bc_ref[jnp.maximum(i - 1, 0)], 0)
    first = jnp.logical_and(i == 0, cnt > 0)
    changed = jnp.logical_or(i == 0, be_ref[i] != be_ref[jnp.maximum(i - 1, 0)])

    def weight_copies(e):
        return (pltpu.make_async_copy(wgu_hbm.at[layer, e], wgu_f32, wsem.at[0]),
                pltpu.make_async_copy(wd_hbm.at[layer, e], wd_f32, wsem.at[1]))

    def gather_copy(tok, r, sl):
        return pltpu.make_async_copy(
            h2_hbm.at[pl.ds(pl.multiple_of(tok * ROW_SLAB, ROW_SLAB), ROW_SLAB), :],
            xbuf.at[sl, pl.ds(r * BUF_PITCH, ROW_SLAB), :], gsem.at[sl])

    def scatter_copy(pair, r):
        tok = lax.shift_right_logical(pair, 2)
        return pltpu.make_async_copy(
            obuf.at[pl.ds(pl.multiple_of(r * BUF_PITCH, 8), ROW_SLAB), :],
            out_hbm.at[jnp.bitwise_and(pair, TOP_K - 1), pl.ds(pl.multiple_of(tok * ROW_SLAB, ROW_SLAB), ROW_SLAB), :],
            ssem.at[0])

    def wait_prev_scatter():
        @pl.when(prev_cnt == MOE_BLK)
        def _():
            for r in range(MOE_BLK):
                scatter_copy(0, r).wait()

        @pl.when(jnp.logical_and(prev_cnt > 0, prev_cnt < MOE_BLK))
        def _():
            def one(r, carry):
                scatter_copy(0, r).wait()
                return carry
            lax.fori_loop(0, prev_cnt, one, 0)

    def block(full):
        x = jnp.concatenate([xbuf[slot, pl.ds(j, MOE_BLK, stride=BUF_PITCH), :] for j in range(ROW_SLAB)],
                            axis=1).astype(BF)
        cw = 2 * D_FF // GU_CHUNKS
        per = MOE_BLK // GU_CHUNKS
        gus = []
        for c in range(GU_CHUNKS):
            gus.append(jnp.dot(x, wgu_bf[:, c * cw:(c + 1) * cw], preferred_element_type=F32)
                       + bgu_ref[0, 0, :, c * cw:(c + 1) * cw])
            for r in range(c * per, (c + 1) * per):
                gather_copy(tokn_ref[0, 0, r], r, 1 - slot).start(priority=r % 2)
        gu = jnp.concatenate(gus, axis=1)
        g = jnp.minimum(gu[:, :D_FF], SWIGLU_LIMIT)
        u = jnp.clip(gu[:, D_FF:], -SWIGLU_LIMIT, SWIGLU_LIMIT)
        act = (u + 1.0) * (g * jax.nn.sigmoid(SWIGLU_ALPHA * g))
        out = (jnp.dot(act.astype(BF), wd_bf[...], preferred_element_type=F32) + bd_ref[0, 0]) * rw_ref[...]
        wait_prev_scatter()
        for j in range(ROW_SLAB):
            obuf[pl.ds(j, MOE_BLK, stride=BUF_PITCH), :] = out[:, j * LANE:(j + 1) * LANE]
        if full:
            for r in range(MOE_BLK):
                scatter_copy(pair_ref[0, 0, r], r).start(priority=r % 2)
        else:
            def one(r, carry):
                scatter_copy(pair_ref[0, 0, r], r).start()
                return carry
            lax.fori_loop(0, cnt, one, 0)

    @pl.when(first)
    def _():
        for r in range(MOE_BLK):
            gather_copy(tokc_ref[0, 0, r], r, 0).start()

    @pl.when(i == 0)
    def _():
        for cp in weight_copies(be_ref[0]):
            cp.start()

    @pl.when(changed)
    def _():
        for cp in weight_copies(be_ref[i]):
            cp.wait()
        wgu_bf[...] = wgu_f32[...].astype(BF)
        wd_bf[...] = wd_f32[...].astype(BF)

        @pl.when(nx_ref[i] >= 0)
        def _():
            for cp in weight_copies(nx_ref[i]):
                cp.start()

    @pl.when(jnp.logical_or(first, prev_cnt > 0))
    def _():
        for r in range(MOE_BLK):
            gather_copy(0, r, slot).wait()

    @pl.when(cnt == MOE_BLK)
    def _():
        block(True)

    @pl.when(jnp.logical_and(cnt > 0, cnt < MOE_BLK))
    def _():
        block(False)

    @pl.when(cnt == 0)
    def _():
        wait_prev_scatter()


def _moe_rows(layer, n_all, blk_expert, blk_cnt, blk_next, row_tok, row_pair, row_w, h2_rows, e_w_gu, e_b_gu, e_w_down,
              e_b_down):
    n_blk = blk_expert.shape[0]
    idx_blk = lambda f: pl.BlockSpec((1, 1, MOE_BLK), f, memory_space=pltpu.SMEM)
    grid_spec = pltpu.PrefetchScalarGridSpec(
        num_scalar_prefetch=3,
        grid=(n_blk,),
        in_specs=[idx_blk(lambda i, be, bc, nx: (i, 0, 0)),
                  idx_blk(lambda i, be, bc, nx: (jnp.minimum(i + 1, n_blk - 1), 0, 0)),
                  idx_blk(lambda i, be, bc, nx: (i, 0, 0)),
                  pl.BlockSpec((MOE_BLK, 1), lambda i, be, bc, nx: (i, 0)),
                  pl.BlockSpec((1, 1, 1, 2 * D_FF), lambda i, be, bc, nx: (layer, be[i], 0, 0)),
                  pl.BlockSpec((1, 1, 1, D_MODEL), lambda i, be, bc, nx: (layer, be[i], 0, 0)),
                  pl.BlockSpec(memory_space=pl.ANY),
                  pl.BlockSpec(memory_space=pl.ANY),
                  pl.BlockSpec(memory_space=pl.ANY)],
        out_specs=pl.BlockSpec(memory_space=pl.ANY),
        scratch_shapes=[pltpu.VMEM((D_MODEL, 2 * D_FF), F32), pltpu.VMEM((D_FF, D_MODEL), F32),
                        pltpu.VMEM((D_MODEL, 2 * D_FF), BF), pltpu.VMEM((D_FF, D_MODEL), BF),
                        pltpu.VMEM((2, MOE_BLK * BUF_PITCH, LANE), F32), pltpu.VMEM((MOE_BLK * BUF_PITCH, LANE), F32),
                        pltpu.SemaphoreType.DMA((2,)), pltpu.SemaphoreType.DMA((1,)), pltpu.SemaphoreType.DMA((2,))],
    )
    tok3 = row_tok.reshape(n_blk, 1, MOE_BLK)
    return pl.pallas_call(
        functools.partial(_moe_kernel, layer=layer), grid_spec=grid_spec,
        out_shape=jax.ShapeDtypeStruct((TOP_K, n_all * ROW_SLAB, LANE), F32),
        compiler_params=_cparams(("arbitrary",), 58),
        name="moe_experts",
    )(blk_expert, blk_cnt, blk_next, tok3, tok3, row_pair.reshape(n_blk, 1, MOE_BLK), row_w, e_b_gu, e_b_down,
      e_w_gu, e_w_down, h2_rows)


def _route(top_idx, top_w):
    n_tok = top_idx.shape[0]
    n_pairs = n_tok * TOP_K
    e_flat = top_idx.reshape(-1)
    onehot = e_flat[:, None] == jnp.arange(N_EXPERTS, dtype=jnp.int32)[None, :]
    counts = jnp.sum(onehot, axis=0, dtype=jnp.int32)
    padded = (counts + MOE_BLK - 1) // MOE_BLK * MOE_BLK
    starts = jnp.cumsum(counts) - counts
    pends = jnp.cumsum(padded)
    pstarts = pends - padded
    n_rows = (n_pairs + MOE_BLK - 1) // MOE_BLK * MOE_BLK + N_EXPERTS * MOE_BLK
    n_blk = n_rows // MOE_BLK
    blk_start = jnp.arange(n_blk, dtype=jnp.int32) * MOE_BLK
    blk_expert = jnp.minimum(jnp.sum(blk_start[:, None] >= pends[None, :], axis=1, dtype=jnp.int32), N_EXPERTS - 1)
    blk_valid = (blk_start < pends[-1]).astype(jnp.int32)
    order = jnp.argsort(e_flat).astype(jnp.int32)
    j_in_blk = jnp.arange(MOE_BLK, dtype=jnp.int32)[None, :]
    j_in_e = (blk_start - pstarts[blk_expert])[:, None] + j_in_blk
    row_valid = (j_in_e < counts[blk_expert][:, None]) & (blk_valid[:, None] > 0)
    src = jnp.clip(starts[blk_expert][:, None] + j_in_e, 0, n_pairs - 1).reshape(-1)
    row_pair = order[src]
    row_valid = row_valid.reshape(-1)
    row_tok = jnp.where(row_valid, row_pair // TOP_K, 0)
    row_w = jnp.where(row_valid, top_w.reshape(-1)[row_pair], 0.0)
    row_pair = jnp.where(row_valid, row_pair, 0)
    blk_cnt = jnp.clip(counts[blk_expert] - (blk_start - pstarts[blk_expert]), 0, MOE_BLK) * blk_valid
    later = jnp.arange(n_blk)[None, :] > jnp.arange(n_blk)[:, None]
    differs = later & (blk_expert[None, :] != blk_expert[:, None])
    blk_next = jnp.where(jnp.any(differs, axis=1), blk_expert[jnp.argmax(differs, axis=1)], -1).astype(jnp.int32)
    return row_tok, row_w, row_pair, blk_expert, blk_cnt, blk_next


def _combine_kernel(x1_ref, r_ref, g2_ref, lng_ref, lnb_ref, o_ref, *, tm):
    f = None
    for k in range(TOP_K):
        fk = jnp.concatenate([r_ref[k, pl.ds(j, tm, stride=ROW_SLAB), :] for j in range(ROW_SLAB)], axis=1)
        f = fk if f is None else f + fk
    r = DN_ALPHA * x1_ref[...] + g2_ref[...] * f
    o_ref[...] = _layer_norm_rows(r, lng_ref[...], lnb_ref[...])


def _combine(x1, rows, g2, ln_g, ln_b, *, tm, row_off=0):
    n = x1.shape[0]
    assert row_off % tm == 0
    blk_off = row_off // tm
    return pl.pallas_call(
        functools.partial(_combine_kernel, tm=tm), grid=(n // tm,),
        in_specs=[pl.BlockSpec((tm, D_MODEL), lambda i: (i, 0)),
                  pl.BlockSpec((TOP_K, tm * ROW_SLAB, LANE), lambda i: (0, i + blk_off, 0)),
                  pl.BlockSpec((1, D_MODEL), lambda i: (0, 0)),
                  pl.BlockSpec((1, D_MODEL), lambda i: (0, 0)),
                  pl.BlockSpec((1, D_MODEL), lambda i: (0, 0))],
        out_specs=pl.BlockSpec((tm, D_MODEL), lambda i: (i, 0)),
        out_shape=jax.ShapeDtypeStruct((n, D_MODEL), F32),
        compiler_params=_cparams(("arbitrary",), 48),
        name="combine_ln2",
    )(x1, rows, g2, ln_g, ln_b)


def _rope_tables(n_tok):
    n_freq = ROPE_DIM // 4
    inv = (np.float32(ROPE_BASE) ** (-np.arange(n_freq, dtype=np.float32) / np.float32(n_freq))).astype(np.float32)
    t = np.arange(n_tok)
    ang_r = (t // GRID_W).astype(np.float32)[:, None] * inv
    ang_c = (t % GRID_W).astype(np.float32)[:, None] * inv
    cos64 = np.concatenate([np.cos(ang_r), np.cos(ang_r), np.cos(ang_c), np.cos(ang_c)], axis=1)
    sin64 = np.concatenate([-np.sin(ang_r), np.sin(ang_r), -np.sin(ang_c), np.sin(ang_c)], axis=1)
    cos_t = np.concatenate([cos64, cos64], axis=1).astype(np.float32)
    sin_t = np.concatenate([sin64, sin64], axis=1).astype(np.float32)
    return jnp.asarray(cos_t), jnp.asarray(sin_t)


def _row_tile(n, pref):
    return pref if n % pref == 0 else n


def kernel(x, c, ctx, c_ctx, w_ada, b_ada, w_in, b_gate, a_ln_g, a_ln_b, a_ws, a_bs, b_conv_w, b_conv_b, c_lq1, c_lk1, c_lq2, c_lk2, c_subln_g, d_sink, w_br, w_out, ln1_g, ln1_b, w_router, b_router, e_w_gu, e_b_gu, e_w_down, e_b_down, ln2_g, ln2_b):
    assert x.shape[0] == 1 and ctx.shape[0] == 1
    n, m = x.shape[1], ctx.shape[1]
    assert n % 256 == 0 and m % BLOCK == 0
    xl = x[0]
    xc = ctx[0]

    c_rows = jnp.zeros((8, D_MODEL), F32).at[0].set(c[0]).at[1].set(c_ctx)
    mods = _ada(c_rows, w_ada, b_ada)
    cos_t, sin_t = _rope_tables(n)
    cos_c = jnp.ones((m, LANE), F32)
    sin_c = jnp.zeros((m, LANE), F32)
    row2 = lambda v: v.reshape(1, -1)

    for l in range(DEPTH):
        need_ctx = l < DEPTH - 1
        lam_init = 0.8 - 0.6 * math.exp(-0.3 * l)
        sh1, sc1, g1, sh2, sc2, g2 = [row2(t) for t in jnp.split(mods[l, 0], 6)]
        csh1, csc1, cg1, csh2, csc2, cg2 = [row2(t) for t in jnp.split(mods[l, 1], 6)]
        lam = (jnp.exp(jnp.sum(c_lq1[l] * c_lk1[l])) - jnp.exp(jnp.sum(c_lq2[l] * c_lk2[l]))
               + lam_init).astype(F32).reshape(1)
        lw = {
            'b_gate': b_gate[l], 'a_ln_g': row2(a_ln_g[l]), 'a_ln_b': row2(a_ln_b[l]),
            'a_ws': a_ws[l].astype(BF),
            'a_bs_full': jnp.repeat(a_bs[l].T, BLOCK, axis=1),
            'b_conv_w': b_conv_w[l], 'b_conv_b': row2(b_conv_b[l]),
            'w_br': w_br[l].astype(BF), 'w_out': w_out[l].astype(BF),
            'ln1_g': row2(ln1_g[l]), 'ln1_b': row2(ln1_b[l]),
            'w_router': jnp.pad(w_router[l], ((0, 0), (0, LANE - N_EXPERTS))).astype(BF),
            'b_router': jnp.pad(row2(b_router[l]), ((0, 0), (0, LANE - N_EXPERTS)), constant_values=NEG_BIG),
        }

        p_lat = _proj(l, xl, sc1, sh1, w_in, cos_t, sin_t, rope=True, tm=_row_tile(n, 2048))
        p_ctx = _proj(l, xc, csc1, csh1, w_in, cos_c, sin_c, rope=False, tm=m)

        sl = lambda arr, col, w: lax.slice_in_dim(arr, col, col + w, axis=1)
        k_all = jnp.concatenate([sl(p_lat, COL_CK, 512), sl(p_ctx, COL_CK, 512)], axis=0)
        v_all = jnp.concatenate([sl(p_lat, COL_CV, 512), sl(p_ctx, COL_CV, 512)], axis=0)
        br_c = _diff_attn(lam, sl(p_lat, COL_CQ, 512), k_all, v_all, c_subln_g[l], lam_init)
        br_d = _win_attn(d_sink[l], p_lat, p_ctx)

        mod3 = jnp.zeros((8, D_MODEL), F32).at[0].set(g1[0]).at[1].set(sc2[0]).at[2].set(sh2[0])
        x1, h2, tidx, tw = _merge(p_lat, br_c, br_d, xl, mod3, lw, tm=256)
        tidx, tw = tidx[:, :TOP_K], tw[:, :TOP_K]

        if need_ctx:
            cbr_c = _diff_attn(lam, sl(p_ctx, COL_CQ, 512), sl(p_ctx, COL_CK, 512), sl(p_ctx, COL_CV, 512),
                               c_subln_g[l], lam_init)
            cbr_d = _win_attn_ctx(d_sink[l], p_ctx)
            cmod3 = jnp.zeros((8, D_MODEL), F32).at[0].set(cg1[0]).at[1].set(csc2[0]).at[2].set(csh2[0])
            xc1, hc2, ctidx, ctw = _merge(p_ctx, cbr_c, cbr_d, xc, cmod3, lw, tm=_row_tile(m, 256))
            h2 = jnp.concatenate([h2, hc2], axis=0)
            tidx = jnp.concatenate([tidx, ctidx[:, :TOP_K]], axis=0)
            tw = jnp.concatenate([tw, ctw[:, :TOP_K]], axis=0)

        n_all = h2.shape[0] // ROW_SLAB
        row_tok, row_w, row_pair, blk_expert, blk_cnt, blk_next = _route(tidx, tw)
        rows_out = _moe_rows(l, n_all, blk_expert, blk_cnt, blk_next, row_tok, row_pair, row_w.reshape(-1, 1), h2,
                             e_w_gu, e_b_gu.reshape(DEPTH, N_EXPERTS, 1, -1),
                             e_w_down, e_b_down.reshape(DEPTH, N_EXPERTS, 1, -1))

        xl = _combine(x1, rows_out, g2, row2(ln2_g[l]), row2(ln2_b[l]), tm=256)
        if need_ctx:
            xc = _combine(xc1, rows_out, cg2, row2(ln2_g[l]), row2(ln2_b[l]), tm=_row_tile(m, 256), row_off=n)

    return xl[None]
```

```python
import functools
import math

import jax
import jax.numpy as jnp
import numpy as np
from jax import lax
from jax.experimental import pallas as pl
from jax.experimental.pallas import tpu as pltpu

BF = jnp.bfloat16
F32 = jnp.float32

D_MODEL = 2048
DEPTH = 2
GRID_W = 64
BLOCK = 128
A_GROUPS = 4
C_HEADS = 4
C_HD = 64
C_VD = 128
D_HEADS = 8
D_KV_HEADS = 2
D_HD = 64
WINDOW = 128
N_BRANCH = 4
BRANCH_W = 512
PROJ_SIZES = (512, 512, 512, 512, 512, 512, 512, 512, 512, 128, 128, N_BRANCH * D_MODEL)
N_EXPERTS = 32
TOP_K = 4
D_FF = 1024
SWIGLU_LIMIT = 7.0
SWIGLU_ALPHA = 1.702
ROPE_DIM = 64
ROPE_BASE = 10000.0
LN_EPS = 1e-5
DN_ALPHA = (2 * DEPTH) ** 0.25

LANE = 128
NEG_BIG = -1e30

GL_W = N_BRANCH * D_MODEL
PROJ_TN = 256
SEG_W = 9 * 512 + 256
P_W = GL_W + SEG_W
N_SEG_TILES = SEG_W // PROJ_TN
COL_AU, COL_AV, COL_BX, COL_BB, COL_BC, COL_CQ, COL_CK, COL_CV, COL_DQ = [GL_W + 512 * s for s in range(9)]
COL_DK = GL_W + 9 * 512
COL_DV = COL_DK + 128
ROPE_FULL_TILES = (10, 11, 12, 13, 16, 17)
ROPE_PART_TILE = 18

MOE_BLK = 256
ROW_SLAB = D_MODEL // LANE
BUF_PITCH = 24


def _cparams(dims, vmem_mib):
    return pltpu.CompilerParams(dimension_semantics=dims, vmem_limit_bytes=vmem_mib * 1024 * 1024)


def _const_spec(shape):
    nd = len(shape)
    return pl.BlockSpec(shape, lambda *_: (0,) * nd, pipeline_mode=pl.Buffered(1))


def _layer_norm_rows(r, g, b):
    mu = jnp.mean(r, axis=-1, keepdims=True)
    d = r - mu
    var = jnp.mean(d * d, axis=-1, keepdims=True)
    return d * lax.rsqrt(var + LN_EPS) * g + b


def _ada_kernel(c_ref, w_ref, b_ref, o_ref):
    cs = c_ref[...]
    s = cs * jax.nn.sigmoid(cs)
    o_ref[0] = jnp.dot(s.astype(BF), w_ref[0].astype(BF), preferred_element_type=F32) + b_ref[0]


def _ada(c_rows, w_ada, b_ada):
    n_l, _, n_out = w_ada.shape
    tn = 1536
    return pl.pallas_call(
        _ada_kernel,
        grid=(n_l, n_out // tn),
        in_specs=[pl.BlockSpec((8, D_MODEL), lambda l, j: (0, 0)),
                  pl.BlockSpec((1, D_MODEL, tn), lambda l, j: (l, 0, j)),
                  pl.BlockSpec((1, 1, tn), lambda l, j: (l, 0, j))],
        out_specs=pl.BlockSpec((1, 8, tn), lambda l, j: (l, 0, j)),
        out_shape=jax.ShapeDtypeStruct((n_l, 8, n_out), F32),
        compiler_params=_cparams(("arbitrary", "arbitrary"), 40),
        name="ada",
    )(c_rows, w_ada, b_ada.reshape(n_l, 1, n_out))


def _rope_rotate(a, cos_ref, sin_ref):
    w = a.shape[1]
    lane = lax.broadcasted_iota(jnp.int32, a.shape, 1)
    first = jnp.bitwise_and(lane, 16) == 0
    swapped = jnp.where(first, pltpu.roll(a, w - 16, 1), pltpu.roll(a, 16, 1))
    reps = w // LANE
    cos = cos_ref[...]
    sin = sin_ref[...]
    if reps > 1:
        cos = jnp.concatenate([cos] * reps, axis=1)
        sin = jnp.concatenate([sin] * reps, axis=1)
    return a * cos + swapped * sin


def _proj_kernel(x_ref, sc_ref, sh_ref, w_ref, cos_ref, sin_ref, o_ref, h_scr, *, rope):
    j = pl.program_id(1)

    @pl.when(j == 0)
    def _():
        h_scr[...] = (x_ref[...] * (1.0 + sc_ref[...]) + sh_ref[...]).astype(BF)

    acc = jnp.dot(h_scr[...], w_ref[0].astype(BF), preferred_element_type=F32)
    if not rope:
        o_ref[...] = acc.astype(BF)
        return

    full = functools.reduce(jnp.logical_or, [j == t for t in ROPE_FULL_TILES])
    part = j == ROPE_PART_TILE

    @pl.when(full)
    def _():
        o_ref[...] = _rope_rotate(acc, cos_ref, sin_ref).astype(BF)

    @pl.when(part)
    def _():
        o_ref[:, :LANE] = _rope_rotate(acc[:, :LANE], cos_ref, sin_ref).astype(BF)
        o_ref[:, LANE:] = acc[:, LANE:].astype(BF)

    @pl.when(jnp.logical_not(jnp.logical_or(full, part)))
    def _():
        o_ref[...] = acc.astype(BF)


def _proj(layer, x2d, sc, sh, w_in, cos_t, sin_t, *, rope, tm):
    n = x2d.shape[0]
    kern = functools.partial(_proj_kernel, rope=rope)
    n_gl_tiles = GL_W // PROJ_TN
    out_tile = lambda j: jnp.where(j < N_SEG_TILES, j + n_gl_tiles, j - N_SEG_TILES)
    return pl.pallas_call(
        kern,
        grid=(n // tm, P_W // PROJ_TN),
        in_specs=[pl.BlockSpec((tm, D_MODEL), lambda i, j: (i, 0), pipeline_mode=pl.Buffered(1)),
                  pl.BlockSpec((1, D_MODEL), lambda i, j: (0, 0)),
                  pl.BlockSpec((1, D_MODEL), lambda i, j: (0, 0)),
                  pl.BlockSpec((1, D_MODEL, PROJ_TN), lambda i, j: (layer, 0, j)),
                  pl.BlockSpec((tm, LANE), lambda i, j: (i, 0)),
                  pl.BlockSpec((tm, LANE), lambda i, j: (i, 0))],
        out_specs=pl.BlockSpec((tm, PROJ_TN), lambda i, j: (i, out_tile(j))),
        out_shape=jax.ShapeDtypeStruct((n, P_W), BF),
        scratch_shapes=[pltpu.VMEM((tm, D_MODEL), BF)],
        compiler_params=_cparams(("arbitrary", "arbitrary"), 56),
        name="proj_rope" if rope else "proj_ctx",
    )(x2d, sc, sh, w_in, cos_t, sin_t)


_ACC_ROWS = 32
_ONES_ROWS = 16
LOG2E = math.log2(math.e)


def _diff_attn_kernel(lam_ref, qt_ref, qtn_ref, k_ref, vt_ref, g_ref, o_ref, qbd_scr, qbdn_scr, m_scr, acc_scr,
                      s0_scr, s_scr, mx0_scr, mx_scr, *, tq, tk, sub, n_chunks, post_scale):
    w = 2 * tq
    n_sub = tk // sub
    i = pl.program_id(1)

    def block_diag(q_ref, dst):
        qt = q_ref[...].astype(F32) * (C_HD ** -0.5 * LOG2E)
        row = lax.broadcasted_iota(jnp.int32, qt.shape, 0)
        dst[...] = jnp.concatenate([jnp.where(row < C_HD, qt, 0.0), jnp.where(row >= C_HD, qt, 0.0)],
                                   axis=1).astype(BF)

    def scores_sub(c, qbd, s_dst, j):
        k = k_ref[0, pl.ds(pl.multiple_of(c * tk + j * sub, sub), sub), :]
        s = jnp.dot(k, qbd[...], preferred_element_type=F32)
        s_dst[j * sub:(j + 1) * sub, :] = s
        return jnp.max(s.reshape(sub // _ACC_ROWS, _ACC_ROWS, w), axis=0)

    def step(c, s_src, mx_src, nxt, qbd, s_dst, mx_dst):
        m_old = m_scr[...]
        m_new = jnp.maximum(m_old, jnp.max(mx_src[...], axis=0, keepdims=True))
        alpha = jnp.exp2(m_old - m_new)
        pv = None
        mx = None
        for j in range(n_sub):
            p = jnp.exp2(s_src[j * sub:(j + 1) * sub, :] - m_new).astype(BF)
            mj = scores_sub(nxt, qbd, s_dst, j)
            mx = mj if mx is None else jnp.maximum(mx, mj)
            d = jnp.dot(vt_ref[0, c, :, j * sub:(j + 1) * sub], p, preferred_element_type=F32)
            pv = d if pv is None else pv + d
        acc_scr[...] = acc_scr[...] * alpha + pv
        m_scr[...] = m_new
        mx_dst[...] = mx

    @pl.when(i == 0)
    def _():
        block_diag(qt_ref, qbd_scr)
        mx = None
        for j in range(n_sub):
            mj = scores_sub(0, qbd_scr, s0_scr, j)
            mx = mj if mx is None else jnp.maximum(mx, mj)
        mx0_scr[...] = mx

    block_diag(qtn_ref, qbdn_scr)
    m_scr[...] = jnp.full(m_scr.shape, -jnp.inf, F32)
    acc_scr[...] = jnp.zeros(acc_scr.shape, F32)

    slot_s = lambda c: s_scr.at[c % 2]
    slot_mx = lambda c: mx_scr.at[c % 2]
    last = n_chunks - 1

    def static_step(c):
        src = (s0_scr, mx0_scr) if c == 0 else (slot_s(c), slot_mx(c))
        if c == last:
            step(c, src[0], src[1], 0, qbdn_scr, s0_scr, mx0_scr)
        else:
            step(c, src[0], src[1], c + 1, qbd_scr, slot_s(c + 1), slot_mx(c + 1))

    static_step(0)
    n_pairs = max((n_chunks - 3) // 2, 0)

    def pair(jp, carry):
        c = 1 + 2 * jp
        step(c, slot_s(1), slot_mx(1), c + 1, qbd_scr, slot_s(0), slot_mx(0))
        step(c + 1, slot_s(0), slot_mx(0), c + 2, qbd_scr, slot_s(1), slot_mx(1))
        return carry

    if n_pairs > 0:
        lax.fori_loop(0, n_pairs, pair, 0)
    for c in range(1 + 2 * n_pairs, n_chunks):
        static_step(c)
    qbd_scr[...] = qbdn_scr[...]

    o = acc_scr[:C_VD, :] / acc_scr[C_VD:C_VD + 1, :]
    od = o[:, :tq] - lam_ref[0] * o[:, tq:]
    ms = jnp.mean(od * od, axis=0, keepdims=True)
    on = od * lax.rsqrt(ms + LN_EPS) * g_ref[...] * post_scale
    o_ref[...] = on.T.astype(BF)


def _pick_tk(n_k):
    for tk in (1280, 1024, 768, 512, 256, 128):
        if n_k % tk == 0:
            return tk
    raise ValueError(f"unsupported key count {n_k}")


def _diff_attn(lam, q, k_all, v_all, subln_g, lam_init):
    n = q.shape[0]
    n_k = k_all.shape[0]
    tq = 256
    tk = _pick_tk(n_k)
    n_chunks = n_k // tk
    qt = q.T
    kh = k_all.reshape(n_k, C_HEADS, C_VD).transpose(1, 0, 2)
    vt = v_all.reshape(n_chunks, tk, C_HEADS, C_VD).transpose(2, 0, 3, 1)
    vt = jnp.concatenate([vt, jnp.ones((C_HEADS, n_chunks, _ONES_ROWS, tk), BF)], axis=2)
    sub = 256 if tk % 256 == 0 else tk
    kern = functools.partial(_diff_attn_kernel, tq=tq, tk=tk, sub=sub, n_chunks=n_chunks,
                             post_scale=1.0 - lam_init)
    return pl.pallas_call(
        kern,
        grid=(C_HEADS, n // tq),
        in_specs=[pl.BlockSpec(memory_space=pltpu.SMEM),
                  pl.BlockSpec((C_VD, tq), lambda h, i: (h, i)),
                  pl.BlockSpec((C_VD, tq), lambda h, i: (h, jnp.minimum(i + 1, n // tq - 1))),
                  pl.BlockSpec((1, n_k, C_VD), lambda h, i: (h, 0, 0)),
                  pl.BlockSpec((1, n_chunks, C_VD + _ONES_ROWS, tk), lambda h, i: (h, 0, 0, 0)),
                  pl.BlockSpec((C_VD, 1), lambda h, i: (0, 0))],
        out_specs=pl.BlockSpec((tq, C_VD), lambda h, i: (i, h)),
        out_shape=jax.ShapeDtypeStruct((n, C_HEADS * C_VD), BF),
        scratch_shapes=[pltpu.VMEM((C_VD, 2 * tq), BF), pltpu.VMEM((C_VD, 2 * tq), BF),
                        pltpu.VMEM((1, 2 * tq), F32), pltpu.VMEM((C_VD + _ONES_ROWS, 2 * tq), F32),
                        pltpu.VMEM((tk, 2 * tq), F32), pltpu.VMEM((2, tk, 2 * tq), F32),
                        pltpu.VMEM((_ACC_ROWS, 2 * tq), F32), pltpu.VMEM((2, _ACC_ROWS, 2 * tq), F32)],
        compiler_params=_cparams(("arbitrary", "arbitrary"), 48),
        name="diff_attn",
    )(lam, qt, qt, kh, vt, subln_g.reshape(C_VD, 1))


def _win_heads(q, kb, vb, valid, sink_ref):
    outs = []
    grp = D_HEADS // D_KV_HEADS
    for h in range(D_HEADS):
        kh = h // grp
        qh = q[:, h * D_HD:(h + 1) * D_HD]
        k_h = kb[:, kh * D_HD:(kh + 1) * D_HD]
        v_h = vb[:, kh * D_HD:(kh + 1) * D_HD]
        s = lax.dot_general(qh, k_h, (((1,), (1,)), ((), ())), preferred_element_type=F32) * (D_HD ** -0.5)
        if valid is not None:
            s = jnp.where(valid, s, NEG_BIG)
        sk = sink_ref[h]
        m = jnp.maximum(jnp.max(s, axis=-1, keepdims=True), sk)
        e = jnp.exp(s - m)
        l = jnp.sum(e, axis=-1, keepdims=True) + jnp.exp(sk - m)
        p = (e / l).astype(BF)
        outs.append(jnp.dot(p, v_h, preferred_element_type=F32))
    return jnp.concatenate(outs, axis=1).astype(BF)


def _win_attn_ctx_kernel(sink_ref, q_ref, kx_ref, vx_ref, o_ref):
    o_ref[...] = _win_heads(q_ref[...], kx_ref[...], vx_ref[...], None, sink_ref)


def _win_attn_t_kernel(sink_ref, qt_ref, kp_ref, kc_ref, kn_ref, kx_ref, vtp_ref, vtc_ref, vtn_ref, vtx_ref, o_ref, *,
                       n_tok, n_ctx):
    i = pl.program_id(0)
    grp = D_HEADS // D_KV_HEADS
    kfull = jnp.concatenate([kp_ref[...], kc_ref[...], kn_ref[...], kx_ref[...]], axis=0)
    vtfull = jnp.concatenate([vtp_ref[...], vtc_ref[...], vtn_ref[...], vtx_ref[...]], axis=1)
    n_keys = 3 * BLOCK + n_ctx
    kk = lax.broadcasted_iota(jnp.int32, (n_keys, BLOCK), 0)
    qq = lax.broadcasted_iota(jnp.int32, (n_keys, BLOCK), 1)
    kpos = (i - 1) * BLOCK + kk
    in_band = (jnp.abs(kk - BLOCK - qq) <= WINDOW) & (kpos >= 0) & (kpos < n_tok)
    bias = jnp.where((kk >= 3 * BLOCK) | in_band, 0.0, NEG_BIG)
    bias = jnp.concatenate([bias] * grp, axis=1)
    qt = (qt_ref[...].astype(F32) * (D_HD ** -0.5)).astype(BF)
    zeros = jnp.zeros((D_HD, grp * BLOCK), BF)
    for g in range(D_KV_HEADS):
        q64 = jnp.concatenate([qt[(g * grp + hh) * D_HD:(g * grp + hh + 1) * D_HD, :] for hh in range(grp)], axis=1)
        qbd = jnp.concatenate([q64, zeros] if g == 0 else [zeros, q64], axis=0)
        s = jnp.dot(kfull, qbd, preferred_element_type=F32) + bias
        sk = sink_ref[g]
        m = jnp.maximum(jnp.max(s, axis=0, keepdims=True), sk)
        e = jnp.exp(s - m)
        l = jnp.sum(e, axis=0, keepdims=True) + jnp.exp(sk - m)
        p = (e * (1.0 / l)).astype(BF)
        ot = jnp.dot(vtfull, p, preferred_element_type=F32)
        og = ot[g * D_HD:(g + 1) * D_HD, :]
        for hh in range(grp):
            h = g * grp + hh
            o_ref[h * D_HD:(h + 1) * D_HD, :] = og[:, hh * BLOCK:(hh + 1) * BLOCK].astype(BF)


def _win_attn(sink, p_lat, p_ctx):
    n = p_lat.shape[0]
    m = p_ctx.shape[0]
    nb = n // BLOCK
    grp = D_HEADS // D_KV_HEADS
    ck = COL_DK // LANE
    qt = lax.slice_in_dim(p_lat, COL_DQ, COL_DQ + 512, axis=1).T
    vt = lax.slice_in_dim(p_lat, COL_DV, COL_DV + LANE, axis=1).T
    vtx = lax.slice_in_dim(p_ctx, COL_DV, COL_DV + LANE, axis=1).T
    sink_rows = jnp.repeat(sink.astype(F32).reshape(D_KV_HEADS, grp), BLOCK, axis=1).reshape(D_KV_HEADS, 1, grp * BLOCK)
    kern = functools.partial(_win_attn_t_kernel, n_tok=n, n_ctx=m)
    prev = lambda i: jnp.maximum(i - 1, 0)
    nxt = lambda i: jnp.minimum(i + 1, nb - 1)
    o_t = pl.pallas_call(
        kern,
        grid=(nb,),
        in_specs=[pl.BlockSpec((D_KV_HEADS, 1, grp * BLOCK), lambda i: (0, 0, 0)),
                  pl.BlockSpec((512, BLOCK), lambda i: (0, i)),
                  pl.BlockSpec((BLOCK, LANE), lambda i: (prev(i), ck)),
                  pl.BlockSpec((BLOCK, LANE), lambda i: (i, ck)),
                  pl.BlockSpec((BLOCK, LANE), lambda i: (nxt(i), ck)),
                  pl.BlockSpec((m, LANE), lambda i: (0, ck)),
                  pl.BlockSpec((LANE, BLOCK), lambda i: (0, prev(i))),
                  pl.BlockSpec((LANE, BLOCK), lambda i: (0, i)),
                  pl.BlockSpec((LANE, BLOCK), lambda i: (0, nxt(i))),
                  pl.BlockSpec((LANE, m), lambda i: (0, 0))],
        out_specs=pl.BlockSpec((512, BLOCK), lambda i: (0, i)),
        out_shape=jax.ShapeDtypeStruct((512, n), BF),
        compiler_params=_cparams(("arbitrary",), 32),
        name="win_attn",
    )(sink_rows, qt, p_lat, p_lat, p_lat, p_ctx, vt, vt, vt, vtx)
    return o_t.T


def _win_attn_ctx(sink, p_ctx):
    m = p_ctx.shape[0]
    ck, cv, cq = COL_DK // LANE, COL_DV // LANE, COL_DQ // 512
    return pl.pallas_call(
        _win_attn_ctx_kernel,
        grid=(m // BLOCK,),
        in_specs=[pl.BlockSpec(memory_space=pltpu.SMEM),
                  pl.BlockSpec((BLOCK, 512), lambda i: (i, cq)),
                  pl.BlockSpec((m, LANE), lambda i: (0, ck)),
                  pl.BlockSpec((m, LANE), lambda i: (0, cv))],
        out_specs=pl.BlockSpec((BLOCK, 512), lambda i: (i, 0)),
        out_shape=jax.ShapeDtypeStruct((m, 512), BF),
        compiler_params=_cparams(("arbitrary",), 32),
        name="win_attn_ctx",
    )(sink, p_ctx, p_ctx, p_ctx)


def _merge_kernel(gl_ref, au_ref, av_ref, bx_ref, bb_ref, bc_ref, bxp_ref, bcp_ref, bxn_ref, bcn_ref,
                  brc_ref, brd_ref, x_ref, mod_ref, bgate_ref, alng_ref, alnb_ref, ws_ref, bs_ref, cw_ref, cb_ref,
                  wbr_ref, wout_ref, ln1g_ref, ln1b_ref, wr_ref, brt_ref,
                  x1_ref, h2_ref, tidx_ref, tw_ref, *, tm):
    i = pl.program_id(0)
    last = pl.num_programs(0) - 1

    u = jax.nn.gelu(au_ref[...].astype(F32), approximate=True)
    v = jax.nn.gelu(av_ref[...].astype(F32), approximate=True)
    vn = _layer_norm_rows(v, alng_ref[...], alnb_ref[...]).astype(BF)
    blocks = []
    for b in range(tm // BLOCK):
        cols = []
        for g in range(A_GROUPS):
            vbg = vn[b * BLOCK:(b + 1) * BLOCK, g * LANE:(g + 1) * LANE]
            cols.append(jnp.dot(ws_ref[g], vbg, preferred_element_type=F32))
        blocks.append(jnp.concatenate(cols, axis=1) + bs_ref[...])
    mixed = jnp.concatenate(blocks, axis=0) if len(blocks) > 1 else blocks[0]
    br_a = u * mixed

    z = bc_ref[...].astype(F32) * bx_ref[...].astype(F32)
    z_prev = bcp_ref[7:8, :].astype(F32) * bxp_ref[7:8, :].astype(F32) * (i > 0).astype(F32)
    z_next = bcn_ref[0:1, :].astype(F32) * bxn_ref[0:1, :].astype(F32) * (i < last).astype(F32)
    row = lax.broadcasted_iota(jnp.int32, z.shape, 0)
    z_up = jnp.where(row == 0, z_prev, pltpu.roll(z, 1, 0))
    z_dn = jnp.where(row == tm - 1, z_next, pltpu.roll(z, tm - 1, 0))
    y_conv = cw_ref[0:1, :] * z_up + cw_ref[1:2, :] * z + cw_ref[2:3, :] * z_dn + cb_ref[...]
    br_b = bb_ref[...].astype(F32) * y_conv

    branches = (br_a.astype(BF), br_b.astype(BF), brc_ref[...], brd_ref[...])
    merged = None
    for g in range(N_BRANCH):
        pr = jnp.dot(branches[g], wbr_ref[g], preferred_element_type=F32)
        gate = jax.nn.sigmoid(gl_ref[:, g * D_MODEL:(g + 1) * D_MODEL].astype(F32) + bgate_ref[g:g + 1, :])
        merged = gate * pr if merged is None else merged + gate * pr
    y = jnp.dot(merged.astype(BF), wout_ref[...], preferred_element_type=F32)

    r = DN_ALPHA * x_ref[...] + mod_ref[0:1, :] * y
    x1 = _layer_norm_rows(r, ln1g_ref[...], ln1b_ref[...])
    x1_ref[...] = x1
    h2f = x1 * (1.0 + mod_ref[1:2, :]) + mod_ref[2:3, :]
    for j in range(ROW_SLAB):
        h2_ref[pl.ds(j, tm, stride=ROW_SLAB), :] = h2f[:, j * LANE:(j + 1) * LANE]
    h2 = h2f.astype(BF)
    logits = jnp.dot(h2, wr_ref[...], preferred_element_type=F32) + brt_ref[...]

    lane = lax.broadcasted_iota(jnp.int32, logits.shape, 1)
    vals, idxs = [], []
    cur = logits
    for _ in range(TOP_K):
        mx = jnp.max(cur, axis=-1, keepdims=True)
        ix = jnp.min(jnp.where(cur == mx, lane, LANE), axis=-1, keepdims=True)
        vals.append(mx)
        idxs.append(ix)
        cur = jnp.where(lane == ix, -jnp.inf, cur)
    es = [jnp.exp(vk - vals[0]) for vk in vals]
    den = es[0] + es[1] + es[2] + es[3]
    tidx = jnp.zeros(logits.shape, jnp.int32)
    tw = jnp.zeros(logits.shape, F32)
    for k in range(TOP_K):
        tidx = jnp.where(lane == k, idxs[k], tidx)
        tw = jnp.where(lane == k, es[k] / den, tw)
    tidx_ref[...] = tidx
    tw_ref[...] = tw


def _merge(p, br_c, br_d, x2d, mod3, lw, *, tm):
    n = p.shape[0]
    nt = n // tm
    r8 = tm // 8
    c512 = lambda col: col // 512
    kern = functools.partial(_merge_kernel, tm=tm)
    prev8 = lambda i: (jnp.maximum(i * r8 - 1, 0))
    next8 = lambda i: (jnp.minimum((i + 1) * r8, n // 8 - 1))
    seg = lambda col: pl.BlockSpec((tm, 512), lambda i: (i, c512(col)))
    in_specs = [
        pl.BlockSpec((tm, GL_W), lambda i: (i, 0)),
        seg(COL_AU), seg(COL_AV), seg(COL_BX), seg(COL_BB), seg(COL_BC),
        pl.BlockSpec((8, 512), lambda i: (prev8(i), c512(COL_BX))),
        pl.BlockSpec((8, 512), lambda i: (prev8(i), c512(COL_BC))),
        pl.BlockSpec((8, 512), lambda i: (next8(i), c512(COL_BX))),
        pl.BlockSpec((8, 512), lambda i: (next8(i), c512(COL_BC))),
        pl.BlockSpec((tm, 512), lambda i: (i, 0)),
        pl.BlockSpec((tm, 512), lambda i: (i, 0)),
        pl.BlockSpec((tm, D_MODEL), lambda i: (i, 0)),
        _const_spec((8, D_MODEL)),
        _const_spec((N_BRANCH, D_MODEL)),
        _const_spec((1, 512)), _const_spec((1, 512)),
        _const_spec((A_GROUPS, BLOCK, BLOCK)),
        _const_spec((BLOCK, 512)),
        _const_spec((3, 512)), _const_spec((1, 512)),
        _const_spec((N_BRANCH, BRANCH_W, D_MODEL)),
        _const_spec((D_MODEL, D_MODEL)),
        _const_spec((1, D_MODEL)), _const_spec((1, D_MODEL)),
        _const_spec((D_MODEL, LANE)), _const_spec((1, LANE)),
    ]
    out_specs = [pl.BlockSpec((tm, D_MODEL), lambda i: (i, 0)),
                 pl.BlockSpec((tm * ROW_SLAB, LANE), lambda i: (i, 0)),
                 pl.BlockSpec((tm, LANE), lambda i: (i, 0)),
                 pl.BlockSpec((tm, LANE), lambda i: (i, 0))]
    out_shape = [jax.ShapeDtypeStruct((n, D_MODEL), F32), jax.ShapeDtypeStruct((n * ROW_SLAB, LANE), F32),
                 jax.ShapeDtypeStruct((n, LANE), jnp.int32), jax.ShapeDtypeStruct((n, LANE), F32)]
    return pl.pallas_call(
        kern, grid=(nt,), in_specs=in_specs, out_specs=out_specs, out_shape=out_shape,
        compiler_params=_cparams(("arbitrary",), 56),
        name="merge",
    )(p, p, p, p, p, p, p, p, p, p, br_c, br_d, x2d, mod3, lw['b_gate'], lw['a_ln_g'], lw['a_ln_b'],
      lw['a_ws'], lw['a_bs_full'], lw['b_conv_w'], lw['b_conv_b'], lw['w_br'], lw['w_out'],
      lw['ln1_g'], lw['ln1_b'], lw['w_router'], lw['b_router'])


GU_CHUNKS = 4


def _moe_kernel(be_ref, bc_ref, nx_ref, tokc_ref, tokn_ref, pair_ref, rw_ref, bgu_ref, bd_ref, wgu_hbm, wd_hbm, h2_hbm,
                out_hbm, wgu_f32, wd_f32, wgu_bf, wd_bf, xbuf, obuf, gsem, ssem, wsem, *, layer):
    i = pl.program_id(0)
    slot = lax.rem(i, 2)
    cnt = bc_ref[i]
    prev_cnt = jnp.where(i > 0, bc_ref[jnp.maximum(i - 1, 0)], 0)
    first = jnp.logical_and(i == 0, cnt > 0)
    changed = jnp.logical_or(i == 0, be_ref[i] != be_ref[jnp.maximum(i - 1, 0)])

    def weight_copies(e):
        return (pltpu.make_async_copy(wgu_hbm.at[layer, e], wgu_f32, wsem.at[0]),
                pltpu.make_async_copy(wd_hbm.at[layer, e], wd_f32, wsem.at[1]))

    def gather_copy(tok, r, sl):
        return pltpu.make_async_copy(
            h2_hbm.at[pl.ds(pl.multiple_of(tok * ROW_SLAB, ROW_SLAB), ROW_SLAB), :],
            xbuf.at[sl, pl.ds(r * BUF_PITCH, ROW_SLAB), :], gsem.at[sl])

    def scatter_copy(pair, r):
        tok = lax.shift_right_logical(pair, 2)
        return pltpu.make_async_copy(
            obuf.at[pl.ds(pl.multiple_of(r * BUF_PITCH, 8), ROW_SLAB), :],
            out_hbm.at[jnp.bitwise_and(pair, TOP_K - 1), pl.ds(pl.multiple_of(tok * ROW_SLAB, ROW_SLAB), ROW_SLAB), :],
            ssem.at[0])

    def wait_prev_scatter():
        @pl.when(prev_cnt == MOE_BLK)
        def _():
            for r in range(MOE_BLK):
                scatter_copy(0, r).wait()

        @pl.when(jnp.logical_and(prev_cnt > 0, prev_cnt < MOE_BLK))
        def _():
            def one(r, carry):
                scatter_copy(0, r).wait()
                return carry
            lax.fori_loop(0, prev_cnt, one, 0)

    def block(full):
        x = jnp.concatenate([xbuf[slot, pl.ds(j, MOE_BLK, stride=BUF_PITCH), :] for j in range(ROW_SLAB)],
                            axis=1).astype(BF)
        cw = 2 * D_FF // GU_CHUNKS
        per = MOE_BLK // GU_CHUNKS
        gus = []
        for c in range(GU_CHUNKS):
            gus.append(jnp.dot(x, wgu_bf[:, c * cw:(c + 1) * cw], preferred_element_type=F32)
                       + bgu_ref[0, 0, :, c * cw:(c + 1) * cw])
            for r in range(c * per, (c + 1) * per):
                gather_copy(tokn_ref[0, 0, r], r, 1 - slot).start(priority=r % 2)
        gu = jnp.concatenate(gus, axis=1)
        g = jnp.minimum(gu[:, :D_FF], SWIGLU_LIMIT)
        u = jnp.clip(gu[:, D_FF:], -SWIGLU_LIMIT, SWIGLU_LIMIT)
        act = (u + 1.0) * (g * jax.nn.sigmoid(SWIGLU_ALPHA * g))
        out = (jnp.dot(act.astype(BF), wd_bf[...], preferred_element_type=F32) + bd_ref[0, 0]) * rw_ref[...]
        wait_prev_scatter()
        for j in range(ROW_SLAB):
            obuf[pl.ds(j, MOE_BLK, stride=BUF_PITCH), :] = out[:, j * LANE:(j + 1) * LANE]
        if full:
            for r in range(MOE_BLK):
                scatter_copy(pair_ref[0, 0, r], r).start(priority=r % 2)
        else:
            def one(r, carry):
                scatter_copy(pair_ref[0, 0, r], r).start()
                return carry
            lax.fori_loop(0, cnt, one, 0)

    @pl.when(first)
    def _():
        for r in range(MOE_BLK):
            gather_copy(tokc_ref[0, 0, r], r, 0).start()

    @pl.when(i == 0)
    def _():
        for cp in weight_copies(be_ref[0]):
            cp.start()

    @pl.when(changed)
    def _():
        for cp in weight_copies(be_ref[i]):
            cp.wait()
        wgu_bf[...] = wgu_f32[...].astype(BF)
        wd_bf[...] = wd_f32[...].astype(BF)

        @pl.when(nx_ref[i] >= 0)
        def _():
            for cp in weight_copies(nx_ref[i]):
                cp.start()

    @pl.when(jnp.logical_or(first, prev_cnt > 0))
    def _():
        for r in range(MOE_BLK):
            gather_copy(0, r, slot).wait()

    @pl.when(cnt == MOE_BLK)
    def _():
        block(True)

    @pl.when(jnp.logical_and(cnt > 0, cnt < MOE_BLK))
    def _():
        block(False)

    @pl.when(cnt == 0)
    def _():
        wait_prev_scatter()


def _moe_rows(layer, n_all, blk_expert, blk_cnt, blk_next, row_tok, row_pair, row_w, h2_rows, e_w_gu, e_b_gu, e_w_down,
              e_b_down):
    n_blk = blk_expert.shape[0]
    idx_blk = lambda f: pl.BlockSpec((1, 1, MOE_BLK), f, memory_space=pltpu.SMEM)
    grid_spec = pltpu.PrefetchScalarGridSpec(
        num_scalar_prefetch=3,
        grid=(n_blk,),
        in_specs=[idx_blk(lambda i, be, bc, nx: (i, 0, 0)),
                  idx_blk(lambda i, be, bc, nx: (jnp.minimum(i + 1, n_blk - 1), 0, 0)),
                  idx_blk(lambda i, be, bc, nx: (i, 0, 0)),
                  pl.BlockSpec((MOE_BLK, 1), lambda i, be, bc, nx: (i, 0)),
                  pl.BlockSpec((1, 1, 1, 2 * D_FF), lambda i, be, bc, nx: (layer, be[i], 0, 0)),
                  pl.BlockSpec((1, 1, 1, D_MODEL), lambda i, be, bc, nx: (layer, be[i], 0, 0)),
                  pl.BlockSpec(memory_space=pl.ANY),
                  pl.BlockSpec(memory_space=pl.ANY),
                  pl.BlockSpec(memory_space=pl.ANY)],
        out_specs=pl.BlockSpec(memory_space=pl.ANY),
        scratch_shapes=[pltpu.VMEM((D_MODEL, 2 * D_FF), F32), pltpu.VMEM((D_FF, D_MODEL), F32),
                        pltpu.VMEM((D_MODEL, 2 * D_FF), BF), pltpu.VMEM((D_FF, D_MODEL), BF),
                        pltpu.VMEM((2, MOE_BLK * BUF_PITCH, LANE), F32), pltpu.VMEM((MOE_BLK * BUF_PITCH, LANE), F32),
                        pltpu.SemaphoreType.DMA((2,)), pltpu.SemaphoreType.DMA((1,)), pltpu.SemaphoreType.DMA((2,))],
    )
    tok3 = row_tok.reshape(n_blk, 1, MOE_BLK)
    return pl.pallas_call(
        functools.partial(_moe_kernel, layer=layer), grid_spec=grid_spec,
        out_shape=jax.ShapeDtypeStruct((TOP_K, n_all * ROW_SLAB, LANE), F32),
        compiler_params=_cparams(("arbitrary",), 58),
        name="moe_experts",
    )(blk_expert, blk_cnt, blk_next, tok3, tok3, row_pair.reshape(n_blk, 1, MOE_BLK), row_w, e_b_gu, e_b_down,
      e_w_gu, e_w_down, h2_rows)


def _route(top_idx, top_w):
    n_tok = top_idx.shape[0]
    n_pairs = n_tok * TOP_K
    e_flat = top_idx.reshape(-1)
    onehot = e_flat[:, None] == jnp.arange(N_EXPERTS, dtype=jnp.int32)[None, :]
    counts = jnp.sum(onehot, axis=0, dtype=jnp.int32)
    padded = (counts + MOE_BLK - 1) // MOE_BLK * MOE_BLK
    starts = jnp.cumsum(counts) - counts
    pends = jnp.cumsum(padded)
    pstarts = pends - padded
    n_rows = (n_pairs + MOE_BLK - 1) // MOE_BLK * MOE_BLK + N_EXPERTS * MOE_BLK
    n_blk = n_rows // MOE_BLK
    blk_start = jnp.arange(n_blk, dtype=jnp.int32) * MOE_BLK
    blk_expert = jnp.minimum(jnp.sum(blk_start[:, None] >= pends[None, :], axis=1, dtype=jnp.int32), N_EXPERTS - 1)
    blk_valid = (blk_start < pends[-1]).astype(jnp.int32)
    order = jnp.argsort(e_flat).astype(jnp.int32)
    j_in_blk = jnp.arange(MOE_BLK, dtype=jnp.int32)[None, :]
    j_in_e = (blk_start - pstarts[blk_expert])[:, None] + j_in_blk
    row_valid = (j_in_e < counts[blk_expert][:, None]) & (blk_valid[:, None] > 0)
    src = jnp.clip(starts[blk_expert][:, None] + j_in_e, 0, n_pairs - 1).reshape(-1)
    row_pair = order[src]
    row_valid = row_valid.reshape(-1)
    row_tok = jnp.where(row_valid, row_pair // TOP_K, 0)
    row_w = jnp.where(row_valid, top_w.reshape(-1)[row_pair], 0.0)
    row_pair = jnp.where(row_valid, row_pair, 0)
    blk_cnt = jnp.clip(counts[blk_expert] - (blk_start - pstarts[blk_expert]), 0, MOE_BLK) * blk_valid
    later = jnp.arange(n_blk)[None, :] > jnp.arange(n_blk)[:, None]
    differs = later & (blk_expert[None, :] != blk_expert[:, None])
    blk_next = jnp.where(jnp.any(differs, axis=1), blk_expert[jnp.argmax(differs, axis=1)], -1).astype(jnp.int32)
    return row_tok, row_w, row_pair, blk_expert, blk_cnt, blk_next


def _combine_kernel(x1_ref, r_ref, g2_ref, lng_ref, lnb_ref, o_ref, *, tm):
    f = None
    for k in range(TOP_K):
        fk = jnp.concatenate([r_ref[k, pl.ds(j, tm, stride=ROW_SLAB), :] for j in range(ROW_SLAB)], axis=1)
        f = fk if f is None else f + fk
    r = DN_ALPHA * x1_ref[...] + g2_ref[...] * f
    o_ref[...] = _layer_norm_rows(r, lng_ref[...], lnb_ref[...])


def _combine(x1, rows, g2, ln_g, ln_b, *, tm, row_off=0):
    n = x1.shape[0]
    assert row_off % tm == 0
    blk_off = row_off // tm
    return pl.pallas_call(
        functools.partial(_combine_kernel, tm=tm), grid=(n // tm,),
        in_specs=[pl.BlockSpec((tm, D_MODEL), lambda i: (i, 0)),
                  pl.BlockSpec((TOP_K, tm * ROW_SLAB, LANE), lambda i: (0, i + blk_off, 0)),
                  pl.BlockSpec((1, D_MODEL), lambda i: (0, 0)),
                  pl.BlockSpec((1, D_MODEL), lambda i: (0, 0)),
                  pl.BlockSpec((1, D_MODEL), lambda i: (0, 0))],
        out_specs=pl.BlockSpec((tm, D_MODEL), lambda i: (i, 0)),
        out_shape=jax.ShapeDtypeStruct((n, D_MODEL), F32),
        compiler_params=_cparams(("arbitrary",), 48),
        name="combine_ln2",
    )(x1, rows, g2, ln_g, ln_b)


def _rope_tables(n_tok):
    n_freq = ROPE_DIM // 4
    inv = (np.float32(ROPE_BASE) ** (-np.arange(n_freq, dtype=np.float32) / np.float32(n_freq))).astype(np.float32)
    t = np.arange(n_tok)
    ang_r = (t // GRID_W).astype(np.float32)[:, None] * inv
    ang_c = (t % GRID_W).astype(np.float32)[:, None] * inv
    cos64 = np.concatenate([np.cos(ang_r), np.cos(ang_r), np.cos(ang_c), np.cos(ang_c)], axis=1)
    sin64 = np.concatenate([-np.sin(ang_r), np.sin(ang_r), -np.sin(ang_c), np.sin(ang_c)], axis=1)
    cos_t = np.concatenate([cos64, cos64], axis=1).astype(np.float32)
    sin_t = np.concatenate([sin64, sin64], axis=1).astype(np.float32)
    return jnp.asarray(cos_t), jnp.asarray(sin_t)


def _row_tile(n, pref):
    return pref if n % pref == 0 else n


def kernel(x, c, ctx, c_ctx, w_ada, b_ada, w_in, b_gate, a_ln_g, a_ln_b, a_ws, a_bs, b_conv_w, b_conv_b, c_lq1, c_lk1, c_lq2, c_lk2, c_subln_g, d_sink, w_br, w_out, ln1_g, ln1_b, w_router, b_router, e_w_gu, e_b_gu, e_w_down, e_b_down, ln2_g, ln2_b):
    assert x.shape[0] == 1 and ctx.shape[0] == 1
    n, m = x.shape[1], ctx.shape[1]
    assert n % 256 == 0 and m % BLOCK == 0
    xl = x[0]
    xc = ctx[0]

    c_rows = jnp.zeros((8, D_MODEL), F32).at[0].set(c[0]).at[1].set(c_ctx)
    mods = _ada(c_rows, w_ada, b_ada)
    cos_t, sin_t = _rope_tables(n)
    cos_c = jnp.ones((m, LANE), F32)
    sin_c = jnp.zeros((m, LANE), F32)
    row2 = lambda v: v.reshape(1, -1)

    for l in range(DEPTH):
        need_ctx = l < DEPTH - 1
        lam_init = 0.8 - 0.6 * math.exp(-0.3 * l)
        sh1, sc1, g1, sh2, sc2, g2 = [row2(t) for t in jnp.split(mods[l, 0], 6)]
        csh1, csc1, cg1, csh2, csc2, cg2 = [row2(t) for t in jnp.split(mods[l, 1], 6)]
        lam = (jnp.exp(jnp.sum(c_lq1[l] * c_lk1[l])) - jnp.exp(jnp.sum(c_lq2[l] * c_lk2[l]))
               + lam_init).astype(F32).reshape(1)
        lw = {
            'b_gate': b_gate[l], 'a_ln_g': row2(a_ln_g[l]), 'a_ln_b': row2(a_ln_b[l]),
            'a_ws': a_ws[l].astype(BF),
            'a_bs_full': jnp.repeat(a_bs[l].T, BLOCK, axis=1),
            'b_conv_w': b_conv_w[l], 'b_conv_b': row2(b_conv_b[l]),
            'w_br': w_br[l].astype(BF), 'w_out': w_out[l].astype(BF),
            'ln1_g': row2(ln1_g[l]), 'ln1_b': row2(ln1_b[l]),
            'w_router': jnp.pad(w_router[l], ((0, 0), (0, LANE - N_EXPERTS))).astype(BF),
            'b_router': jnp.pad(row2(b_router[l]), ((0, 0), (0, LANE - N_EXPERTS)), constant_values=NEG_BIG),
        }

        p_lat = _proj(l, xl, sc1, sh1, w_in, cos_t, sin_t, rope=True, tm=_row_tile(n, 2048))
        p_ctx = _proj(l, xc, csc1, csh1, w_in, cos_c, sin_c, rope=False, tm=m)

        sl = lambda arr, col, w: lax.slice_in_dim(arr, col, col + w, axis=1)
        k_all = jnp.concatenate([sl(p_lat, COL_CK, 512), sl(p_ctx, COL_CK, 512)], axis=0)
        v_all = jnp.concatenate([sl(p_lat, COL_CV, 512), sl(p_ctx, COL_CV, 512)], axis=0)
        br_c = _diff_attn(lam, sl(p_lat, COL_CQ, 512), k_all, v_all, c_subln_g[l], lam_init)
        br_d = _win_attn(d_sink[l], p_lat, p_ctx)

        mod3 = jnp.zeros((8, D_MODEL), F32).at[0].set(g1[0]).at[1].set(sc2[0]).at[2].set(sh2[0])
        x1, h2, tidx, tw = _merge(p_lat, br_c, br_d, xl, mod3, lw, tm=256)
        tidx, tw = tidx[:, :TOP_K], tw[:, :TOP_K]

        if need_ctx:
            cbr_c = _diff_attn(lam, sl(p_ctx, COL_CQ, 512), sl(p_ctx, COL_CK, 512), sl(p_ctx, COL_CV, 512),
                               c_subln_g[l], lam_init)
            cbr_d = _win_attn_ctx(d_sink[l], p_ctx)
            cmod3 = jnp.zeros((8, D_MODEL), F32).at[0].set(cg1[0]).at[1].set(csc2[0]).at[2].set(csh2[0])
            xc1, hc2, ctidx, ctw = _merge(p_ctx, cbr_c, cbr_d, xc, cmod3, lw, tm=_row_tile(m, 256))
            h2 = jnp.concatenate([h2, hc2], axis=0)
            tidx = jnp.concatenate([tidx, ctidx[:, :TOP_K]], axis=0)
            tw = jnp.concatenate([tw, ctw[:, :TOP_K]], axis=0)

        n_all = h2.shape[0] // ROW_SLAB
        row_tok, row_w, row_pair, blk_expert, blk_cnt, blk_next = _route(tidx, tw)
        rows_out = _moe_rows(l, n_all, blk_expert, blk_cnt, blk_next, row_tok, row_pair, row_w.reshape(-1, 1), h2,
                             e_w_gu, e_b_gu.reshape(DEPTH, N_EXPERTS, 1, -1),
                             e_w_down, e_b_down.reshape(DEPTH, N_EXPERTS, 1, -1))

        xl = _combine(x1, rows_out, g2, row2(ln2_g[l]), row2(ln2_b[l]), tm=256)
        if need_ctx:
            xc = _combine(xc1, rows_out, cg2, row2(ln2_g[l]), row2(ln2_b[l]), tm=_row_tile(m, 256), row_off=n)

    return xl[None]
```
